```python
import math
import jax, jax.numpy as jnp
from jax import lax
import numpy as np

D_MODEL = 1024
BATCH = 4
SEQ = 4096
DEPTH = 1

N_HEADS = 8
HEAD_DIM = 64
N_KV_HEADS = 2
GROUP = N_HEADS // N_KV_HEADS
ATTN_WIDTH = N_HEADS * HEAD_DIM
KV_W = N_KV_HEADS * HEAD_DIM
LRU_WIDTH = D_MODEL - ATTN_WIDTH
LRU_BLOCKS = 8
LRU_BLOCK_DIM = LRU_WIDTH // LRU_BLOCKS
CONV_WIDTH = 4
LRU_C = 8.0
CMP_LEN = 32
CMP_STRIDE = 16
CMP_HIDDEN = 256
SEL_LEN = 64
SEL_TOPK = 16
N_LOCAL_BLOCKS = 2
WINDOW = 512
Q_BLOCK = 128
N_BRANCH = 3
FORCE_LOCAL = 2.0e4
FORCE_INIT = 1.0e4
REL_BUCKETS = 32
REL_MAX_DIST = 128
PEER_HEADS = 8
PEER_N_KEYS = 128
PEER_EXPERTS = PEER_N_KEYS * PEER_N_KEYS
PEER_D_KEY = 256
PEER_TOPK = 16
PEER_CHUNK = 128
PLE_DIM = 256
EPS = 1e-6
NEG = -1e30
IN_SPLITS = [ATTN_WIDTH] + [KV_W] * 6 + [N_HEADS * N_BRANCH, LRU_WIDTH, LRU_WIDTH]
IN_COLS = sum(IN_SPLITS)

kernel_name = 'hymba_nsa_rglru_peer_layer'


def rmsnorm(x, g):
    xf = x.astype(jnp.float32)
    y = xf * lax.rsqrt(jnp.mean(xf * xf, axis=-1, keepdims=True) + EPS)
    return (y * g.astype(jnp.float32)).astype(x.dtype)


def masked_softmax(logits, mask):
    logits = jnp.where(mask, logits.astype(jnp.float32), NEG)
    p = jax.nn.softmax(logits, axis=-1)
    return jnp.where(mask, p, 0.0)


def rel_bucket(dist):
    n = jnp.maximum(dist, 0)
    max_exact = REL_BUCKETS // 2
    nf = jnp.maximum(n, max_exact).astype(jnp.float32)
    large = max_exact + (jnp.log(nf / max_exact) / math.log(REL_MAX_DIST / max_exact)
                         * (REL_BUCKETS - max_exact)).astype(jnp.int32)
    large = jnp.minimum(large, REL_BUCKETS - 1)
    return jnp.where(n < max_exact, n, large)


def nsa_mixer(q, kc, vc, ks, vs, kw, vw, gates, cmp_k_pe, cmp_k_w1, cmp_k_w2,
              cmp_v_pe, cmp_v_w1, cmp_v_w2, rel_table):
    B, T = q.shape[0], q.shape[1]
    dt = q.dtype
    f32 = jnp.float32
    table_kg = rel_table.T.reshape(N_KV_HEADS, GROUP, REL_BUCKETS)

    def heads(t, n):
        return t.reshape(B, T, n, HEAD_DIM).transpose(0, 2, 1, 3)

    qh = heads(q, N_HEADS).reshape(B, N_KV_HEADS, GROUP, T, HEAD_DIM) * (HEAD_DIM ** -0.5)
    kc, vc, ks, vs, kw, vw = [heads(t, N_KV_HEADS) for t in (kc, vc, ks, vs, kw, vw)]
    t_pos = jnp.arange(T, dtype=jnp.int32)

    n_cmp = (T - CMP_LEN) // CMP_STRIDE + 1
    cmp_start = jnp.arange(n_cmp, dtype=jnp.int32) * CMP_STRIDE
    tok_idx = cmp_start[:, None] + jnp.arange(CMP_LEN, dtype=jnp.int32)[None, :]

    def compress(kv, pe, w1, w2):
        blk = kv[:, :, tok_idx] + pe
        blk = blk.reshape(B, N_KV_HEADS, n_cmp, CMP_LEN * HEAD_DIM)
        return jax.nn.gelu(blk @ w1) @ w2

    k_cmp = compress(kc, cmp_k_pe, cmp_k_w1, cmp_k_w2)
    v_cmp = compress(vc, cmp_v_pe, cmp_v_w1, cmp_v_w2)
    dist_c = t_pos[:, None] - (cmp_start + CMP_LEN - 1)[None, :]
    bias_c = table_kg[:, :, rel_bucket(dist_c)]
    logit_c = jnp.einsum('bkgtd,bkcd->bkgtc', qh, k_cmp).astype(f32) + bias_c
    p_c = masked_softmax(logit_c, dist_c >= 0)
    o_c = jnp.einsum('bkgtc,bkcd->bkgtd', p_c.astype(dt), v_cmp)

    n_sel = T // SEL_LEN
    sel_start = jnp.arange(n_sel, dtype=jnp.int32) * SEL_LEN
    overlap = jnp.clip(jnp.minimum(cmp_start[:, None] + CMP_LEN, sel_start[None, :] + SEL_LEN)
                       - jnp.maximum(cmp_start[:, None], sel_start[None, :]), 0, None)
    overlap = overlap.astype(f32) / CMP_LEN
    imp = jnp.einsum('bkgtc,cs->bkts', p_c, overlap)
    d_blk = (t_pos // SEL_LEN)[:, None] - jnp.arange(n_sel, dtype=jnp.int32)[None, :]
    local = (d_blk >= 0) & (d_blk < N_LOCAL_BLOCKS)
    initial = (jnp.arange(n_sel) == 0)[None, :]
    imp = jnp.where(local, FORCE_LOCAL, jnp.where(initial, FORCE_INIT, jnp.where(d_blk >= 0, imp, -1.0)))
    k_top = min(SEL_TOPK, n_sel)
    _, sel_idx = lax.top_k(imp, k_top)

    ks_blk = ks.reshape(B, N_KV_HEADS, n_sel, SEL_LEN, HEAD_DIM)
    vs_blk = vs.reshape(B, N_KV_HEADS, n_sel, SEL_LEN, HEAD_DIM)
    kw_pad = jnp.pad(kw, ((0, 0), (0, 0), (WINDOW, 0), (0, 0)))
    vw_pad = jnp.pad(vw, ((0, 0), (0, 0), (WINDOW, 0), (0, 0)))
    span = WINDOW + Q_BLOCK
    dist_w = (jnp.arange(Q_BLOCK, dtype=jnp.int32)[:, None]
              - jnp.arange(span, dtype=jnp.int32)[None, :] + WINDOW)
    band = (dist_w >= 0) & (dist_w < WINDOW)
    bias_w = table_kg[:, :, rel_bucket(dist_w)]
    bi = jnp.arange(B)[:, None, None, None]
    hi = jnp.arange(N_KV_HEADS)[None, :, None, None]
    hk6 = jnp.arange(N_KV_HEADS)[None, :, None, None, None, None]
    gi6 = jnp.arange(GROUP)[None, None, :, None, None, None]
    n_keys_sel = k_top * SEL_LEN

    def chunk_fn(args):
        q_c, idx_c, q0 = args
        tq = q0 + jnp.arange(Q_BLOCK, dtype=jnp.int32)
        k_g = ks_blk[bi, hi, idx_c]
        v_g = vs_blk[bi, hi, idx_c]
        kpos = idx_c[..., None] * SEL_LEN + jnp.arange(SEL_LEN, dtype=jnp.int32)
        dist_s = tq[None, None, :, None, None] - kpos
        bias_s = table_kg[hk6, gi6, rel_bucket(dist_s)[:, :, None]]
        logit_s = jnp.einsum('bkgqd,bkqnsd->bkgqns', q_c, k_g).astype(f32) + bias_s
        logit_s = logit_s.reshape(B, N_KV_HEADS, GROUP, Q_BLOCK, n_keys_sel)
        mask_s = (dist_s >= 0).reshape(B, N_KV_HEADS, 1, Q_BLOCK, n_keys_sel)
        p_s = masked_softmax(logit_s, mask_s)
        o_s = jnp.einsum('bkgqn,bkqnd->bkgqd', p_s.astype(dt),
                         v_g.reshape(B, N_KV_HEADS, Q_BLOCK, n_keys_sel, HEAD_DIM))
        k_win = lax.dynamic_slice_in_dim(kw_pad, q0, span, axis=2)
        v_win = lax.dynamic_slice_in_dim(vw_pad, q0, span, axis=2)
        kpos_w = q0 - WINDOW + jnp.arange(span, dtype=jnp.int32)
        mask_w = band & (kpos_w >= 0)[None, :]
        logit_w = jnp.einsum('bkgqd,bksd->bkgqs', q_c, k_win).astype(f32) + bias_w
        p_w = masked_softmax(logit_w, mask_w)
        o_w = jnp.einsum('bkgqs,bksd->bkgqd', p_w.astype(dt), v_win)
        return o_s, o_w

    n_chunk = T // Q_BLOCK
    q_chunks = jnp.moveaxis(qh.reshape(B, N_KV_HEADS, GROUP, n_chunk, Q_BLOCK, HEAD_DIM), 3, 0)
    idx_chunks = jnp.moveaxis(sel_idx.reshape(B, N_KV_HEADS, n_chunk, Q_BLOCK, k_top), 2, 0)
    q0s = jnp.arange(n_chunk, dtype=jnp.int32) * Q_BLOCK
    o_s, o_w = lax.map(chunk_fn, (q_chunks, idx_chunks, q0s))
    o_s = jnp.moveaxis(o_s, 0, 3).reshape(B, N_HEADS, T, HEAD_DIM)
    o_w = jnp.moveaxis(o_w, 0, 3).reshape(B, N_HEADS, T, HEAD_DIM)
    o_c = o_c.reshape(B, N_HEADS, T, HEAD_DIM)

    g = jax.nn.sigmoid(gates.reshape(B, T, N_HEADS, N_BRANCH)).transpose(0, 2, 1, 3)
    o = g[..., 0:1] * o_c + g[..., 1:2] * o_s + g[..., 2:3] * o_w
    return o.transpose(0, 2, 1, 3).reshape(B, T, ATTN_WIDTH)


def rglru_mixer(xr, xg, conv_w, conv_b, wa, ba, wx, bx, lam):
    B, T, _ = xr.shape
    dt = xr.dtype
    xc = lax.conv_general_dilated(xr, conv_w, window_strides=(1,), padding=[(CONV_WIDTH - 1, 0)],
                                  dimension_numbers=('NWC', 'WIO', 'NWC'),
                                  feature_group_count=LRU_WIDTH) + conv_b
    xb = xc.reshape(B, T, LRU_BLOCKS, LRU_BLOCK_DIM)
    r = jax.nn.sigmoid(jnp.einsum('btnc,ncd->btnd', xb, wa).reshape(B, T, LRU_WIDTH) + ba)
    i = jax.nn.sigmoid(jnp.einsum('btnc,ncd->btnd', xb, wx).reshape(B, T, LRU_WIDTH) + bx)
    log_a = -LRU_C * r.astype(jnp.float32) * jax.nn.softplus(-lam.astype(jnp.float32))
    a = jnp.exp(log_a)
    b = jnp.sqrt(-jnp.expm1(2.0 * log_a)) * (i * xc).astype(jnp.float32)

    def combine(left, right):
        a1, b1 = left
        a2, b2 = right
        return a1 * a2, a2 * b1 + b2

    _, h = lax.associative_scan(combine, (a, b), axis=1)
    return h.astype(dt) * jax.nn.gelu(xg)


def peer_ffn(xn, wq, sub_keys, u_tab, v_tab):
    B, T, D = xn.shape
    dt = xn.dtype
    xt = xn.reshape(-1, PEER_CHUNK, D)

    def chunk(xc):
        q = (xc @ wq).reshape(PEER_CHUNK, PEER_HEADS, 2, PEER_D_KEY // 2)
        s = jnp.einsum('chpd,pkd->chpk', q, sub_keys).astype(jnp.float32)
        s1, i1 = lax.top_k(s[:, :, 0], PEER_TOPK)
        s2, i2 = lax.top_k(s[:, :, 1], PEER_TOPK)
        cand = (s1[..., :, None] + s2[..., None, :]).reshape(PEER_CHUNK, PEER_HEADS, PEER_TOPK * PEER_TOPK)
        sc, ci = lax.top_k(cand, PEER_TOPK)
        e = (jnp.take_along_axis(i1, ci // PEER_TOPK, axis=-1) * PEER_N_KEYS
             + jnp.take_along_axis(i2, ci % PEER_TOPK, axis=-1))
        g = jax.nn.softmax(sc, axis=-1).astype(dt)
        act = jax.nn.gelu(jnp.einsum('cd,chkd->chk', xc, u_tab[e]))
        return jnp.einsum('chk,chkd->cd', g * act, v_tab[e])

    return lax.map(chunk, xt).reshape(B, T, D)


def setup_inputs(seed: int = 0) -> dict:
    key = jax.random.key(seed)
    ks = jax.random.split(key, 32)
    f32 = jnp.float32

    def nrm(k, shape, scale):
        return jax.random.normal(k, shape, f32) * scale

    def gain(k, shape):
        return 1.0 + 0.02 * jax.random.normal(k, shape, f32)

    a8 = jax.random.uniform(ks[17], (DEPTH, LRU_WIDTH), f32, 0.9, 0.999)
    a = a8 ** (1.0 / LRU_C)
    lam = jnp.log(a) - jnp.log1p(-a)
    return {
        'x': nrm(ks[0], (BATCH, SEQ, D_MODEL), 1.0),
        'p': nrm(ks[1], (DEPTH, BATCH, SEQ, PLE_DIM), 1.0),
        'attn_norm': gain(ks[2], (DEPTH, D_MODEL)),
        'w_in': nrm(ks[3], (DEPTH, D_MODEL, IN_COLS), D_MODEL ** -0.5),
        'cmp_k_pe': nrm(ks[4], (DEPTH, CMP_LEN, HEAD_DIM), 0.3),
        'cmp_k_w1': nrm(ks[5], (DEPTH, CMP_LEN * HEAD_DIM, CMP_HIDDEN), (CMP_LEN * HEAD_DIM) ** -0.5),
        'cmp_k_w2': nrm(ks[6], (DEPTH, CMP_HIDDEN, HEAD_DIM), CMP_HIDDEN ** -0.5),
        'cmp_v_pe': nrm(ks[7], (DEPTH, CMP_LEN, HEAD_DIM), 0.3),
        'cmp_v_w1': nrm(ks[8], (DEPTH, CMP_LEN * HEAD_DIM, CMP_HIDDEN), (CMP_LEN * HEAD_DIM) ** -0.5),
        'cmp_v_w2': nrm(ks[9], (DEPTH, CMP_HIDDEN, HEAD_DIM), CMP_HIDDEN ** -0.5),
        'rel_table': nrm(ks[10], (REL_BUCKETS, N_HEADS), 0.5),
        'conv_w': nrm(ks[11], (DEPTH, CONV_WIDTH, 1, LRU_WIDTH), CONV_WIDTH ** -0.5),
        'conv_b': nrm(ks[12], (DEPTH, LRU_WIDTH), 0.01),
        'lru_wa': nrm(ks[13], (DEPTH, LRU_BLOCKS, LRU_BLOCK_DIM, LRU_BLOCK_DIM), LRU_BLOCK_DIM ** -0.5),
        'lru_ba': nrm(ks[14], (DEPTH, LRU_WIDTH), 0.01),
        'lru_wx': nrm(ks[15], (DEPTH, LRU_BLOCKS, LRU_BLOCK_DIM, LRU_BLOCK_DIM), LRU_BLOCK_DIM ** -0.5),
        'lru_bx': nrm(ks[16], (DEPTH, LRU_WIDTH), 0.01),
        'lru_lambda': lam,
        'grp_norm_attn': gain(ks[18], (DEPTH, ATTN_WIDTH)),
        'grp_norm_lru': gain(ks[19], (DEPTH, LRU_WIDTH)),
        'w_out': nrm(ks[20], (DEPTH, D_MODEL, D_MODEL), D_MODEL ** -0.5),
        'ffn_norm': gain(ks[21], (DEPTH, D_MODEL)),
        'peer_wq': nrm(ks[22], (DEPTH, D_MODEL, PEER_HEADS * PEER_D_KEY), D_MODEL ** -0.5),
        'peer_subkeys': nrm(ks[23], (DEPTH, 2, PEER_N_KEYS, PEER_D_KEY // 2), (PEER_D_KEY // 2) ** -0.5),
        'peer_u': nrm(ks[24], (DEPTH, PEER_EXPERTS, D_MODEL), D_MODEL ** -0.5),
        'peer_v': nrm(ks[25], (DEPTH, PEER_EXPERTS, D_MODEL), 0.3),
        'ple_norm': gain(ks[26], (DEPTH, D_MODEL)),
        'ple_wgate': nrm(ks[27], (DEPTH, D_MODEL, D_MODEL), D_MODEL ** -0.5),
        'ple_bgate': nrm(ks[28], (DEPTH, D_MODEL), 0.01),
        'ple_proj': nrm(ks[29], (DEPTH, PLE_DIM, D_MODEL), PLE_DIM ** -0.5),
        'final_norm': gain(ks[30], (D_MODEL,)),
    }


def reference(x, p, attn_norm, w_in, cmp_k_pe, cmp_k_w1, cmp_k_w2, cmp_v_pe, cmp_v_w1, cmp_v_w2,
              rel_table, conv_w, conv_b, lru_wa, lru_ba, lru_wx, lru_bx, lru_lambda,
              grp_norm_attn, grp_norm_lru, w_out, ffn_norm, peer_wq, peer_subkeys, peer_u, peer_v,
              ple_norm, ple_wgate, ple_bgate, ple_proj, final_norm):
    h = x
    split_at = [int(v) for v in np.cumsum(IN_SPLITS)[:-1]]
    for i in range(DEPTH):
        xn = rmsnorm(h, attn_norm[i])
        proj = xn @ w_in[i]
        q, kc, vc, ks, vs, kw, vw, gates, xr, xg = jnp.split(proj, split_at, axis=-1)
        a_out = nsa_mixer(q, kc, vc, ks, vs, kw, vw, gates, cmp_k_pe[i], cmp_k_w1[i], cmp_k_w2[i],
                          cmp_v_pe[i], cmp_v_w1[i], cmp_v_w2[i], rel_table)
        l_out = rglru_mixer(xr, xg, conv_w[i], conv_b[i], lru_wa[i], lru_ba[i], lru_wx[i], lru_bx[i],
                            lru_lambda[i])
        mixed = jnp.concatenate([rmsnorm(a_out, grp_norm_attn[i]), rmsnorm(l_out, grp_norm_lru[i])], axis=-1)
        h = h + mixed @ w_out[i]
        h = h + peer_ffn(rmsnorm(h, ffn_norm[i]), peer_wq[i], peer_subkeys[i], peer_u[i], peer_v[i])
        gate = jax.nn.sigmoid(rmsnorm(h, ple_norm[i]) @ ple_wgate[i] + ple_bgate[i])
        h = h + gate * (p[i] @ ple_proj[i])
    return rmsnorm(h, final_norm)
```

```python
import functools
import math

import numpy as np
import jax
import jax.numpy as jnp
from jax import lax
from jax.experimental import pallas as pl
from jax.experimental.pallas import tpu as pltpu

F32 = jnp.float32
BF16 = jnp.bfloat16

N_HEADS = 8
HEAD_DIM = 64
N_KV_HEADS = 2
GROUP = N_HEADS // N_KV_HEADS
LRU_C = 8.0
CONV_WIDTH = 4
CMP_LEN = 32
CMP_STRIDE = 16
SEL_LEN = 64
SEL_TOPK = 16
N_LOCAL_BLOCKS = 2
WINDOW = 512
FORCE_LOCAL = 2.0e4
FORCE_INIT = 1.0e4
REL_BUCKETS = 32
REL_MAX_DIST = 128
PEER_HEADS = 8
PEER_N_KEYS = 128
PEER_TOPK = 16
EPS = 1e-6
NEG = -1e30
REMOVED = -3.0e38

LANES = 128
VMEM_LIMIT = 56 * 1024 * 1024

ATT_TILE = 128
N_GATE_COLS = GROUP * 3


def _dot(a, b):
    return jnp.dot(a, b, preferred_element_type=F32)


def _dot_nt(a, b):
    return lax.dot_general(a, b, (((1,), (1,)), ((), ())), preferred_element_type=F32)


def _rms(x, g):
    return x * lax.rsqrt(jnp.mean(x * x, axis=-1, keepdims=True) + EPS) * g


def _params(sem, vmem=VMEM_LIMIT):
    return pltpu.CompilerParams(dimension_semantics=sem, vmem_limit_bytes=vmem)


def _in_proj_kernel(x_ref, g_ref, wf_ref, wb_ref, xrg_ref, kvc_ref, gates_ref, q4_ref, kvsw_ref):
    xb = _rms(x_ref[...], g_ref[...]).astype(BF16)
    yf = _dot(xb, wf_ref[...])
    xrg_ref[...] = yf[:, :1024]
    for j in range(4):
        kvc_ref[j] = yf[:, 1024 + 64 * j:1024 + 64 * (j + 1)]
    gates_ref[0] = yf[:, 1280:1408]
    gates_ref[1] = yf[:, 1408:1536]
    yb = _dot(xb, wb_ref[...])
    for h in range(N_HEADS):
        q4_ref[h] = (yb[:, 64 * h:64 * (h + 1)] * (HEAD_DIM ** -0.5)).astype(BF16)
    for j in range(8):
        kvsw_ref[j] = yb[:, 512 + 64 * j:512 + 64 * (j + 1)].astype(BF16)


def _in_proj(x2, g, wf, wb, tm):
    n, d = x2.shape
    return pl.pallas_call(
        _in_proj_kernel,
        grid=(n // tm,),
        in_specs=[
            pl.BlockSpec((tm, d), lambda i: (i, 0)),
            pl.BlockSpec((1, d), lambda i: (0, 0)),
            pl.BlockSpec(wf.shape, lambda i: (0, 0)),
            pl.BlockSpec(wb.shape, lambda i: (0, 0)),
        ],
        out_specs=[
            pl.BlockSpec((tm, 1024), lambda i: (i, 0)),
            pl.BlockSpec((4, tm, 64), lambda i: (0, i, 0)),
            pl.BlockSpec((2, tm, LANES), lambda i: (0, i, 0)),
            pl.BlockSpec((8, tm, 64), lambda i: (0, i, 0)),
            pl.BlockSpec((8, tm, 64), lambda i: (0, i, 0)),
        ],
        out_shape=[
            jax.ShapeDtypeStruct((n, 1024), F32),
            jax.ShapeDtypeStruct((4, n, 64), F32),
            jax.ShapeDtypeStruct((2, n, LANES), F32),
            jax.ShapeDtypeStruct((8, n, 64), BF16),
            jax.ShapeDtypeStruct((8, n, 64), BF16),
        ],
        compiler_params=_params(("parallel",)),
        name="in_proj",
    )(x2, g, wf, wb)


def _compress_kernel(r_ref, pe_ref, w1_ref, w2_ref, o_ref):
    r = r_ref[0, 0]
    n_rows = r.shape[0]
    half = r.shape[1]
    pe = pe_ref[0]
    top = _dot((r + pe[:, :half]).astype(BF16), w1_ref[0, :half, :])
    bot = _dot((r + pe[:, half:]).astype(BF16), w1_ref[0, half:, :])
    hid = top + pltpu.roll(bot, n_rows - 1, 0)
    act = jax.nn.gelu(hid)
    out = _dot(act.astype(BF16), w2_ref[0])
    row = lax.broadcasted_iota(jnp.int32, out.shape, 0)
    o_ref[0, 0] = jnp.where(row < n_rows - 1, out, 0.0).astype(BF16)


def _compress(r4, pe2, w1s, w2s):
    four, b, n_rows, width = r4.shape
    return pl.pallas_call(
        _compress_kernel,
        grid=(four, b),
        in_specs=[
            pl.BlockSpec((1, 1, n_rows, width), lambda j, bb: (j, bb, 0, 0)),
            pl.BlockSpec((1, 1, 2 * width), lambda j, bb: (j // 2, 0, 0)),
            pl.BlockSpec((1, 2 * width, w1s.shape[2]), lambda j, bb: (j // 2, 0, 0)),
            pl.BlockSpec((1, w2s.shape[1], 64), lambda j, bb: (j // 2, 0, 0)),
        ],
        out_specs=pl.BlockSpec((1, 1, n_rows, 64), lambda j, bb: (j, bb, 0, 0)),
        out_shape=jax.ShapeDtypeStruct((four, b, n_rows, 64), BF16),
        compiler_params=_params(("parallel", "parallel")),
        name="compress",
    )(r4, pe2, w1s, w2s)


def _bucket_breakpoints():
    n = np.arange(0, 4 * REL_MAX_DIST)
    max_exact = REL_BUCKETS // 2
    nf = np.maximum(n, max_exact).astype(np.float32)
    large = max_exact + (np.log(nf / max_exact) / np.float32(math.log(REL_MAX_DIST / max_exact))
                         * (REL_BUCKETS - max_exact)).astype(np.int32)
    large = np.minimum(large, REL_BUCKETS - 1)
    bucket = np.where(n < max_exact, n, large)
    return [int(np.argmax(bucket >= j)) for j in range(1, REL_BUCKETS)]


_BREAKS = _bucket_breakpoints()


def _bias_of_dist(dist, tab_ref, h):
    v = jnp.full(dist.shape, tab_ref[0, h], F32)
    for j, bp in enumerate(_BREAKS, start=1):
        v = jnp.where(dist >= bp, tab_ref[j, h], v)
    v = v - tab_ref[REL_BUCKETS - 1, h]
    return jnp.where(dist >= 0, v, NEG)


def _bias_near_kernel(tab_ref, o_ref):
    h = pl.program_id(0)
    t = ATT_TILE
    r = lax.broadcasted_iota(jnp.int32, (t, t), 0)
    c = lax.broadcasted_iota(jnp.int32, (t, t), 1)
    for delta in range(2):
        o_ref[0, delta] = _bias_of_dist(r - c + t * delta, tab_ref, h)


def _bias_near(rel_table):
    t = ATT_TILE
    return pl.pallas_call(
        _bias_near_kernel,
        grid=(N_HEADS,),
        in_specs=[pl.BlockSpec(memory_space=pltpu.SMEM)],
        out_specs=pl.BlockSpec((1, 2, t, t), lambda h: (h, 0, 0, 0)),
        out_shape=jax.ShapeDtypeStruct((N_HEADS, 2, t, t), F32),
        compiler_params=_params(("arbitrary",)),
        name="bias_near",
    )(rel_table)


def _bias_cmp_kernel(tab_ref, o_ref, *, n_cmp):
    h = pl.program_id(0)
    i = pl.program_id(1)
    tq, nc = o_ref.shape[1], o_ref.shape[2]
    t = lax.broadcasted_iota(jnp.int32, (tq, nc), 0) + i * tq
    c = lax.broadcasted_iota(jnp.int32, (tq, nc), 1)
    dist = t - (c * CMP_STRIDE + CMP_LEN - 1)
    o_ref[0] = jnp.where(c < n_cmp, _bias_of_dist(dist, tab_ref, h), NEG)


def _bias_cmp(rel_table, seq, n_rows, tq):
    return pl.pallas_call(
        functools.partial(_bias_cmp_kernel, n_cmp=n_rows - 1),
        grid=(N_HEADS, seq // tq),
        in_specs=[pl.BlockSpec(memory_space=pltpu.SMEM)],
        out_specs=pl.BlockSpec((1, tq, n_rows), lambda h, i: (h, i, 0)),
        out_shape=jax.ShapeDtypeStruct((N_HEADS, seq, n_rows), F32),
        compiler_params=_params(("arbitrary", "arbitrary")),
        name="bias_cmp",
    )(rel_table)


def _cmp_attn_kernel(q_ref, kc_ref, vc_ref, addc_ref, ovlt_ref, eye_ref, oc_ref, sel_ref):
    i = pl.program_id(1)
    tq = q_ref.shape[1]
    n_sel = ovlt_ref.shape[0]
    kc = kc_ref[0, 0]
    vc = vc_ref[0, 0]
    t_col = lax.broadcasted_iota(jnp.int32, (tq, 1), 0) + i * tq
    row_ok = (t_col >= CMP_LEN - 1).astype(F32)
    psum = None
    for hh in range(GROUP):
        logit = _dot_nt(q_ref[hh], kc) + addc_ref[hh]
        m = jnp.max(logit, axis=-1, keepdims=True)
        e = jnp.exp(logit - m)
        p = e / jnp.sum(e, axis=-1, keepdims=True)
        oc_ref[:, 64 * hh:64 * (hh + 1)] = _dot(p.astype(BF16), vc) * row_ok
        psum = p if psum is None else psum + p
    psum = psum * row_ok
    hi = psum.astype(BF16)
    lo = (psum - hi.astype(F32)).astype(BF16)
    ovlt = ovlt_ref[...]
    imp = _dot_nt(ovlt, hi) + _dot_nt(ovlt, lo)
    s_idx = lax.broadcasted_iota(jnp.int32, (n_sel, tq), 0)
    t_blk = (lax.broadcasted_iota(jnp.int32, (n_sel, tq), 1) + i * tq) // SEL_LEN
    d_blk = t_blk - s_idx
    local = (d_blk >= 0) & (d_blk < N_LOCAL_BLOCKS)
    v = jnp.where(local, FORCE_LOCAL,
                  jnp.where(s_idx == 0, FORCE_INIT, jnp.where(d_blk >= 0, imp, -1.0)))
    rows = s_idx.astype(F32)
    sel = jnp.zeros((n_sel, tq), F32)
    for _ in range(min(SEL_TOPK, n_sel)):
        m = jnp.max(v, axis=0, keepdims=True)
        first = jnp.min(jnp.where(v == m, rows, float(n_sel)), axis=0, keepdims=True)
        pick = rows == first
        sel = jnp.where(pick, 1.0, sel)
        v = jnp.where(pick, REMOVED, v)
    sel_ref[0] = _dot_nt(eye_ref[...], sel.astype(BF16)).astype(BF16)


def _cmp_attn(q4, cmp4, addc, ovlt, batch, seq):
    tq = ATT_TILE
    nt = seq // tq
    n_rows = cmp4.shape[2]
    n_sel = ovlt.shape[0]
    eye = jnp.eye(tq, dtype=BF16)
    return pl.pallas_call(
        _cmp_attn_kernel,
        grid=(batch * N_KV_HEADS, nt),
        in_specs=[
            pl.BlockSpec((GROUP, tq, 64), lambda g, i: (g % 2, (g // 2) * nt + i, 0)),
            pl.BlockSpec((1, 1, n_rows, 64), lambda g, i: (g % 2, g // 2, 0, 0)),
            pl.BlockSpec((1, 1, n_rows, 64), lambda g, i: (2 + g % 2, g // 2, 0, 0)),
            pl.BlockSpec((GROUP, tq, n_rows), lambda g, i: (g % 2, i, 0)),
            pl.BlockSpec(ovlt.shape, lambda g, i: (0, 0)),
            pl.BlockSpec((tq, tq), lambda g, i: (0, 0)),
        ],
        out_specs=[
            pl.BlockSpec((tq, GROUP * 64), lambda g, i: ((g // 2) * nt + i, g % 2)),
            pl.BlockSpec((1, tq, n_sel), lambda g, i: (g, i, 0)),
        ],
        out_shape=[
            jax.ShapeDtypeStruct((batch * seq, N_HEADS * 64), F32),
            jax.ShapeDtypeStruct((batch * N_KV_HEADS, seq, n_sel), BF16),
        ],
        compiler_params=_params(("parallel", "arbitrary")),
        name="cmp_attn",
    )(q4, cmp4, cmp4, addc, ovlt, eye)


def _softmax_step(qs, k, v, mask, add, carry):
    m, l, acc = carry
    tq = m.shape[1]
    s = _dot_nt(qs, k).reshape(GROUP, tq, k.shape[0])
    if mask is not None:
        s = jnp.where(mask[None] > 0.5, s, NEG)
    if add is not None:
        s = s + add
    m_new = jnp.maximum(m, jnp.max(s, axis=-1, keepdims=True))
    alpha = jnp.exp(m - m_new)
    p = jnp.exp(s - m_new)
    l = alpha * l + jnp.sum(p, axis=-1, keepdims=True)
    pv = _dot(p.reshape(GROUP * tq, k.shape[0]).astype(BF16), v).reshape(GROUP, tq, v.shape[1])
    return m_new, l, alpha * acc + pv


def _sel_win_kernel(q_ref, ks_ref, vs_ref, kw_ref, vw_ref, sel_ref, e3_ref, near_ref, edge_ref,
                    oc_ref, gates_ref, o_ref):
    i = pl.program_id(1)
    tq = q_ref.shape[1]
    tk = tq
    qs = q_ref[...].reshape(GROUP * tq, 64)
    sel = sel_ref[0]

    def init():
        return (jnp.full((GROUP, tq, 1), NEG, F32), jnp.zeros((GROUP, tq, 1), F32),
                jnp.zeros((GROUP, tq, 64), F32))

    def tile(ref, j):
        return ref[0, 0, pl.ds(pl.multiple_of(j * tk, tk), tk), :]

    def gated(delta, x):
        return jnp.where(i >= delta, x, NEG)

    near0 = near_ref[:, 0]
    near1 = gated(1, near_ref[:, 1])
    jm1 = jnp.maximum(i - 1, 0)

    c = _softmax_step(qs, tile(ks_ref, i), tile(vs_ref, i), _dot(sel, e3_ref[i]), near0, init())
    c = _softmax_step(qs, tile(ks_ref, jm1), tile(vs_ref, jm1), _dot(sel, e3_ref[jm1]), near1, c)

    def far(j, carry):
        return _softmax_step(qs, tile(ks_ref, j), tile(vs_ref, j), _dot(sel, e3_ref[j]), None, carry)

    m_s, l_s, acc_s = lax.fori_loop(0, jnp.maximum(i - 1, 0), far, c)
    o_s = acc_s / l_s

    c = _softmax_step(qs, tile(kw_ref, i), tile(vw_ref, i), None, near0, init())
    c = _softmax_step(qs, tile(kw_ref, jm1), tile(vw_ref, jm1), None, near1, c)
    n_back = WINDOW // tk
    for delta in range(2, n_back + 1):
        j = jnp.maximum(i - delta, 0)
        add = gated(delta, edge_ref[...] if delta == n_back else jnp.zeros((tq, tk), F32))
        c = _softmax_step(qs, tile(kw_ref, j), tile(vw_ref, j), None, add[None], c)
    m_w, l_w, acc_w = c
    o_w = acc_w / l_w

    g = jax.nn.sigmoid(gates_ref[0])
    for hh in range(GROUP):
        o_c = oc_ref[:, 64 * hh:64 * (hh + 1)]
        o_ref[:, 64 * hh:64 * (hh + 1)] = (g[:, 3 * hh:3 * hh + 1] * o_c
                                            + g[:, 3 * hh + 1:3 * hh + 2] * o_s[hh]
                                            + g[:, 3 * hh + 2:3 * hh + 3] * o_w[hh])


def _sel_win_attn(q4, kvsw5, sel, e3, near, edge, o_c, gates2, batch, seq):
    tq = ATT_TILE
    nt = seq // tq
    n_sel = sel.shape[2]

    def kv_spec(base):
        return pl.BlockSpec((1, 1, seq, 64), lambda g, i: (base + g % 2, g // 2, 0, 0))

    return pl.pallas_call(
        _sel_win_kernel,
        grid=(batch * N_KV_HEADS, nt),
        in_specs=[
            pl.BlockSpec((GROUP, tq, 64), lambda g, i: (g % 2, (g // 2) * nt + i, 0)),
            kv_spec(0), kv_spec(2), kv_spec(4), kv_spec(6),
            pl.BlockSpec((1, tq, n_sel), lambda g, i: (g, i, 0)),
            pl.BlockSpec(e3.shape, lambda g, i: (0, 0, 0)),
            pl.BlockSpec((GROUP, 2, tq, tq), lambda g, i: (g % 2, 0, 0, 0)),
            pl.BlockSpec((tq, tq), lambda g, i: (0, 0)),
            pl.BlockSpec((tq, GROUP * 64), lambda g, i: ((g // 2) * nt + i, g % 2)),
            pl.BlockSpec((1, tq, LANES), lambda g, i: (g % 2, (g // 2) * nt + i, 0)),
        ],
        out_specs=pl.BlockSpec((tq, GROUP * 64), lambda g, i: ((g // 2) * nt + i, g % 2)),
        out_shape=jax.ShapeDtypeStruct((batch * seq, N_HEADS * 64), F32),
        compiler_params=_params(("parallel", "arbitrary")),
        name="sel_win_attn",
    )(q4, kvsw5, kvsw5, kvsw5, kvsw5, sel, e3, near, edge, o_c, gates2)


def _rglru_kernel(xrg_ref, xg_ref, cw_ref, cb_ref, wa_ref, ba_ref, wx_ref, bx_ref, lam_ref, gn_ref,
                  o_ref, xs_ref, h_ref):
    step = pl.program_id(1)
    tc = xrg_ref.shape[0]

    @pl.when(step == 0)
    def _():
        xs_ref[0:8, :] = jnp.zeros((8, xs_ref.shape[1]), F32)
        h_ref[...] = jnp.zeros(h_ref.shape, F32)

    xs_ref[8:, :] = xrg_ref[...]
    xc = cb_ref[...] + cw_ref[CONV_WIDTH - 1:CONV_WIDTH, :] * xs_ref[8:, :]
    for k in range(1, CONV_WIDTH):
        xc = xc + cw_ref[CONV_WIDTH - 1 - k:CONV_WIDTH - k, :] * xs_ref[pl.ds(8 - k, tc), :]
    xs_ref[0:8, :] = xs_ref[tc:tc + 8, :]

    xcb = xc.astype(BF16)
    r = jax.nn.sigmoid(_dot(xcb, wa_ref[...]) + ba_ref[...])
    gi = jax.nn.sigmoid(_dot(xcb, wx_ref[...]) + bx_ref[...])
    lam = lam_ref[...]
    softplus_neg = jnp.maximum(-lam, 0.0) + jnp.log1p(jnp.exp(-jnp.abs(lam)))
    log_a = -LRU_C * r * softplus_neg
    a = jnp.exp(log_a)
    b = jnp.sqrt(1.0 - jnp.exp(2.0 * log_a)) * (gi * xc)

    row = lax.broadcasted_iota(jnp.int32, (tc, 1), 0)
    s = 1
    while s < tc:
        valid = row >= s
        b = jnp.where(valid, a * pltpu.roll(b, s, 0) + b, b)
        a = jnp.where(valid, a * pltpu.roll(a, s, 0), a)
        s *= 2
    h = b + a * h_ref[0:1, :]
    h_ref[0:1, :] = h[tc - 1:tc, :]
    o_ref[...] = _rms(h * jax.nn.gelu(xg_ref[...]), gn_ref[...]).astype(BF16)


def _rglru(xrg, cw, cb, wa, ba, wx, bx, lam, gn, batch, seq, tc):
    nt = seq // tc
    w = cw.shape[1]
    vec = pl.BlockSpec((1, w), lambda b, i: (0, 0))
    return pl.pallas_call(
        _rglru_kernel,
        grid=(batch, nt),
        in_specs=[
            pl.BlockSpec((tc, w), lambda b, i: (b * nt + i, 0)),
            pl.BlockSpec((tc, w), lambda b, i: (b * nt + i, 1)),
            pl.BlockSpec((CONV_WIDTH, w), lambda b, i: (0, 0)),
            vec,
            pl.BlockSpec((w, w), lambda b, i: (0, 0)), vec,
            pl.BlockSpec((w, w), lambda b, i: (0, 0)), vec,
            vec, vec,
        ],
        out_specs=pl.BlockSpec((tc, w), lambda b, i: (b * nt + i, 0)),
        out_shape=jax.ShapeDtypeStruct((batch * seq, w), BF16),
        scratch_shapes=[pltpu.VMEM((tc + 8, w), F32), pltpu.VMEM((8, w), F32)],
        compiler_params=_params(("parallel", "arbitrary")),
        name="rglru",
    )(xrg, xrg, cw, cb, wa, ba, wx, bx, lam, gn)


def _out_proj_kernel(a_ref, l_ref, x_ref, ga_ref, woa_ref, wol_ref, gf_ref, wq_ref, sk_ref,
                     h1_ref, xt_ref, st_ref):
    an = _rms(a_ref[...], ga_ref[...]).astype(BF16)
    h1 = x_ref[...] + _dot(an, woa_ref[...]) + _dot(l_ref[...], wol_ref[...])
    h1_ref[...] = h1
    xn = _rms(h1, gf_ref[...])
    xt_ref[...] = xn.T.astype(BF16)
    q = _dot(xn.astype(BF16), wq_ref[...])
    for hp in range(2 * PEER_HEADS):
        qs = q[:, LANES * hp:LANES * (hp + 1)].astype(BF16)
        st_ref[hp] = _dot_nt(sk_ref[hp % 2], qs)


def _out_proj(a_out, l_n, x2, ga, woa, wol, gf, wq, sk, tm):
    n, d = x2.shape
    full = lambda arr: pl.BlockSpec(arr.shape, lambda i: (0,) * arr.ndim)
    return pl.pallas_call(
        _out_proj_kernel,
        grid=(n // tm,),
        in_specs=[
            pl.BlockSpec((tm, a_out.shape[1]), lambda i: (i, 0)),
            pl.BlockSpec((tm, l_n.shape[1]), lambda i: (i, 0)),
            pl.BlockSpec((tm, d), lambda i: (i, 0)),
            full(ga), full(woa), full(wol), full(gf), full(wq), full(sk),
        ],
        out_specs=[
            pl.BlockSpec((tm, d), lambda i: (i, 0)),
            pl.BlockSpec((d, tm), lambda i: (0, i)),
            pl.BlockSpec((2 * PEER_HEADS, PEER_N_KEYS, tm), lambda i: (0, 0, i)),
        ],
        out_shape=[
            jax.ShapeDtypeStruct((n, d), F32),
            jax.ShapeDtypeStruct((d, n), BF16),
            jax.ShapeDtypeStruct((2 * PEER_HEADS, PEER_N_KEYS, n), F32),
        ],
        compiler_params=_params(("parallel",)),
        name="out_proj",
    )(a_out, l_n, x2, ga, woa, wol, gf, wq, sk)


N_TOP = PEER_TOPK + 1
_PAIRS = [(i, j) for i in range(N_TOP) for j in range(N_TOP) if (i + 1) * (j + 1) <= N_TOP]


def _top_values(s):
    rows = lax.broadcasted_iota(jnp.int32, s.shape, 0).astype(F32)
    out = []
    for _ in range(N_TOP):
        m = jnp.max(s, axis=0, keepdims=True)
        out.append(m)
        first = jnp.min(jnp.where(s == m, rows, float(s.shape[0])), axis=0, keepdims=True)
        s = jnp.where(rows == first, REMOVED, s)
    return out


def _peer_stats_kernel(st_ref, o_ref):
    tops = [[_top_values(st_ref[2 * h + p]) for h in range(PEER_HEADS)] for p in range(2)]
    a = [jnp.concatenate([tops[0][h][i] for h in range(PEER_HEADS)], axis=0) for i in range(N_TOP)]
    b = [jnp.concatenate([tops[1][h][j] for h in range(PEER_HEADS)], axis=0) for j in range(N_TOP)]
    cand = [a[i] + b[j] for (i, j) in _PAIRS]
    work = list(cand)
    kth = []
    for _ in range(N_TOP):
        m = functools.reduce(jnp.maximum, work)
        kth.append(m)
        found = jnp.zeros(m.shape, jnp.bool_)
        for k in range(len(work)):
            hit = (work[k] == m) & jnp.logical_not(found)
            work[k] = jnp.where(hit, REMOVED, work[k])
            found = found | hit
    thr = 0.5 * (kth[PEER_TOPK - 1] + kth[PEER_TOPK])
    top = a[0] + b[0]
    z = functools.reduce(lambda u, w: u + w, [jnp.where(c >= thr, jnp.exp(c - top), 0.0) for c in cand])
    o_ref[0] = a[0]
    o_ref[1] = b[0]
    o_ref[2] = thr
    o_ref[3] = z


def _peer_stats(st, tl):
    hp, nk, n = st.shape
    return pl.pallas_call(
        _peer_stats_kernel,
        grid=(n // tl,),
        in_specs=[pl.BlockSpec((hp, nk, tl), lambda i: (0, 0, i))],
        out_specs=pl.BlockSpec((4, PEER_HEADS, tl), lambda i: (0, 0, i)),
        out_shape=jax.ShapeDtypeStruct((4, PEER_HEADS, n), F32),
        compiler_params=_params(("parallel",)),
        name="peer_stats",
    )(st)


def _peer_main_kernel(u_ref, vt_ref, xt_ref, st_ref, stat_ref, h1_ref, o_ref,
                      e1_ref, thr_ref, e2_ref, acc_ref):
    c = pl.program_id(1)
    ce = u_ref.shape[0]
    nk = PEER_N_KEYS

    @pl.when(c == 0)
    def _():
        acc_ref[...] = jnp.zeros(acc_ref.shape, F32)
        for h in range(PEER_HEADS):
            s1 = st_ref[2 * h]
            s2 = st_ref[2 * h + 1]
            e1_ref[h] = jnp.exp(s1 - stat_ref[0, h:h + 1, :])
            thr_ref[h] = stat_ref[2, h:h + 1, :] - s1
            e2_ref[h] = jnp.exp(s2 - stat_ref[1, h:h + 1, :]) / stat_ref[3, h:h + 1, :]

    act = jax.nn.gelu(_dot(u_ref[...], xt_ref[...]))
    w_parts = []
    for ii in range(ce // nk):
        i1 = c * (ce // nk) + ii
        g = None
        for h in range(PEER_HEADS):
            keep = st_ref[2 * h + 1] >= thr_ref[h, pl.ds(i1, 1), :]
            term = e1_ref[h, pl.ds(i1, 1), :] * jnp.where(keep, e2_ref[h], 0.0)
            g = term if g is None else g + term
        w_parts.append((act[ii * nk:(ii + 1) * nk, :] * g).astype(BF16))
    w = jnp.concatenate(w_parts, axis=0)
    acc_ref[...] += _dot(vt_ref[...], w)

    @pl.when(c == pl.num_programs(1) - 1)
    def _():
        o_ref[...] = h1_ref[...] + acc_ref[...].T


def _peer_main(u_b, vt_b, xt, st, stats, h1, tm, ce):
    ne, d = u_b.shape
    n = xt.shape[1]
    return pl.pallas_call(
        _peer_main_kernel,
        grid=(n // tm, ne // ce),
        in_specs=[
            pl.BlockSpec((ce, d), lambda t, c: (c, 0)),
            pl.BlockSpec((d, ce), lambda t, c: (0, c)),
            pl.BlockSpec((d, tm), lambda t, c: (0, t)),
            pl.BlockSpec((2 * PEER_HEADS, PEER_N_KEYS, tm), lambda t, c: (0, 0, t)),
            pl.BlockSpec((4, PEER_HEADS, tm), lambda t, c: (0, 0, t)),
            pl.BlockSpec((tm, d), lambda t, c: (t, 0)),
        ],
        out_specs=pl.BlockSpec((tm, d), lambda t, c: (t, 0)),
        out_shape=jax.ShapeDtypeStruct((n, d), F32),
        scratch_shapes=[
            pltpu.VMEM((PEER_HEADS, PEER_N_KEYS, tm), F32),
            pltpu.VMEM((PEER_HEADS, PEER_N_KEYS, tm), F32),
            pltpu.VMEM((PEER_HEADS, PEER_N_KEYS, tm), F32),
            pltpu.VMEM((d, tm), F32),
        ],
        compiler_params=_params(("parallel", "arbitrary")),
        name="peer_main",
    )(u_b, vt_b, xt, st, stats, h1)


def _ple_final_kernel(h_ref, p_ref, gp_ref, wg_ref, bg_ref, wp_ref, gfin_ref, o_ref):
    h = h_ref[...]
    gate = jax.nn.sigmoid(_dot(_rms(h, gp_ref[...]).astype(BF16), wg_ref[...]) + bg_ref[...])
    h = h + gate * _dot(p_ref[...].astype(BF16), wp_ref[...])
    o_ref[...] = _rms(h, gfin_ref[...])


def _ple_final(h2, p2, gp, wg, bg, wp, gfin, tm):
    n, d = h2.shape
    full = lambda arr: pl.BlockSpec(arr.shape, lambda i: (0,) * arr.ndim)
    return pl.pallas_call(
        _ple_final_kernel,
        grid=(n // tm,),
        in_specs=[
            pl.BlockSpec((tm, d), lambda i: (i, 0)),
            pl.BlockSpec((tm, p2.shape[1]), lambda i: (i, 0)),
            full(gp), full(wg), full(bg), full(wp), full(gfin),
        ],
        out_specs=pl.BlockSpec((tm, d), lambda i: (i, 0)),
        out_shape=jax.ShapeDtypeStruct((n, d), F32),
        compiler_params=_params(("parallel",)),
        name="ple_final",
    )(h2, p2, gp, wg, bg, wp, gfin)


def _overlap_t(seq, n_rows):
    n_cmp = n_rows - 1
    cmp_start = np.arange(n_cmp) * CMP_STRIDE
    sel_start = np.arange(seq // SEL_LEN) * SEL_LEN
    ov = np.clip(np.minimum(cmp_start[:, None] + CMP_LEN, sel_start[None, :] + SEL_LEN)
                 - np.maximum(cmp_start[:, None], sel_start[None, :]), 0, None).astype(np.float32) / CMP_LEN
    out = np.zeros((seq // SEL_LEN, n_rows), np.float32)
    out[:, :n_cmp] = ov.T
    return out


def _block_expand(seq, tk):
    key_blk = (np.arange(seq) // SEL_LEN).reshape(seq // tk, 1, tk)
    return (key_blk == np.arange(seq // SEL_LEN).reshape(1, -1, 1)).astype(np.float32)


def _window_edge(t):
    r = np.arange(t)[:, None]
    c = np.arange(t)[None, :]
    return np.where(r - c + WINDOW < WINDOW, 0.0, NEG).astype(np.float32)


def _block_diag(w):
    nb, bd, _ = w.shape
    out = jnp.zeros((nb * bd, nb * bd), w.dtype)
    for j in range(nb):
        out = out.at[j * bd:(j + 1) * bd, j * bd:(j + 1) * bd].set(w[j])
    return out


def _layer(h, p_i, attn_norm, w_in, cmp_k_pe, cmp_k_w1, cmp_k_w2, cmp_v_pe, cmp_v_w1, cmp_v_w2,
           rel_table, conv_w, conv_b, lru_wa, lru_ba, lru_wx, lru_bx, lru_lambda,
           grp_norm_attn, grp_norm_lru, w_out, ffn_norm, peer_wq, peer_subkeys, peer_u, peer_v,
           ple_norm, ple_wgate, ple_bgate, ple_proj, *, tm, tc, tl, ce):
    batch, seq, d = h.shape
    n = batch * seq
    x2 = h.reshape(n, d)
    aw = N_HEADS * HEAD_DIM
    kvw = N_KV_HEADS * HEAD_DIM
    lw = d - aw

    o = 0
    cols = {}
    for name, width in (("q", aw), ("kc", kvw), ("vc", kvw), ("ks", kvw), ("vs", kvw), ("kw", kvw),
                        ("vw", kvw), ("gates", N_HEADS * 3), ("xr", lw), ("xg", lw)):
        cols[name] = w_in[:, o:o + width]
        o += width
    gate_pad = [jnp.pad(cols["gates"][:, N_GATE_COLS * k:N_GATE_COLS * (k + 1)], ((0, 0), (0, LANES - N_GATE_COLS)))
                for k in range(N_KV_HEADS)]
    wf = jnp.concatenate([cols["xr"], cols["xg"], cols["kc"], cols["vc"]] + gate_pad, axis=1).astype(BF16)
    wb = jnp.concatenate([cols["q"], cols["ks"], cols["vs"], cols["kw"], cols["vw"]], axis=1).astype(BF16)

    xrg, kvc, gates2, q4, kvsw = _in_proj(x2, attn_norm.reshape(1, d), wf, wb, tm)

    n_rows = seq // CMP_STRIDE
    r4 = kvc.reshape(4, batch, n_rows, CMP_STRIDE * HEAD_DIM)
    pe2 = jnp.stack([cmp_k_pe.reshape(1, -1), cmp_v_pe.reshape(1, -1)])
    w1s = jnp.stack([cmp_k_w1, cmp_v_w1]).astype(BF16)
    w2s = jnp.stack([cmp_k_w2, cmp_v_w2]).astype(BF16)
    cmp4 = _compress(r4, pe2, w1s, w2s)

    near = _bias_near(rel_table)
    addc = _bias_cmp(rel_table, seq, n_rows, min(seq, 512))
    ovlt = jnp.asarray(_overlap_t(seq, n_rows), BF16)
    o_c, sel = _cmp_attn(q4, cmp4, addc, ovlt, batch, seq)

    e3 = jnp.asarray(_block_expand(seq, ATT_TILE), BF16)
    edge = jnp.asarray(_window_edge(ATT_TILE))
    kvsw5 = kvsw.reshape(8, batch, seq, HEAD_DIM)
    a_out = _sel_win_attn(q4, kvsw5, sel, e3, near, edge, o_c, gates2, batch, seq)

    l_n = _rglru(xrg, conv_w.reshape(CONV_WIDTH, lw), conv_b.reshape(1, lw),
                 _block_diag(lru_wa).astype(BF16), lru_ba.reshape(1, lw),
                 _block_diag(lru_wx).astype(BF16), lru_bx.reshape(1, lw),
                 lru_lambda.reshape(1, lw), grp_norm_lru.reshape(1, lw), batch, seq, tc)

    w_out_b = w_out.astype(BF16)
    h1, xt, st = _out_proj(a_out, l_n, x2, grp_norm_attn.reshape(1, aw), w_out_b[:aw], w_out_b[aw:],
                           ffn_norm.reshape(1, d), peer_wq.astype(BF16), peer_subkeys.astype(BF16), tm)
    stats = _peer_stats(st, tl)
    h2 = _peer_main(peer_u.astype(BF16), peer_v.T.astype(BF16), xt, st, stats, h1, tm, ce)
    ple_args = (h2, p_i.reshape(n, -1), ple_norm.reshape(1, d), ple_wgate.astype(BF16), ple_bgate.reshape(1, d),
                ple_proj.astype(BF16))
    return ple_args, dict(a_out=a_out, l_n=l_n, h1=h1)


def kernel(x, p, attn_norm, w_in, cmp_k_pe, cmp_k_w1, cmp_k_w2, cmp_v_pe, cmp_v_w1, cmp_v_w2, rel_table, conv_w, conv_b, lru_wa, lru_ba, lru_wx, lru_bx, lru_lambda, grp_norm_attn, grp_norm_lru, w_out, ffn_norm, peer_wq, peer_subkeys, peer_u, peer_v, ple_norm, ple_wgate, ple_bgate, ple_proj, final_norm):
    batch, seq, d = x.shape
    assert w_in.shape[0] == 1, "single-layer stack only"
    tm = min(512, seq)
    (h2, p2, gp, wg, bg, wp), _ = _layer(
        x, p[0], attn_norm[0], w_in[0], cmp_k_pe[0], cmp_k_w1[0], cmp_k_w2[0], cmp_v_pe[0], cmp_v_w1[0],
        cmp_v_w2[0], rel_table, conv_w[0], conv_b[0], lru_wa[0], lru_ba[0], lru_wx[0], lru_bx[0],
        lru_lambda[0], grp_norm_attn[0], grp_norm_lru[0], w_out[0], ffn_norm[0], peer_wq[0],
        peer_subkeys[0], peer_u[0], peer_v[0], ple_norm[0], ple_wgate[0], ple_bgate[0], ple_proj[0],
        tm=tm, tc=min(256, seq), tl=min(256, seq), ce=512)
    out = _ple_final(h2, p2, gp, wg, bg, wp, final_norm.reshape(1, d), tm)
    return out.reshape(batch, seq, d)
```

```python
import functools
import math

import numpy as np
import jax
import jax.numpy as jnp
from jax import lax
from jax.experimental import pallas as pl
from jax.experimental.pallas import tpu as pltpu

F32 = jnp.float32
BF16 = jnp.bfloat16

N_HEADS = 8
HEAD_DIM = 64
N_KV_HEADS = 2
GROUP = N_HEADS // N_KV_HEADS
LRU_C = 8.0
CONV_WIDTH = 4
CMP_LEN = 32
CMP_STRIDE = 16
SEL_LEN = 64
SEL_TOPK = 16
N_LOCAL_BLOCKS = 2
WINDOW = 512
FORCE_LOCAL = 2.0e4
FORCE_INIT = 1.0e4
REL_BUCKETS = 32
REL_MAX_DIST = 128
PEER_HEADS = 8
PEER_N_KEYS = 128
PEER_TOPK = 16
EPS = 1e-6
NEG = -1e30
REMOVED = -3.0e38

LANES = 128
VMEM_LIMIT = 56 * 1024 * 1024

ATT_TILE = 128
N_GATE_COLS = GROUP * 3


def _dot(a, b):
    return jnp.dot(a, b, preferred_element_type=F32)


def _dot_nt(a, b):
    return lax.dot_general(a, b, (((1,), (1,)), ((), ())), preferred_element_type=F32)


def _rms(x, g):
    return x * lax.rsqrt(jnp.mean(x * x, axis=-1, keepdims=True) + EPS) * g


def _params(sem, vmem=VMEM_LIMIT):
    return pltpu.CompilerParams(dimension_semantics=sem, vmem_limit_bytes=vmem)


def _in_proj_kernel(x_ref, g_ref, wf_ref, wb_ref, xrg_ref, kvc_ref, gates_ref, q4_ref, kk_ref, vv_ref):
    xb = _rms(x_ref[...], g_ref[...]).astype(BF16)
    yf = _dot(xb, wf_ref[...])
    xrg_ref[...] = yf[:, :1024]
    for j in range(4):
        kvc_ref[j] = yf[:, 1024 + 64 * j:1024 + 64 * (j + 1)]
    gates_ref[0] = yf[:, 1280:1408]
    gates_ref[1] = yf[:, 1408:1536]
    yb = _dot(xb, wb_ref[...])
    for h in range(N_HEADS):
        q4_ref[h] = (yb[:, 64 * h:64 * (h + 1)] * (HEAD_DIM ** -0.5)).astype(BF16)
    ones_col = (lax.broadcasted_iota(jnp.int32, (yb.shape[0], 64), 1) == 0).astype(F32)
    for j in range(4):
        kk_ref[j] = yb[:, 512 + 64 * j:512 + 64 * (j + 1)].astype(BF16)
        vv_ref[j] = jnp.concatenate([yb[:, 768 + 64 * j:768 + 64 * (j + 1)], ones_col], axis=1).astype(BF16)


def _in_proj(x2, g, wf, wb, tm):
    n, d = x2.shape
    return pl.pallas_call(
        _in_proj_kernel,
        grid=(n // tm,),
        in_specs=[
            pl.BlockSpec((tm, d), lambda i: (i, 0)),
            pl.BlockSpec((1, d), lambda i: (0, 0)),
            pl.BlockSpec(wf.shape, lambda i: (0, 0)),
            pl.BlockSpec(wb.shape, lambda i: (0, 0)),
        ],
        out_specs=[
            pl.BlockSpec((tm, 1024), lambda i: (i, 0)),
            pl.BlockSpec((4, tm, 64), lambda i: (0, i, 0)),
            pl.BlockSpec((2, tm, LANES), lambda i: (0, i, 0)),
            pl.BlockSpec((8, tm, 64), lambda i: (0, i, 0)),
            pl.BlockSpec((4, tm, 64), lambda i: (0, i, 0)),
            pl.BlockSpec((4, tm, LANES), lambda i: (0, i, 0)),
        ],
        out_shape=[
            jax.ShapeDtypeStruct((n, 1024), F32),
            jax.ShapeDtypeStruct((4, n, 64), F32),
            jax.ShapeDtypeStruct((2, n, LANES), F32),
            jax.ShapeDtypeStruct((8, n, 64), BF16),
            jax.ShapeDtypeStruct((4, n, 64), BF16),
            jax.ShapeDtypeStruct((4, n, LANES), BF16),
        ],
        compiler_params=_params(("parallel",)),
        name="in_proj",
    )(x2, g, wf, wb)


def _compress_kernel(r_ref, pe_ref, w1_ref, w2_ref, o_ref):
    r = r_ref[0, 0]
    n_rows = r.shape[0]
    half = r.shape[1]
    pe = pe_ref[0]
    top = _dot((r + pe[:, :half]).astype(BF16), w1_ref[0, :half, :])
    bot = _dot((r + pe[:, half:]).astype(BF16), w1_ref[0, half:, :])
    hid = top + pltpu.roll(bot, n_rows - 1, 0)
    act = jax.nn.gelu(hid)
    out = _dot(act.astype(BF16), w2_ref[0])
    row = lax.broadcasted_iota(jnp.int32, out.shape, 0)
    o_ref[0, 0] = jnp.where(row < n_rows - 1, out, 0.0).astype(BF16)


def _compress(r4, pe2, w1s, w2s):
    four, b, n_rows, width = r4.shape
    return pl.pallas_call(
        _compress_kernel,
        grid=(four, b),
        in_specs=[
            pl.BlockSpec((1, 1, n_rows, width), lambda j, bb: (j, bb, 0, 0)),
            pl.BlockSpec((1, 1, 2 * width), lambda j, bb: (j // 2, 0, 0)),
            pl.BlockSpec((1, 2 * width, w1s.shape[2]), lambda j, bb: (j // 2, 0, 0)),
            pl.BlockSpec((1, w2s.shape[1], 64), lambda j, bb: (j // 2, 0, 0)),
        ],
        out_specs=pl.BlockSpec((1, 1, n_rows, 64), lambda j, bb: (j, bb, 0, 0)),
        out_shape=jax.ShapeDtypeStruct((four, b, n_rows, 64), BF16),
        compiler_params=_params(("parallel", "parallel")),
        name="compress",
    )(r4, pe2, w1s, w2s)


def _bucket_breakpoints():
    n = np.arange(0, 4 * REL_MAX_DIST)
    max_exact = REL_BUCKETS // 2
    nf = np.maximum(n, max_exact).astype(np.float32)
    large = max_exact + (np.log(nf / max_exact) / np.float32(math.log(REL_MAX_DIST / max_exact))
                         * (REL_BUCKETS - max_exact)).astype(np.int32)
    large = np.minimum(large, REL_BUCKETS - 1)
    bucket = np.where(n < max_exact, n, large)
    return [int(np.argmax(bucket >= j)) for j in range(1, REL_BUCKETS)]


_BREAKS = _bucket_breakpoints()


def _bias_of_dist(dist, tab_ref, h):
    v = jnp.full(dist.shape, tab_ref[0, h], F32)
    for j, bp in enumerate(_BREAKS, start=1):
        v = jnp.where(dist >= bp, tab_ref[j, h], v)
    v = v - tab_ref[REL_BUCKETS - 1, h]
    return jnp.where(dist >= 0, v, NEG)


WIN_SPAN = WINDOW + ATT_TILE
NEAR_SPAN = 2 * ATT_TILE


def _bias_near_kernel(tab_ref, win_ref, sel_ref):
    h = pl.program_id(0)
    t = ATT_TILE
    for ref, span in ((win_ref, WIN_SPAN), (sel_ref, NEAR_SPAN)):
        r = lax.broadcasted_iota(jnp.int32, (t, span), 0)
        c = lax.broadcasted_iota(jnp.int32, (t, span), 1)
        dist = r - c + (span - t)
        ref[0] = jnp.where(dist < WINDOW, _bias_of_dist(dist, tab_ref, h), NEG)


def _bias_near(rel_table):
    t = ATT_TILE
    return pl.pallas_call(
        _bias_near_kernel,
        grid=(N_HEADS,),
        in_specs=[pl.BlockSpec(memory_space=pltpu.SMEM)],
        out_specs=[pl.BlockSpec((1, t, WIN_SPAN), lambda h: (h, 0, 0)),
                   pl.BlockSpec((1, t, NEAR_SPAN), lambda h: (h, 0, 0))],
        out_shape=[jax.ShapeDtypeStruct((N_HEADS, t, WIN_SPAN), F32),
                   jax.ShapeDtypeStruct((N_HEADS, t, NEAR_SPAN), F32)],
        compiler_params=_params(("arbitrary",)),
        name="bias_near",
    )(rel_table)


def _bias_cmp_kernel(tab_ref, o_ref, *, n_cmp):
    h = pl.program_id(0)
    i = pl.program_id(1)
    tq, nc = o_ref.shape[1], o_ref.shape[2]
    t = lax.broadcasted_iota(jnp.int32, (tq, nc), 0) + i * tq
    c = lax.broadcasted_iota(jnp.int32, (tq, nc), 1)
    dist = t - (c * CMP_STRIDE + CMP_LEN - 1)
    o_ref[0] = jnp.where(c < n_cmp, _bias_of_dist(dist, tab_ref, h), NEG)


def _bias_cmp(rel_table, seq, n_rows, tq):
    return pl.pallas_call(
        functools.partial(_bias_cmp_kernel, n_cmp=n_rows - 1),
        grid=(N_HEADS, seq // tq),
        in_specs=[pl.BlockSpec(memory_space=pltpu.SMEM)],
        out_specs=pl.BlockSpec((1, tq, n_rows), lambda h, i: (h, i, 0)),
        out_shape=jax.ShapeDtypeStruct((N_HEADS, seq, n_rows), F32),
        compiler_params=_params(("arbitrary", "arbitrary")),
        name="bias_cmp",
    )(rel_table)


def _cmp_attn_kernel(q_ref, kc_ref, vc_ref, addc_ref, ovlt_ref, eye_ref, oc_ref, sel_ref):
    i = pl.program_id(1)
    tq = q_ref.shape[1]
    n_sel = ovlt_ref.shape[0]
    kc = kc_ref[0, 0]
    vc = vc_ref[0, 0]
    t_col = lax.broadcasted_iota(jnp.int32, (tq, 1), 0) + i * tq
    row_ok = (t_col >= CMP_LEN - 1).astype(F32)
    psum = None
    for hh in range(GROUP):
        logit = _dot_nt(q_ref[hh], kc) + addc_ref[hh]
        m = jnp.max(logit, axis=-1, keepdims=True)
        e = jnp.exp(logit - m)
        p = e / jnp.sum(e, axis=-1, keepdims=True)
        oc_ref[:, 64 * hh:64 * (hh + 1)] = _dot(p.astype(BF16), vc) * row_ok
        psum = p if psum is None else psum + p
    psum = psum * row_ok
    hi = psum.astype(BF16)
    lo = (psum - hi.astype(F32)).astype(BF16)
    ovlt = ovlt_ref[...]
    imp = _dot_nt(ovlt, hi) + _dot_nt(ovlt, lo)
    s_idx = lax.broadcasted_iota(jnp.int32, (n_sel, tq), 0)
    t_blk = (lax.broadcasted_iota(jnp.int32, (n_sel, tq), 1) + i * tq) // SEL_LEN
    d_blk = t_blk - s_idx
    local = (d_blk >= 0) & (d_blk < N_LOCAL_BLOCKS)
    v = jnp.where(local, FORCE_LOCAL,
                  jnp.where(s_idx == 0, FORCE_INIT, jnp.where(d_blk >= 0, imp, -1.0)))
    rows = s_idx.astype(F32)
    sel = jnp.zeros((n_sel, tq), F32)
    for _ in range(min(SEL_TOPK, n_sel)):
        m = jnp.max(v, axis=0, keepdims=True)
        first = jnp.min(jnp.where(v == m, rows, float(n_sel)), axis=0, keepdims=True)
        pick = rows == first
        sel = jnp.where(pick, 1.0, sel)
        v = jnp.where(pick, REMOVED, v)
    sel_ref[0] = _dot_nt(eye_ref[...], sel.astype(BF16)).astype(BF16)


def _cmp_attn(q4, cmp4, addc, ovlt, batch, seq):
    tq = ATT_TILE
    nt = seq // tq
    n_rows = cmp4.shape[2]
    n_sel = ovlt.shape[0]
    eye = jnp.eye(tq, dtype=BF16)
    return pl.pallas_call(
        _cmp_attn_kernel,
        grid=(batch * N_KV_HEADS, nt),
        in_specs=[
            pl.BlockSpec((GROUP, tq, 64), lambda g, i: (g % 2, (g // 2) * nt + i, 0)),
            pl.BlockSpec((1, 1, n_rows, 64), lambda g, i: (g % 2, g // 2, 0, 0)),
            pl.BlockSpec((1, 1, n_rows, 64), lambda g, i: (2 + g % 2, g // 2, 0, 0)),
            pl.BlockSpec((GROUP, tq, n_rows), lambda g, i: (g % 2, i, 0)),
            pl.BlockSpec(ovlt.shape, lambda g, i: (0, 0)),
            pl.BlockSpec((tq, tq), lambda g, i: (0, 0)),
        ],
        out_specs=[
            pl.BlockSpec((tq, GROUP * 64), lambda g, i: ((g // 2) * nt + i, g % 2)),
            pl.BlockSpec((1, tq, n_sel), lambda g, i: (g, i, 0)),
        ],
        out_shape=[
            jax.ShapeDtypeStruct((batch * seq, N_HEADS * 64), F32),
            jax.ShapeDtypeStruct((batch * N_KV_HEADS, seq, n_sel), BF16),
        ],
        compiler_params=_params(("parallel", "arbitrary")),
        name="cmp_attn",
    )(q4, cmp4, cmp4, addc, ovlt, eye)


def _softmax_step(qs, k, v1, mask, add, carry):
    tq = qs.shape[0] // GROUP
    s = _dot_nt(qs, k).reshape(GROUP, tq, k.shape[0])
    if mask is not None:
        s = jnp.where(mask[None] > 0.5, s, NEG)
    if add is not None:
        s = s + add
    m_tile = jnp.max(s, axis=-1, keepdims=True)
    if carry is None:
        m_new = m_tile
    else:
        m, acc = carry
        m_new = jnp.maximum(m, m_tile)
    p = jnp.exp(s - m_new)
    pv = _dot(p.reshape(GROUP * tq, k.shape[0]).astype(BF16), v1).reshape(GROUP, tq, v1.shape[1])
    if carry is None:
        return m_new, pv
    return m_new, jnp.exp(m - m_new) * acc + pv


FAR_SLAB = 4 * ATT_TILE


def _sel_win_kernel(q_ref, ks_ref, vs_ref, kw_ref, vw_ref, sel_ref, enear_ref, eslab_ref, etile_ref,
                    winadd_ref, seladd_ref, oc_ref, gates_ref, o_ref):
    i = pl.program_id(1)
    tq = q_ref.shape[1]
    qs = q_ref[...].reshape(GROUP * tq, 64)
    sel = sel_ref[0]

    def rows(ref, start, size):
        return ref[0, 0, pl.ds(pl.multiple_of(start, ATT_TILE), size), :]

    def hide_padding(add, n_pad):
        col = lax.broadcasted_iota(jnp.int32, (1, 1, add.shape[2]), 2)
        return jnp.where(col < n_pad, NEG, add)

    w0 = i * tq
    _, acc_w = _softmax_step(qs, rows(kw_ref, w0, WIN_SPAN), rows(vw_ref, w0, WIN_SPAN), None,
                             hide_padding(winadd_ref[...], WINDOW - i * tq), None)

    n0 = w0 + WINDOW - tq
    carry = _softmax_step(qs, rows(ks_ref, n0, NEAR_SPAN), rows(vs_ref, n0, NEAR_SPAN),
                          _dot(sel, enear_ref[i]), hide_padding(seladd_ref[...], tq - i * tq), None)
    n_far = jnp.maximum(i - 1, 0)
    per_slab = FAR_SLAB // tq
    n_slab = n_far // per_slab

    def far_slab(j, c):
        start = WINDOW + j * FAR_SLAB
        return _softmax_step(qs, rows(ks_ref, start, FAR_SLAB), rows(vs_ref, start, FAR_SLAB),
                             _dot(sel, eslab_ref[j]), None, c)

    def far_tile(j, c):
        start = WINDOW + j * tq
        return _softmax_step(qs, rows(ks_ref, start, tq), rows(vs_ref, start, tq),
                             _dot(sel, etile_ref[j]), None, c)

    carry = lax.fori_loop(0, n_slab, far_slab, carry)
    _, acc_s = lax.fori_loop(n_slab * per_slab, n_far, far_tile, carry)
    o_s = acc_s[..., :64] / acc_s[..., 64:65]
    o_w = acc_w[..., :64] / acc_w[..., 64:65]

    g = jax.nn.sigmoid(gates_ref[0])
    for hh in range(GROUP):
        o_c = oc_ref[:, 64 * hh:64 * (hh + 1)]
        o_ref[:, 64 * hh:64 * (hh + 1)] = (g[:, 3 * hh:3 * hh + 1] * o_c
                                            + g[:, 3 * hh + 1:3 * hh + 2] * o_s[hh]
                                            + g[:, 3 * hh + 2:3 * hh + 3] * o_w[hh])


def _sel_win_attn(q4, kk5, vv5, sel, enear, eslab, etile, winadd, seladd, o_c, gates2, batch, seq):
    tq = ATT_TILE
    nt = seq // tq
    n_sel = sel.shape[2]
    padded = kk5.shape[2]

    def kv_spec(arr, base):
        return pl.BlockSpec((1, 1, padded, arr.shape[3]), lambda g, i: (base + g % 2, g // 2, 0, 0))

    whole = lambda arr: pl.BlockSpec(arr.shape, lambda g, i: (0,) * arr.ndim)
    return pl.pallas_call(
        _sel_win_kernel,
        grid=(batch * N_KV_HEADS, nt),
        in_specs=[
            pl.BlockSpec((GROUP, tq, 64), lambda g, i: (g % 2, (g // 2) * nt + i, 0)),
            kv_spec(kk5, 0), kv_spec(vv5, 0), kv_spec(kk5, 2), kv_spec(vv5, 2),
            pl.BlockSpec((1, tq, n_sel), lambda g, i: (g, i, 0)),
            whole(enear), whole(eslab), whole(etile),
            pl.BlockSpec((GROUP, tq, WIN_SPAN), lambda g, i: (g % 2, 0, 0)),
            pl.BlockSpec((GROUP, tq, NEAR_SPAN), lambda g, i: (g % 2, 0, 0)),
            pl.BlockSpec((tq, GROUP * 64), lambda g, i: ((g // 2) * nt + i, g % 2)),
            pl.BlockSpec((1, tq, LANES), lambda g, i: (g % 2, (g // 2) * nt + i, 0)),
        ],
        out_specs=pl.BlockSpec((tq, GROUP * 64), lambda g, i: ((g // 2) * nt + i, g % 2)),
        out_shape=jax.ShapeDtypeStruct((batch * seq, N_HEADS * 64), F32),
        compiler_params=_params(("parallel", "arbitrary")),
        name="sel_win_attn",
    )(q4, kk5, vv5, kk5, vv5, sel, enear, eslab, etile, winadd, seladd, o_c, gates2)


def _rglru_kernel(xrg_ref, xg_ref, cw_ref, cb_ref, wa_ref, ba_ref, wx_ref, bx_ref, lam_ref, gn_ref,
                  o_ref, xs_ref, h_ref):
    step = pl.program_id(1)
    tc = xrg_ref.shape[0]

    @pl.when(step == 0)
    def _():
        xs_ref[0:8, :] = jnp.zeros((8, xs_ref.shape[1]), F32)
        h_ref[...] = jnp.zeros(h_ref.shape, F32)

    xs_ref[8:, :] = xrg_ref[...]
    xc = cb_ref[...] + cw_ref[CONV_WIDTH - 1:CONV_WIDTH, :] * xs_ref[8:, :]
    for k in range(1, CONV_WIDTH):
        xc = xc + cw_ref[CONV_WIDTH - 1 - k:CONV_WIDTH - k, :] * xs_ref[pl.ds(8 - k, tc), :]
    xs_ref[0:8, :] = xs_ref[tc:tc + 8, :]

    xcb = xc.astype(BF16)
    r = jax.nn.sigmoid(_dot(xcb, wa_ref[...]) + ba_ref[...])
    gi = jax.nn.sigmoid(_dot(xcb, wx_ref[...]) + bx_ref[...])
    lam = lam_ref[...]
    softplus_neg = jnp.maximum(-lam, 0.0) + jnp.log1p(jnp.exp(-jnp.abs(lam)))
    log_a = -LRU_C * r * softplus_neg
    a = jnp.exp(log_a)
    b = jnp.sqrt(1.0 - jnp.exp(2.0 * log_a)) * (gi * xc)

    row = lax.broadcasted_iota(jnp.int32, (tc, 1), 0)
    s = 1
    while s < tc:
        valid = row >= s
        b = jnp.where(valid, a * pltpu.roll(b, s, 0) + b, b)
        a = jnp.where(valid, a * pltpu.roll(a, s, 0), a)
        s *= 2
    h = b + a * h_ref[0:1, :]
    h_ref[0:1, :] = h[tc - 1:tc, :]
    o_ref[...] = _rms(h * jax.nn.gelu(xg_ref[...]), gn_ref[...]).astype(BF16)


def _rglru(xrg, cw, cb, wa, ba, wx, bx, lam, gn, batch, seq, tc):
    nt = seq // tc
    w = cw.shape[1]
    vec = pl.BlockSpec((1, w), lambda b, i: (0, 0))
    return pl.pallas_call(
        _rglru_kernel,
        grid=(batch, nt),
        in_specs=[
            pl.BlockSpec((tc, w), lambda b, i: (b * nt + i, 0)),
            pl.BlockSpec((tc, w), lambda b, i: (b * nt + i, 1)),
            pl.BlockSpec((CONV_WIDTH, w), lambda b, i: (0, 0)),
            vec,
            pl.BlockSpec((w, w), lambda b, i: (0, 0)), vec,
            pl.BlockSpec((w, w), lambda b, i: (0, 0)), vec,
            vec, vec,
        ],
        out_specs=pl.BlockSpec((tc, w), lambda b, i: (b * nt + i, 0)),
        out_shape=jax.ShapeDtypeStruct((batch * seq, w), BF16),
        scratch_shapes=[pltpu.VMEM((tc + 8, w), F32), pltpu.VMEM((8, w), F32)],
        compiler_params=_params(("parallel", "arbitrary")),
        name="rglru",
    )(xrg, xrg, cw, cb, wa, ba, wx, bx, lam, gn)


def _out_proj_kernel(a_ref, l_ref, x_ref, ga_ref, woa_ref, wol_ref, gf_ref, wq_ref, sk_ref,
                     h1_ref, xt_ref, st_ref):
    an = _rms(a_ref[...], ga_ref[...]).astype(BF16)
    h1 = x_ref[...] + _dot(an, woa_ref[...]) + _dot(l_ref[...], wol_ref[...])
    h1_ref[...] = h1
    xn = _rms(h1, gf_ref[...])
    xt_ref[...] = xn.T.astype(BF16)
    q = _dot(xn.astype(BF16), wq_ref[...])
    for hp in range(2 * PEER_HEADS):
        qs = q[:, LANES * hp:LANES * (hp + 1)].astype(BF16)
        st_ref[hp] = _dot_nt(sk_ref[hp % 2], qs)


def _out_proj(a_out, l_n, x2, ga, woa, wol, gf, wq, sk, tm):
    n, d = x2.shape
    full = lambda arr: pl.BlockSpec(arr.shape, lambda i: (0,) * arr.ndim)
    return pl.pallas_call(
        _out_proj_kernel,
        grid=(n // tm,),
        in_specs=[
            pl.BlockSpec((tm, a_out.shape[1]), lambda i: (i, 0)),
            pl.BlockSpec((tm, l_n.shape[1]), lambda i: (i, 0)),
            pl.BlockSpec((tm, d), lambda i: (i, 0)),
            full(ga), full(woa), full(wol), full(gf), full(wq), full(sk),
        ],
        out_specs=[
            pl.BlockSpec((tm, d), lambda i: (i, 0)),
            pl.BlockSpec((d, tm), lambda i: (0, i)),
            pl.BlockSpec((2 * PEER_HEADS, PEER_N_KEYS, tm), lambda i: (0, 0, i)),
        ],
        out_shape=[
            jax.ShapeDtypeStruct((n, d), F32),
            jax.ShapeDtypeStruct((d, n), BF16),
            jax.ShapeDtypeStruct((2 * PEER_HEADS, PEER_N_KEYS, n), F32),
        ],
        compiler_params=_params(("parallel",)),
        name="out_proj",
    )(a_out, l_n, x2, ga, woa, wol, gf, wq, sk)


N_TOP = PEER_TOPK + 1
_PAIRS = [(i, j) for i in range(N_TOP) for j in range(N_TOP) if (i + 1) * (j + 1) <= N_TOP]


def _top_values(s):
    rows = lax.broadcasted_iota(jnp.int32, s.shape, 0).astype(F32)
    out = []
    for _ in range(N_TOP):
        m = jnp.max(s, axis=0, keepdims=True)
        out.append(m)
        first = jnp.min(jnp.where(s == m, rows, float(s.shape[0])), axis=0, keepdims=True)
        s = jnp.where(rows == first, REMOVED, s)
    return out


def _peer_stats_kernel(st_ref, o_ref):
    tops = [[_top_values(st_ref[2 * h + p]) for h in range(PEER_HEADS)] for p in range(2)]
    a = [jnp.concatenate([tops[0][h][i] for h in range(PEER_HEADS)], axis=0) for i in range(N_TOP)]
    b = [jnp.concatenate([tops[1][h][j] for h in range(PEER_HEADS)], axis=0) for j in range(N_TOP)]
    cand = [a[i] + b[j] for (i, j) in _PAIRS]
    work = list(cand)
    kth = []
    for _ in range(N_TOP):
        m = functools.reduce(jnp.maximum, work)
        kth.append(m)
        found = jnp.zeros(m.shape, jnp.bool_)
        for k in range(len(work)):
            hit = (work[k] == m) & jnp.logical_not(found)
            work[k] = jnp.where(hit, REMOVED, work[k])
            found = found | hit
    thr = 0.5 * (kth[PEER_TOPK - 1] + kth[PEER_TOPK])
    top = a[0] + b[0]
    z = functools.reduce(lambda u, w: u + w, [jnp.where(c >= thr, jnp.exp(c - top), 0.0) for c in cand])
    o_ref[0] = a[0]
    o_ref[1] = b[0]
    o_ref[2] = thr
    o_ref[3] = z


def _peer_stats(st, tl):
    hp, nk, n = st.shape
    return pl.pallas_call(
        _peer_stats_kernel,
        grid=(n // tl,),
        in_specs=[pl.BlockSpec((hp, nk, tl), lambda i: (0, 0, i))],
        out_specs=pl.BlockSpec((4, PEER_HEADS, tl), lambda i: (0, 0, i)),
        out_shape=jax.ShapeDtypeStruct((4, PEER_HEADS, n), F32),
        compiler_params=_params(("parallel",)),
        name="peer_stats",
    )(st)


def _peer_main_kernel(u_ref, vt_ref, xt_ref, st_ref, stat_ref, h1_ref, o_ref,
                      e1_ref, thr_ref, e2_ref, acc_ref):
    c = pl.program_id(1)
    ce = u_ref.shape[0]
    nk = PEER_N_KEYS

    @pl.when(c == 0)
    def _():
        acc_ref[...] = jnp.zeros(acc_ref.shape, F32)
        for h in range(PEER_HEADS):
            s1 = st_ref[2 * h]
            s2 = st_ref[2 * h + 1]
            e1_ref[h] = jnp.exp(s1 - stat_ref[0, h:h + 1, :])
            thr_ref[h] = stat_ref[2, h:h + 1, :] - s1
            e2_ref[h] = jnp.exp(s2 - stat_ref[1, h:h + 1, :]) / stat_ref[3, h:h + 1, :]

    act = jax.nn.gelu(_dot(u_ref[...], xt_ref[...]))
    w_parts = []
    for ii in range(ce // nk):
        i1 = c * (ce // nk) + ii
        g = None
        for h in range(PEER_HEADS):
            keep = st_ref[2 * h + 1] >= thr_ref[h, pl.ds(i1, 1), :]
            term = e1_ref[h, pl.ds(i1, 1), :] * jnp.where(keep, e2_ref[h], 0.0)
            g = term if g is None else g + term
        w_parts.append((act[ii * nk:(ii + 1) * nk, :] * g).astype(BF16))
    w = jnp.concatenate(w_parts, axis=0)
    acc_ref[...] += _dot(vt_ref[...], w)

    @pl.when(c == pl.num_programs(1) - 1)
    def _():
        o_ref[...] = h1_ref[...] + acc_ref[...].T


def _peer_main(u_b, vt_b, xt, st, stats, h1, tm, ce):
    ne, d = u_b.shape
    n = xt.shape[1]
    return pl.pallas_call(
        _peer_main_kernel,
        grid=(n // tm, ne // ce),
        in_specs=[
            pl.BlockSpec((ce, d), lambda t, c: (c, 0)),
            pl.BlockSpec((d, ce), lambda t, c: (0, c)),
            pl.BlockSpec((d, tm), lambda t, c: (0, t)),
            pl.BlockSpec((2 * PEER_HEADS, PEER_N_KEYS, tm), lambda t, c: (0, 0, t)),
            pl.BlockSpec((4, PEER_HEADS, tm), lambda t, c: (0, 0, t)),
            pl.BlockSpec((tm, d), lambda t, c: (t, 0)),
        ],
        out_specs=pl.BlockSpec((tm, d), lambda t, c: (t, 0)),
        out_shape=jax.ShapeDtypeStruct((n, d), F32),
        scratch_shapes=[
            pltpu.VMEM((PEER_HEADS, PEER_N_KEYS, tm), F32),
            pltpu.VMEM((PEER_HEADS, PEER_N_KEYS, tm), F32),
            pltpu.VMEM((PEER_HEADS, PEER_N_KEYS, tm), F32),
            pltpu.VMEM((d, tm), F32),
        ],
        compiler_params=_params(("parallel", "arbitrary")),
        name="peer_main",
    )(u_b, vt_b, xt, st, stats, h1)


def _ple_final_kernel(h_ref, p_ref, gp_ref, wg_ref, bg_ref, wp_ref, gfin_ref, o_ref):
    h = h_ref[...]
    gate = jax.nn.sigmoid(_dot(_rms(h, gp_ref[...]).astype(BF16), wg_ref[...]) + bg_ref[...])
    h = h + gate * _dot(p_ref[...].astype(BF16), wp_ref[...])
    o_ref[...] = _rms(h, gfin_ref[...])


def _ple_final(h2, p2, gp, wg, bg, wp, gfin, tm):
    n, d = h2.shape
    full = lambda arr: pl.BlockSpec(arr.shape, lambda i: (0,) * arr.ndim)
    return pl.pallas_call(
        _ple_final_kernel,
        grid=(n // tm,),
        in_specs=[
            pl.BlockSpec((tm, d), lambda i: (i, 0)),
            pl.BlockSpec((tm, p2.shape[1]), lambda i: (i, 0)),
            full(gp), full(wg), full(bg), full(wp), full(gfin),
        ],
        out_specs=pl.BlockSpec((tm, d), lambda i: (i, 0)),
        out_shape=jax.ShapeDtypeStruct((n, d), F32),
        compiler_params=_params(("parallel",)),
        name="ple_final",
    )(h2, p2, gp, wg, bg, wp, gfin)


def _overlap_t(seq, n_rows):
    n_cmp = n_rows - 1
    cmp_start = np.arange(n_cmp) * CMP_STRIDE
    sel_start = np.arange(seq // SEL_LEN) * SEL_LEN
    ov = np.clip(np.minimum(cmp_start[:, None] + CMP_LEN, sel_start[None, :] + SEL_LEN)
                 - np.maximum(cmp_start[:, None], sel_start[None, :]), 0, None).astype(np.float32) / CMP_LEN
    out = np.zeros((seq // SEL_LEN, n_rows), np.float32)
    out[:, :n_cmp] = ov.T
    return out


def _block_expand(seq, starts, width):
    pos = np.asarray(starts).reshape(-1, 1, 1) + np.arange(width).reshape(1, 1, width)
    blk = np.where(pos >= 0, pos // SEL_LEN, -1)
    return (blk == np.arange(seq // SEL_LEN).reshape(1, -1, 1)).astype(np.float32)


def _block_diag(w):
    nb, bd, _ = w.shape
    out = jnp.zeros((nb * bd, nb * bd), w.dtype)
    for j in range(nb):
        out = out.at[j * bd:(j + 1) * bd, j * bd:(j + 1) * bd].set(w[j])
    return out


def _layer(h, p_i, attn_norm, w_in, cmp_k_pe, cmp_k_w1, cmp_k_w2, cmp_v_pe, cmp_v_w1, cmp_v_w2,
           rel_table, conv_w, conv_b, lru_wa, lru_ba, lru_wx, lru_bx, lru_lambda,
           grp_norm_attn, grp_norm_lru, w_out, ffn_norm, peer_wq, peer_subkeys, peer_u, peer_v,
           ple_norm, ple_wgate, ple_bgate, ple_proj, *, tm, tc, tl, ce):
    batch, seq, d = h.shape
    n = batch * seq
    x2 = h.reshape(n, d)
    aw = N_HEADS * HEAD_DIM
    kvw = N_KV_HEADS * HEAD_DIM
    lw = d - aw

    o = 0
    cols = {}
    for name, width in (("q", aw), ("kc", kvw), ("vc", kvw), ("ks", kvw), ("vs", kvw), ("kw", kvw),
                        ("vw", kvw), ("gates", N_HEADS * 3), ("xr", lw), ("xg", lw)):
        cols[name] = w_in[:, o:o + width]
        o += width
    gate_pad = [jnp.pad(cols["gates"][:, N_GATE_COLS * k:N_GATE_COLS * (k + 1)], ((0, 0), (0, LANES - N_GATE_COLS)))
                for k in range(N_KV_HEADS)]
    wf = jnp.concatenate([cols["xr"], cols["xg"], cols["kc"], cols["vc"]] + gate_pad, axis=1).astype(BF16)
    wb = jnp.concatenate([cols["q"], cols["ks"], cols["kw"], cols["vs"], cols["vw"]], axis=1).astype(BF16)

    xrg, kvc, gates2, q4, kk, vv = _in_proj(x2, attn_norm.reshape(1, d), wf, wb, tm)

    n_rows = seq // CMP_STRIDE
    r4 = kvc.reshape(4, batch, n_rows, CMP_STRIDE * HEAD_DIM)
    pe2 = jnp.stack([cmp_k_pe.reshape(1, -1), cmp_v_pe.reshape(1, -1)])
    w1s = jnp.stack([cmp_k_w1, cmp_v_w1]).astype(BF16)
    w2s = jnp.stack([cmp_k_w2, cmp_v_w2]).astype(BF16)
    cmp4 = _compress(r4, pe2, w1s, w2s)

    winadd, seladd = _bias_near(rel_table)
    addc = _bias_cmp(rel_table, seq, n_rows, min(seq, 512))
    ovlt = jnp.asarray(_overlap_t(seq, n_rows), BF16)
    o_c, sel = _cmp_attn(q4, cmp4, addc, ovlt, batch, seq)

    nt = seq // ATT_TILE
    tile_starts = np.arange(nt) * ATT_TILE
    enear = jnp.asarray(_block_expand(seq, tile_starts - ATT_TILE, NEAR_SPAN), BF16)
    eslab = jnp.asarray(_block_expand(seq, np.arange(max(seq // FAR_SLAB, 1)) * FAR_SLAB, FAR_SLAB), BF16)
    etile = jnp.asarray(_block_expand(seq, tile_starts, ATT_TILE), BF16)
    front = ((0, 0), (0, 0), (WINDOW, 0), (0, 0))
    kk5 = jnp.pad(kk.reshape(4, batch, seq, HEAD_DIM), front)
    vv5 = jnp.pad(vv.reshape(4, batch, seq, LANES), front)
    a_out = _sel_win_attn(q4, kk5, vv5, sel, enear, eslab, etile, winadd, seladd, o_c, gates2, batch, seq)

    l_n = _rglru(xrg, conv_w.reshape(CONV_WIDTH, lw), conv_b.reshape(1, lw),
                 _block_diag(lru_wa).astype(BF16), lru_ba.reshape(1, lw),
                 _block_diag(lru_wx).astype(BF16), lru_bx.reshape(1, lw),
                 lru_lambda.reshape(1, lw), grp_norm_lru.reshape(1, lw), batch, seq, tc)

    w_out_b = w_out.astype(BF16)
    h1, xt, st = _out_proj(a_out, l_n, x2, grp_norm_attn.reshape(1, aw), w_out_b[:aw], w_out_b[aw:],
                           ffn_norm.reshape(1, d), peer_wq.astype(BF16), peer_subkeys.astype(BF16), tm)
    stats = _peer_stats(st, tl)
    h2 = _peer_main(peer_u.astype(BF16), peer_v.T.astype(BF16), xt, st, stats, h1, tm, ce)
    ple_args = (h2, p_i.reshape(n, -1), ple_norm.reshape(1, d), ple_wgate.astype(BF16), ple_bgate.reshape(1, d),
                ple_proj.astype(BF16))
    return ple_args, dict(a_out=a_out, l_n=l_n, h1=h1)


def kernel(x, p, attn_norm, w_in, cmp_k_pe, cmp_k_w1, cmp_k_w2, cmp_v_pe, cmp_v_w1, cmp_v_w2, rel_table, conv_w, conv_b, lru_wa, lru_ba, lru_wx, lru_bx, lru_lambda, grp_norm_attn, grp_norm_lru, w_out, ffn_norm, peer_wq, peer_subkeys, peer_u, peer_v, ple_norm, ple_wgate, ple_bgate, ple_proj, final_norm):
    batch, seq, d = x.shape
    assert w_in.shape[0] == 1, "single-layer stack only"
    tm = min(512, seq)
    (h2, p2, gp, wg, bg, wp), _ = _layer(
        x, p[0], attn_norm[0], w_in[0], cmp_k_pe[0], cmp_k_w1[0], cmp_k_w2[0], cmp_v_pe[0], cmp_v_w1[0],
        cmp_v_w2[0], rel_table, conv_w[0], conv_b[0], lru_wa[0], lru_ba[0], lru_wx[0], lru_bx[0],
        lru_lambda[0], grp_norm_attn[0], grp_norm_lru[0], w_out[0], ffn_norm[0], peer_wq[0],
        peer_subkeys[0], peer_u[0], peer_v[0], ple_norm[0], ple_wgate[0], ple_bgate[0], ple_proj[0],
        tm=tm, tc=min(256, seq), tl=min(256, seq), ce=512)
    out = _ple_final(h2, p2, gp, wg, bg, wp, final_norm.reshape(1, d), tm)
    return out.reshape(batch, seq, d)
```

```python
import functools
import math

import numpy as np
import jax
import jax.numpy as jnp
from jax import lax
from jax.experimental import pallas as pl
from jax.experimental.pallas import tpu as pltpu

F32 = jnp.float32
BF16 = jnp.bfloat16

N_HEADS = 8
HEAD_DIM = 64
N_KV_HEADS = 2
GROUP = N_HEADS // N_KV_HEADS
LRU_C = 8.0
CONV_WIDTH = 4
CMP_LEN = 32
CMP_STRIDE = 16
SEL_LEN = 64
SEL_TOPK = 16
N_LOCAL_BLOCKS = 2
WINDOW = 512
FORCE_LOCAL = 2.0e4
FORCE_INIT = 1.0e4
REL_BUCKETS = 32
REL_MAX_DIST = 128
PEER_HEADS = 8
PEER_N_KEYS = 128
PEER_TOPK = 16
EPS = 1e-6
NEG = -1e30
REMOVED = -3.0e38

LANES = 128
VMEM_LIMIT = 56 * 1024 * 1024

ATT_TILE = 128
N_GATE_COLS = GROUP * 3


def _dot(a, b):
    return jnp.dot(a, b, preferred_element_type=F32)


def _dot_nt(a, b):
    return lax.dot_general(a, b, (((1,), (1,)), ((), ())), preferred_element_type=F32)


def _rms(x, g):
    return x * lax.rsqrt(jnp.mean(x * x, axis=-1, keepdims=True) + EPS) * g


def _params(sem, vmem=VMEM_LIMIT):
    return pltpu.CompilerParams(dimension_semantics=sem, vmem_limit_bytes=vmem)


def _in_proj_kernel(x_ref, g_ref, wf_ref, wb_ref, xrg_ref, kvc_ref, gates_ref, q4_ref, kk_ref, vv_ref):
    xb = _rms(x_ref[...], g_ref[...]).astype(BF16)
    yf = _dot(xb, wf_ref[...])
    xrg_ref[...] = yf[:, :1024]
    for j in range(4):
        kvc_ref[j] = yf[:, 1024 + 64 * j:1024 + 64 * (j + 1)]
    gates_ref[0] = yf[:, 1280:1408]
    gates_ref[1] = yf[:, 1408:1536]
    yb = _dot(xb, wb_ref[...])
    for h in range(N_HEADS):
        q4_ref[h] = (yb[:, 64 * h:64 * (h + 1)] * (HEAD_DIM ** -0.5)).astype(BF16)
    ones_col = (lax.broadcasted_iota(jnp.int32, (yb.shape[0], 64), 1) == 0).astype(F32)
    for j in range(4):
        kk_ref[j] = yb[:, 512 + 64 * j:512 + 64 * (j + 1)].astype(BF16)
        vv_ref[j] = jnp.concatenate([yb[:, 768 + 64 * j:768 + 64 * (j + 1)], ones_col], axis=1).astype(BF16)


def _in_proj(x2, g, wf, wb, tm):
    n, d = x2.shape
    return pl.pallas_call(
        _in_proj_kernel,
        grid=(n // tm,),
        in_specs=[
            pl.BlockSpec((tm, d), lambda i: (i, 0)),
            pl.BlockSpec((1, d), lambda i: (0, 0)),
            pl.BlockSpec(wf.shape, lambda i: (0, 0)),
            pl.BlockSpec(wb.shape, lambda i: (0, 0)),
        ],
        out_specs=[
            pl.BlockSpec((tm, 1024), lambda i: (i, 0)),
            pl.BlockSpec((4, tm, 64), lambda i: (0, i, 0)),
            pl.BlockSpec((2, tm, LANES), lambda i: (0, i, 0)),
            pl.BlockSpec((8, tm, 64), lambda i: (0, i, 0)),
            pl.BlockSpec((4, tm, 64), lambda i: (0, i, 0)),
            pl.BlockSpec((4, tm, LANES), lambda i: (0, i, 0)),
        ],
        out_shape=[
            jax.ShapeDtypeStruct((n, 1024), F32),
            jax.ShapeDtypeStruct((4, n, 64), F32),
            jax.ShapeDtypeStruct((2, n, LANES), F32),
            jax.ShapeDtypeStruct((8, n, 64), BF16),
            jax.ShapeDtypeStruct((4, n, 64), BF16),
            jax.ShapeDtypeStruct((4, n, LANES), BF16),
        ],
        compiler_params=_params(("parallel",)),
        name="in_proj",
    )(x2, g, wf, wb)


def _compress_kernel(r_ref, pe_ref, w1_ref, w2_ref, o_ref):
    r = r_ref[0, 0]
    n_rows = r.shape[0]
    half = r.shape[1]
    pe = pe_ref[0]
    top = _dot((r + pe[:, :half]).astype(BF16), w1_ref[0, :half, :])
    bot = _dot((r + pe[:, half:]).astype(BF16), w1_ref[0, half:, :])
    hid = top + pltpu.roll(bot, n_rows - 1, 0)
    act = jax.nn.gelu(hid)
    out = _dot(act.astype(BF16), w2_ref[0])
    row = lax.broadcasted_iota(jnp.int32, out.shape, 0)
    o_ref[0, 0] = jnp.where(row < n_rows - 1, out, 0.0).astype(BF16)


def _compress(r4, pe2, w1s, w2s):
    four, b, n_rows, width = r4.shape
    return pl.pallas_call(
        _compress_kernel,
        grid=(four, b),
        in_specs=[
            pl.BlockSpec((1, 1, n_rows, width), lambda j, bb: (j, bb, 0, 0)),
            pl.BlockSpec((1, 1, 2 * width), lambda j, bb: (j // 2, 0, 0)),
            pl.BlockSpec((1, 2 * width, w1s.shape[2]), lambda j, bb: (j // 2, 0, 0)),
            pl.BlockSpec((1, w2s.shape[1], 64), lambda j, bb: (j // 2, 0, 0)),
        ],
        out_specs=pl.BlockSpec((1, 1, n_rows, 64), lambda j, bb: (j, bb, 0, 0)),
        out_shape=jax.ShapeDtypeStruct((four, b, n_rows, 64), BF16),
        compiler_params=_params(("parallel", "parallel")),
        name="compress",
    )(r4, pe2, w1s, w2s)


def _bucket_breakpoints():
    n = np.arange(0, 4 * REL_MAX_DIST)
    max_exact = REL_BUCKETS // 2
    nf = np.maximum(n, max_exact).astype(np.float32)
    large = max_exact + (np.log(nf / max_exact) / np.float32(math.log(REL_MAX_DIST / max_exact))
                         * (REL_BUCKETS - max_exact)).astype(np.int32)
    large = np.minimum(large, REL_BUCKETS - 1)
    bucket = np.where(n < max_exact, n, large)
    return [int(np.argmax(bucket >= j)) for j in range(1, REL_BUCKETS)]


_BREAKS = _bucket_breakpoints()


def _bias_of_dist(dist, tab_ref, h):
    v = jnp.full(dist.shape, tab_ref[0, h], F32)
    for j, bp in enumerate(_BREAKS, start=1):
        v = jnp.where(dist >= bp, tab_ref[j, h], v)
    v = v - tab_ref[REL_BUCKETS - 1, h]
    return jnp.where(dist >= 0, v, NEG)


WIN_SPAN = WINDOW + ATT_TILE
NEAR_SPAN = 2 * ATT_TILE


def _bias_near_kernel(tab_ref, win_ref, sel_ref):
    h = pl.program_id(0)
    t = ATT_TILE
    for ref, span in ((win_ref, WIN_SPAN), (sel_ref, NEAR_SPAN)):
        r = lax.broadcasted_iota(jnp.int32, (t, span), 0)
        c = lax.broadcasted_iota(jnp.int32, (t, span), 1)
        dist = r - c + (span - t)
        ref[0] = jnp.where(dist < WINDOW, _bias_of_dist(dist, tab_ref, h), NEG)


def _bias_near(rel_table):
    t = ATT_TILE
    return pl.pallas_call(
        _bias_near_kernel,
        grid=(N_HEADS,),
        in_specs=[pl.BlockSpec(memory_space=pltpu.SMEM)],
        out_specs=[pl.BlockSpec((1, t, WIN_SPAN), lambda h: (h, 0, 0)),
                   pl.BlockSpec((1, t, NEAR_SPAN), lambda h: (h, 0, 0))],
        out_shape=[jax.ShapeDtypeStruct((N_HEADS, t, WIN_SPAN), F32),
                   jax.ShapeDtypeStruct((N_HEADS, t, NEAR_SPAN), F32)],
        compiler_params=_params(("arbitrary",)),
        name="bias_near",
    )(rel_table)


def _bias_cmp_kernel(tab_ref, o_ref, *, n_cmp):
    h = pl.program_id(0)
    i = pl.program_id(1)
    tq, nc = o_ref.shape[1], o_ref.shape[2]
    t = lax.broadcasted_iota(jnp.int32, (tq, nc), 0) + i * tq
    c = lax.broadcasted_iota(jnp.int32, (tq, nc), 1)
    dist = t - (c * CMP_STRIDE + CMP_LEN - 1)
    o_ref[0] = jnp.where(c < n_cmp, _bias_of_dist(dist, tab_ref, h), NEG)


def _bias_cmp(rel_table, seq, n_rows, tq):
    return pl.pallas_call(
        functools.partial(_bias_cmp_kernel, n_cmp=n_rows - 1),
        grid=(N_HEADS, seq // tq),
        in_specs=[pl.BlockSpec(memory_space=pltpu.SMEM)],
        out_specs=pl.BlockSpec((1, tq, n_rows), lambda h, i: (h, i, 0)),
        out_shape=jax.ShapeDtypeStruct((N_HEADS, seq, n_rows), F32),
        compiler_params=_params(("arbitrary", "arbitrary")),
        name="bias_cmp",
    )(rel_table)


def _cmp_attn_kernel(q_ref, kc_ref, vc_ref, addc_ref, ovlt_ref, eye_ref, oc_ref, sel_ref):
    i = pl.program_id(1)
    tq = q_ref.shape[1]
    n_sel = ovlt_ref.shape[0]
    kc = kc_ref[0, 0]
    vc = vc_ref[0, 0]
    t_col = lax.broadcasted_iota(jnp.int32, (tq, 1), 0) + i * tq
    row_ok = (t_col >= CMP_LEN - 1).astype(F32)
    psum = None
    for hh in range(GROUP):
        logit = _dot_nt(q_ref[hh], kc) + addc_ref[hh]
        m = jnp.max(logit, axis=-1, keepdims=True)
        e = jnp.exp(logit - m)
        p = e / jnp.sum(e, axis=-1, keepdims=True)
        oc_ref[:, 64 * hh:64 * (hh + 1)] = _dot(p.astype(BF16), vc) * row_ok
        psum = p if psum is None else psum + p
    psum = psum * row_ok
    hi = psum.astype(BF16)
    lo = (psum - hi.astype(F32)).astype(BF16)
    ovlt = ovlt_ref[...]
    imp = _dot_nt(ovlt, hi) + _dot_nt(ovlt, lo)
    s_idx = lax.broadcasted_iota(jnp.int32, (n_sel, tq), 0)
    t_blk = (lax.broadcasted_iota(jnp.int32, (n_sel, tq), 1) + i * tq) // SEL_LEN
    d_blk = t_blk - s_idx
    local = (d_blk >= 0) & (d_blk < N_LOCAL_BLOCKS)
    v = jnp.where(local, FORCE_LOCAL,
                  jnp.where(s_idx == 0, FORCE_INIT, jnp.where(d_blk >= 0, imp, -1.0)))
    rows = s_idx.astype(F32)
    sel = jnp.zeros((n_sel, tq), F32)
    for _ in range(min(SEL_TOPK, n_sel)):
        m = jnp.max(v, axis=0, keepdims=True)
        first = jnp.min(jnp.where(v == m, rows, float(n_sel)), axis=0, keepdims=True)
        pick = rows == first
        sel = jnp.where(pick, 1.0, sel)
        v = jnp.where(pick, REMOVED, v)
    sel_ref[0] = _dot_nt(eye_ref[...], sel.astype(BF16)).astype(BF16)


def _cmp_attn(q4, cmp4, addc, ovlt, batch, seq):
    tq = ATT_TILE
    nt = seq // tq
    n_rows = cmp4.shape[2]
    n_sel = ovlt.shape[0]
    eye = jnp.eye(tq, dtype=BF16)
    return pl.pallas_call(
        _cmp_attn_kernel,
        grid=(batch * N_KV_HEADS, nt),
        in_specs=[
            pl.BlockSpec((GROUP, tq, 64), lambda g, i: (g % 2, (g // 2) * nt + i, 0)),
            pl.BlockSpec((1, 1, n_rows, 64), lambda g, i: (g % 2, g // 2, 0, 0)),
            pl.BlockSpec((1, 1, n_rows, 64), lambda g, i: (2 + g % 2, g // 2, 0, 0)),
            pl.BlockSpec((GROUP, tq, n_rows), lambda g, i: (g % 2, i, 0)),
            pl.BlockSpec(ovlt.shape, lambda g, i: (0, 0)),
            pl.BlockSpec((tq, tq), lambda g, i: (0, 0)),
        ],
        out_specs=[
            pl.BlockSpec((tq, GROUP * 64), lambda g, i: ((g // 2) * nt + i, g % 2)),
            pl.BlockSpec((1, tq, n_sel), lambda g, i: (g, i, 0)),
        ],
        out_shape=[
            jax.ShapeDtypeStruct((batch * seq, N_HEADS * 64), F32),
            jax.ShapeDtypeStruct((batch * N_KV_HEADS, seq, n_sel), BF16),
        ],
        compiler_params=_params(("parallel", "arbitrary")),
        name="cmp_attn",
    )(q4, cmp4, cmp4, addc, ovlt, eye)


def _softmax_step(qs, k, v1, mask, add, carry):
    tq = qs.shape[0] // GROUP
    s = _dot_nt(qs, k).reshape(GROUP, tq, k.shape[0])
    if mask is not None:
        s = jnp.where(mask[None] > 0.5, s, NEG)
    if add is not None:
        s = s + add
    m_tile = jnp.max(s, axis=-1, keepdims=True)
    if carry is None:
        m_new = m_tile
    else:
        m, acc = carry
        m_new = jnp.maximum(m, m_tile)
    p = jnp.exp(s - m_new)
    pv = _dot(p.reshape(GROUP * tq, k.shape[0]).astype(BF16), v1).reshape(GROUP, tq, v1.shape[1])
    if carry is None:
        return m_new, pv
    return m_new, jnp.exp(m - m_new) * acc + pv


FAR_SLAB = 4 * ATT_TILE


def _sel_win_kernel(q_ref, ks_ref, vs_ref, kw_ref, vw_ref, sel_ref, enear_ref, eslab_ref, etile_ref,
                    winadd_ref, seladd_ref, oc_ref, gates_ref, o_ref):
    i = pl.program_id(1)
    tq = q_ref.shape[1]
    qs = q_ref[...].reshape(GROUP * tq, 64)
    sel = sel_ref[0]

    def rows(ref, start, size):
        return ref[0, 0, pl.ds(pl.multiple_of(start, ATT_TILE), size), :]

    def hide_padding(add, n_pad):
        col = lax.broadcasted_iota(jnp.int32, (1, 1, add.shape[2]), 2)
        return jnp.where(col < n_pad, NEG, add)

    w0 = i * tq
    _, acc_w = _softmax_step(qs, rows(kw_ref, w0, WIN_SPAN), rows(vw_ref, w0, WIN_SPAN), None,
                             hide_padding(winadd_ref[...], WINDOW - i * tq), None)

    n0 = w0 + WINDOW - tq
    carry = _softmax_step(qs, rows(ks_ref, n0, NEAR_SPAN), rows(vs_ref, n0, NEAR_SPAN),
                          _dot(sel, enear_ref[i]), hide_padding(seladd_ref[...], tq - i * tq), None)
    n_far = jnp.maximum(i - 1, 0)
    per_slab = FAR_SLAB // tq
    n_slab = n_far // per_slab

    def far_slab(j, c):
        start = WINDOW + j * FAR_SLAB
        return _softmax_step(qs, rows(ks_ref, start, FAR_SLAB), rows(vs_ref, start, FAR_SLAB),
                             _dot(sel, eslab_ref[j]), None, c)

    def far_tile(j, c):
        start = WINDOW + j * tq
        return _softmax_step(qs, rows(ks_ref, start, tq), rows(vs_ref, start, tq),
                             _dot(sel, etile_ref[j]), None, c)

    carry = lax.fori_loop(0, n_slab, far_slab, carry)
    _, acc_s = lax.fori_loop(n_slab * per_slab, n_far, far_tile, carry)
    o_s = acc_s[..., :64] / acc_s[..., 64:65]
    o_w = acc_w[..., :64] / acc_w[..., 64:65]

    g = jax.nn.sigmoid(gates_ref[0])
    for hh in range(GROUP):
        o_c = oc_ref[:, 64 * hh:64 * (hh + 1)]
        o_ref[:, 64 * hh:64 * (hh + 1)] = (g[:, 3 * hh:3 * hh + 1] * o_c
                                            + g[:, 3 * hh + 1:3 * hh + 2] * o_s[hh]
                                            + g[:, 3 * hh + 2:3 * hh + 3] * o_w[hh])


def _sel_win_attn(q4, kk5, vv5, sel, enear, eslab, etile, winadd, seladd, o_c, gates2, batch, seq):
    tq = ATT_TILE
    nt = seq // tq
    n_sel = sel.shape[2]
    padded = kk5.shape[2]

    def kv_spec(arr, base):
        return pl.BlockSpec((1, 1, padded, arr.shape[3]), lambda g, i: (base + g % 2, g // 2, 0, 0))

    whole = lambda arr: pl.BlockSpec(arr.shape, lambda g, i: (0,) * arr.ndim)
    return pl.pallas_call(
        _sel_win_kernel,
        grid=(batch * N_KV_HEADS, nt),
        in_specs=[
            pl.BlockSpec((GROUP, tq, 64), lambda g, i: (g % 2, (g // 2) * nt + i, 0)),
            kv_spec(kk5, 0), kv_spec(vv5, 0), kv_spec(kk5, 2), kv_spec(vv5, 2),
            pl.BlockSpec((1, tq, n_sel), lambda g, i: (g, i, 0)),
            whole(enear), whole(eslab), whole(etile),
            pl.BlockSpec((GROUP, tq, WIN_SPAN), lambda g, i: (g % 2, 0, 0)),
            pl.BlockSpec((GROUP, tq, NEAR_SPAN), lambda g, i: (g % 2, 0, 0)),
            pl.BlockSpec((tq, GROUP * 64), lambda g, i: ((g // 2) * nt + i, g % 2)),
            pl.BlockSpec((1, tq, LANES), lambda g, i: (g % 2, (g // 2) * nt + i, 0)),
        ],
        out_specs=pl.BlockSpec((tq, GROUP * 64), lambda g, i: ((g // 2) * nt + i, g % 2)),
        out_shape=jax.ShapeDtypeStruct((batch * seq, N_HEADS * 64), F32),
        compiler_params=_params(("parallel", "arbitrary")),
        name="sel_win_attn",
    )(q4, kk5, vv5, kk5, vv5, sel, enear, eslab, etile, winadd, seladd, o_c, gates2)


def _rglru_kernel(xrg_ref, xg_ref, cw_ref, cb_ref, wa_ref, ba_ref, wx_ref, bx_ref, lam_ref, gn_ref,
                  o_ref, xs_ref, h_ref):
    step = pl.program_id(1)
    tc = xrg_ref.shape[0]

    @pl.when(step == 0)
    def _():
        xs_ref[0:8, :] = jnp.zeros((8, xs_ref.shape[1]), F32)
        h_ref[...] = jnp.zeros(h_ref.shape, F32)

    xs_ref[8:, :] = xrg_ref[...]
    xc = cb_ref[...] + cw_ref[CONV_WIDTH - 1:CONV_WIDTH, :] * xs_ref[8:, :]
    for k in range(1, CONV_WIDTH):
        xc = xc + cw_ref[CONV_WIDTH - 1 - k:CONV_WIDTH - k, :] * xs_ref[pl.ds(8 - k, tc), :]
    xs_ref[0:8, :] = xs_ref[tc:tc + 8, :]

    xcb = xc.astype(BF16)
    r = jax.nn.sigmoid(_dot(xcb, wa_ref[...]) + ba_ref[...])
    gi = jax.nn.sigmoid(_dot(xcb, wx_ref[...]) + bx_ref[...])
    lam = lam_ref[...]
    softplus_neg = jnp.maximum(-lam, 0.0) + jnp.log1p(jnp.exp(-jnp.abs(lam)))
    log_a = -LRU_C * r * softplus_neg
    a = jnp.exp(log_a)
    b = jnp.sqrt(1.0 - jnp.exp(2.0 * log_a)) * (gi * xc)

    row = lax.broadcasted_iota(jnp.int32, (tc, 1), 0)
    s = 1
    while s < tc:
        valid = row >= s
        b = jnp.where(valid, a * pltpu.roll(b, s, 0) + b, b)
        a = jnp.where(valid, a * pltpu.roll(a, s, 0), a)
        s *= 2
    h = b + a * h_ref[0:1, :]
    h_ref[0:1, :] = h[tc - 1:tc, :]
    o_ref[...] = _rms(h * jax.nn.gelu(xg_ref[...]), gn_ref[...]).astype(BF16)


def _rglru(xrg, cw, cb, wa, ba, wx, bx, lam, gn, batch, seq, tc):
    nt = seq // tc
    w = cw.shape[1]
    vec = pl.BlockSpec((1, w), lambda b, i: (0, 0))
    return pl.pallas_call(
        _rglru_kernel,
        grid=(batch, nt),
        in_specs=[
            pl.BlockSpec((tc, w), lambda b, i: (b * nt + i, 0)),
            pl.BlockSpec((tc, w), lambda b, i: (b * nt + i, 1)),
            pl.BlockSpec((CONV_WIDTH, w), lambda b, i: (0, 0)),
            vec,
            pl.BlockSpec((w, w), lambda b, i: (0, 0)), vec,
            pl.BlockSpec((w, w), lambda b, i: (0, 0)), vec,
            vec, vec,
        ],
        out_specs=pl.BlockSpec((tc, w), lambda b, i: (b * nt + i, 0)),
        out_shape=jax.ShapeDtypeStruct((batch * seq, w), BF16),
        scratch_shapes=[pltpu.VMEM((tc + 8, w), F32), pltpu.VMEM((8, w), F32)],
        compiler_params=_params(("parallel", "arbitrary")),
        name="rglru",
    )(xrg, xrg, cw, cb, wa, ba, wx, bx, lam, gn)


def _out_proj_kernel(a_ref, l_ref, x_ref, ga_ref, woa_ref, wol_ref, gf_ref, wq_ref, sk_ref,
                     h1_ref, xt_ref, st_ref):
    an = _rms(a_ref[...], ga_ref[...]).astype(BF16)
    h1 = x_ref[...] + _dot(an, woa_ref[...]) + _dot(l_ref[...], wol_ref[...])
    h1_ref[...] = h1
    xn = _rms(h1, gf_ref[...])
    xt_ref[...] = xn.T.astype(BF16)
    q = _dot(xn.astype(BF16), wq_ref[...])
    for hp in range(2 * PEER_HEADS):
        qs = q[:, LANES * hp:LANES * (hp + 1)].astype(BF16)
        st_ref[hp] = _dot_nt(sk_ref[hp % 2], qs)


def _out_proj(a_out, l_n, x2, ga, woa, wol, gf, wq, sk, tm):
    n, d = x2.shape
    full = lambda arr: pl.BlockSpec(arr.shape, lambda i: (0,) * arr.ndim)
    return pl.pallas_call(
        _out_proj_kernel,
        grid=(n // tm,),
        in_specs=[
            pl.BlockSpec((tm, a_out.shape[1]), lambda i: (i, 0)),
            pl.BlockSpec((tm, l_n.shape[1]), lambda i: (i, 0)),
            pl.BlockSpec((tm, d), lambda i: (i, 0)),
            full(ga), full(woa), full(wol), full(gf), full(wq), full(sk),
        ],
        out_specs=[
            pl.BlockSpec((tm, d), lambda i: (i, 0)),
            pl.BlockSpec((d, tm), lambda i: (0, i)),
            pl.BlockSpec((2 * PEER_HEADS, PEER_N_KEYS, tm), lambda i: (0, 0, i)),
        ],
        out_shape=[
            jax.ShapeDtypeStruct((n, d), F32),
            jax.ShapeDtypeStruct((d, n), BF16),
            jax.ShapeDtypeStruct((2 * PEER_HEADS, PEER_N_KEYS, n), F32),
        ],
        compiler_params=_params(("parallel",)),
        name="out_proj",
    )(a_out, l_n, x2, ga, woa, wol, gf, wq, sk)


N_TOP = PEER_TOPK + 1
_PAIRS = [(i, j) for i in range(N_TOP) for j in range(N_TOP) if (i + 1) * (j + 1) <= N_TOP]


def _top_values(s):
    rows = lax.broadcasted_iota(jnp.int32, s.shape, 0).astype(F32)
    rank = jnp.full(s.shape, float(PEER_TOPK), F32)
    out = []
    for r in range(N_TOP):
        m = jnp.max(s, axis=0, keepdims=True)
        out.append(m)
        pick = rows == jnp.min(jnp.where(s == m, rows, float(s.shape[0])), axis=0, keepdims=True)
        s = jnp.where(pick, REMOVED, s)
        if r < PEER_TOPK:
            rank = jnp.where(pick, float(r), rank)
    return out, rank


def _peer_stats_kernel(st_ref, r2_ref, e2_ref, l_ref, e1_ref):
    tops = [[], []]
    rank1 = []
    for h in range(PEER_HEADS):
        vals, rank = _top_values(st_ref[2 * h])
        tops[0].append(vals)
        rank1.append(rank)
        vals, rank = _top_values(st_ref[2 * h + 1])
        tops[1].append(vals)
        r2_ref[h] = rank.astype(BF16)
    a = [jnp.concatenate([tops[0][h][i] for h in range(PEER_HEADS)], axis=0) for i in range(N_TOP)]
    b = [jnp.concatenate([tops[1][h][j] for h in range(PEER_HEADS)], axis=0) for j in range(N_TOP)]
    cand = [a[i] + b[j] for (i, j) in _PAIRS]
    work = list(cand)
    kth = []
    for _ in range(N_TOP):
        m = functools.reduce(jnp.maximum, work)
        kth.append(m)
        found = jnp.zeros(m.shape, jnp.bool_)
        for k in range(len(work)):
            hit = (work[k] == m) & jnp.logical_not(found)
            work[k] = jnp.where(hit, REMOVED, work[k])
            found = found | hit
    thr = 0.5 * (kth[PEER_TOPK - 1] + kth[PEER_TOPK])
    top = a[0] + b[0]
    z = functools.reduce(lambda u, w: u + w, [jnp.where(c >= thr, jnp.exp(c - top), 0.0) for c in cand])
    n_sel = [functools.reduce(lambda u, w: u + w,
                              [(a[r] + b[j] >= thr).astype(F32) for j in range(PEER_TOPK)])
             for r in range(PEER_TOPK)]
    for h in range(PEER_HEADS):
        row = slice(h, h + 1)
        l_ref[h] = functools.reduce(
            lambda u, w: u + w, [jnp.where(rank1[h] == float(r), n_sel[r][row], 0.0) for r in range(PEER_TOPK)])
        e1_ref[h] = jnp.exp(st_ref[2 * h] - a[0][row])
        e2_ref[h] = (jnp.exp(st_ref[2 * h + 1] - b[0][row]) / z[row]).astype(BF16)


def _peer_stats(st, tl):
    hp, nk, n = st.shape
    out_spec = pl.BlockSpec((PEER_HEADS, nk, tl), lambda i: (0, 0, i))
    return pl.pallas_call(
        _peer_stats_kernel,
        grid=(n // tl,),
        in_specs=[pl.BlockSpec((hp, nk, tl), lambda i: (0, 0, i))],
        out_specs=[out_spec] * 4,
        out_shape=[
            jax.ShapeDtypeStruct((PEER_HEADS, nk, n), BF16),
            jax.ShapeDtypeStruct((PEER_HEADS, nk, n), BF16),
            jax.ShapeDtypeStruct((PEER_HEADS, nk, n), F32),
            jax.ShapeDtypeStruct((PEER_HEADS, nk, n), F32),
        ],
        compiler_params=_params(("parallel",)),
        name="peer_stats",
    )(st)


def _peer_main_kernel(u_ref, vt_ref, xt_ref, r2_ref, e2_ref, l_ref, e1_ref, h1_ref, o_ref, acc_ref):
    c = pl.program_id(1)
    ce = u_ref.shape[0]
    tm = xt_ref.shape[1]
    nk = PEER_N_KEYS

    @pl.when(c == 0)
    def _():
        acc_ref[...] = jnp.zeros(acc_ref.shape, F32)

    act = jax.nn.gelu(_dot(u_ref[...], xt_ref[...]))
    w_parts = []
    for ii in range(ce // nk):
        i1 = c * (ce // nk) + ii
        g = None
        for h in range(PEER_HEADS):
            n_sel = jnp.broadcast_to(l_ref[h, pl.ds(i1, 1), :].astype(BF16), (nk, tm))
            e1 = jnp.broadcast_to(e1_ref[h, pl.ds(i1, 1), :].astype(BF16), (nk, tm))
            e2 = e2_ref[h]
            term = jnp.where(r2_ref[h] < n_sel, e2, jnp.zeros_like(e2)) * e1
            g = term if g is None else g + term
        w_parts.append(act[ii * nk:(ii + 1) * nk, :].astype(BF16) * g)
    w = jnp.concatenate(w_parts, axis=0)
    acc_ref[...] += _dot(vt_ref[...], w)

    @pl.when(c == pl.num_programs(1) - 1)
    def _():
        o_ref[...] = h1_ref[...] + acc_ref[...].T


def _peer_main(u_b, vt_b, xt, r2, e2, n_sel, e1, h1, tm, ce):
    ne, d = u_b.shape
    n = xt.shape[1]
    per_tok = pl.BlockSpec((PEER_HEADS, PEER_N_KEYS, tm), lambda t, c: (0, 0, t))
    return pl.pallas_call(
        _peer_main_kernel,
        grid=(n // tm, ne // ce),
        in_specs=[
            pl.BlockSpec((ce, d), lambda t, c: (c, 0)),
            pl.BlockSpec((d, ce), lambda t, c: (0, c)),
            pl.BlockSpec((d, tm), lambda t, c: (0, t)),
            per_tok, per_tok, per_tok, per_tok,
            pl.BlockSpec((tm, d), lambda t, c: (t, 0)),
        ],
        out_specs=pl.BlockSpec((tm, d), lambda t, c: (t, 0)),
        out_shape=jax.ShapeDtypeStruct((n, d), F32),
        scratch_shapes=[pltpu.VMEM((d, tm), F32)],
        compiler_params=_params(("parallel", "arbitrary")),
        name="peer_main",
    )(u_b, vt_b, xt, r2, e2, n_sel, e1, h1)


def _ple_final_kernel(h_ref, p_ref, gp_ref, wg_ref, bg_ref, wp_ref, gfin_ref, o_ref):
    h = h_ref[...]
    gate = jax.nn.sigmoid(_dot(_rms(h, gp_ref[...]).astype(BF16), wg_ref[...]) + bg_ref[...])
    h = h + gate * _dot(p_ref[...].astype(BF16), wp_ref[...])
    o_ref[...] = _rms(h, gfin_ref[...])


def _ple_final(h2, p2, gp, wg, bg, wp, gfin, tm):
    n, d = h2.shape
    full = lambda arr: pl.BlockSpec(arr.shape, lambda i: (0,) * arr.ndim)
    return pl.pallas_call(
        _ple_final_kernel,
        grid=(n // tm,),
        in_specs=[
            pl.BlockSpec((tm, d), lambda i: (i, 0)),
            pl.BlockSpec((tm, p2.shape[1]), lambda i: (i, 0)),
            full(gp), full(wg), full(bg), full(wp), full(gfin),
        ],
        out_specs=pl.BlockSpec((tm, d), lambda i: (i, 0)),
        out_shape=jax.ShapeDtypeStruct((n, d), F32),
        compiler_params=_params(("parallel",)),
        name="ple_final",
    )(h2, p2, gp, wg, bg, wp, gfin)


def _overlap_t(seq, n_rows):
    n_cmp = n_rows - 1
    cmp_start = np.arange(n_cmp) * CMP_STRIDE
    sel_start = np.arange(seq // SEL_LEN) * SEL_LEN
    ov = np.clip(np.minimum(cmp_start[:, None] + CMP_LEN, sel_start[None, :] + SEL_LEN)
                 - np.maximum(cmp_start[:, None], sel_start[None, :]), 0, None).astype(np.float32) / CMP_LEN
    out = np.zeros((seq // SEL_LEN, n_rows), np.float32)
    out[:, :n_cmp] = ov.T
    return out


def _block_expand(seq, starts, width):
    pos = np.asarray(starts).reshape(-1, 1, 1) + np.arange(width).reshape(1, 1, width)
    blk = np.where(pos >= 0, pos // SEL_LEN, -1)
    return (blk == np.arange(seq // SEL_LEN).reshape(1, -1, 1)).astype(np.float32)


def _block_diag(w):
    nb, bd, _ = w.shape
    out = jnp.zeros((nb * bd, nb * bd), w.dtype)
    for j in range(nb):
        out = out.at[j * bd:(j + 1) * bd, j * bd:(j + 1) * bd].set(w[j])
    return out


def _layer(h, p_i, attn_norm, w_in, cmp_k_pe, cmp_k_w1, cmp_k_w2, cmp_v_pe, cmp_v_w1, cmp_v_w2,
           rel_table, conv_w, conv_b, lru_wa, lru_ba, lru_wx, lru_bx, lru_lambda,
           grp_norm_attn, grp_norm_lru, w_out, ffn_norm, peer_wq, peer_subkeys, peer_u, peer_v,
           ple_norm, ple_wgate, ple_bgate, ple_proj, *, tm, tc, tl, ce):
    batch, seq, d = h.shape
    n = batch * seq
    x2 = h.reshape(n, d)
    aw = N_HEADS * HEAD_DIM
    kvw = N_KV_HEADS * HEAD_DIM
    lw = d - aw

    o = 0
    cols = {}
    for name, width in (("q", aw), ("kc", kvw), ("vc", kvw), ("ks", kvw), ("vs", kvw), ("kw", kvw),
                        ("vw", kvw), ("gates", N_HEADS * 3), ("xr", lw), ("xg", lw)):
        cols[name] = w_in[:, o:o + width]
        o += width
    gate_pad = [jnp.pad(cols["gates"][:, N_GATE_COLS * k:N_GATE_COLS * (k + 1)], ((0, 0), (0, LANES - N_GATE_COLS)))
                for k in range(N_KV_HEADS)]
    wf = jnp.concatenate([cols["xr"], cols["xg"], cols["kc"], cols["vc"]] + gate_pad, axis=1).astype(BF16)
    wb = jnp.concatenate([cols["q"], cols["ks"], cols["kw"], cols["vs"], cols["vw"]], axis=1).astype(BF16)

    xrg, kvc, gates2, q4, kk, vv = _in_proj(x2, attn_norm.reshape(1, d), wf, wb, tm)

    n_rows = seq // CMP_STRIDE
    r4 = kvc.reshape(4, batch, n_rows, CMP_STRIDE * HEAD_DIM)
    pe2 = jnp.stack([cmp_k_pe.reshape(1, -1), cmp_v_pe.reshape(1, -1)])
    w1s = jnp.stack([cmp_k_w1, cmp_v_w1]).astype(BF16)
    w2s = jnp.stack([cmp_k_w2, cmp_v_w2]).astype(BF16)
    cmp4 = _compress(r4, pe2, w1s, w2s)

    winadd, seladd = _bias_near(rel_table)
    addc = _bias_cmp(rel_table, seq, n_rows, min(seq, 512))
    ovlt = jnp.asarray(_overlap_t(seq, n_rows), BF16)
    o_c, sel = _cmp_attn(q4, cmp4, addc, ovlt, batch, seq)

    nt = seq // ATT_TILE
    tile_starts = np.arange(nt) * ATT_TILE
    enear = jnp.asarray(_block_expand(seq, tile_starts - ATT_TILE, NEAR_SPAN), BF16)
    eslab = jnp.asarray(_block_expand(seq, np.arange(max(seq // FAR_SLAB, 1)) * FAR_SLAB, FAR_SLAB), BF16)
    etile = jnp.asarray(_block_expand(seq, tile_starts, ATT_TILE), BF16)
    front = ((0, 0), (0, 0), (WINDOW, 0), (0, 0))
    kk5 = jnp.pad(kk.reshape(4, batch, seq, HEAD_DIM), front)
    vv5 = jnp.pad(vv.reshape(4, batch, seq, LANES), front)
    a_out = _sel_win_attn(q4, kk5, vv5, sel, enear, eslab, etile, winadd, seladd, o_c, gates2, batch, seq)

    l_n = _rglru(xrg, conv_w.reshape(CONV_WIDTH, lw), conv_b.reshape(1, lw),
                 _block_diag(lru_wa).astype(BF16), lru_ba.reshape(1, lw),
                 _block_diag(lru_wx).astype(BF16), lru_bx.reshape(1, lw),
                 lru_lambda.reshape(1, lw), grp_norm_lru.reshape(1, lw), batch, seq, tc)

    w_out_b = w_out.astype(BF16)
    h1, xt, st = _out_proj(a_out, l_n, x2, grp_norm_attn.reshape(1, aw), w_out_b[:aw], w_out_b[aw:],
                           ffn_norm.reshape(1, d), peer_wq.astype(BF16), peer_subkeys.astype(BF16), tm)
    r2, e2, n_sel, e1 = _peer_stats(st, tl)
    h2 = _peer_main(peer_u.astype(BF16), peer_v.T.astype(BF16), xt, r2, e2, n_sel, e1, h1, tm, ce)
    ple_args = (h2, p_i.reshape(n, -1), ple_norm.reshape(1, d), ple_wgate.astype(BF16), ple_bgate.reshape(1, d),
                ple_proj.astype(BF16))
    return ple_args, dict(a_out=a_out, l_n=l_n, h1=h1)


def kernel(x, p, attn_norm, w_in, cmp_k_pe, cmp_k_w1, cmp_k_w2, cmp_v_pe, cmp_v_w1, cmp_v_w2, rel_table, conv_w, conv_b, lru_wa, lru_ba, lru_wx, lru_bx, lru_lambda, grp_norm_attn, grp_norm_lru, w_out, ffn_norm, peer_wq, peer_subkeys, peer_u, peer_v, ple_norm, ple_wgate, ple_bgate, ple_proj, final_norm):
    batch, seq, d = x.shape
    assert w_in.shape[0] == 1, "single-layer stack only"
    tm = min(512, seq)
    (h2, p2, gp, wg, bg, wp), _ = _layer(
        x, p[0], attn_norm[0], w_in[0], cmp_k_pe[0], cmp_k_w1[0], cmp_k_w2[0], cmp_v_pe[0], cmp_v_w1[0],
        cmp_v_w2[0], rel_table, conv_w[0], conv_b[0], lru_wa[0], lru_ba[0], lru_wx[0], lru_bx[0],
        lru_lambda[0], grp_norm_attn[0], grp_norm_lru[0], w_out[0], ffn_norm[0], peer_wq[0],
        peer_subkeys[0], peer_u[0], peer_v[0], ple_norm[0], ple_wgate[0], ple_bgate[0], ple_proj[0],
        tm=tm, tc=min(256, seq), tl=min(256, seq), ce=512)
    out = _ple_final(h2, p2, gp, wg, bg, wp, final_norm.reshape(1, d), tm)
    return out.reshape(batch, seq, d)
```

```python
import functools
import math

import numpy as np
import jax
import jax.numpy as jnp
from jax import lax
from jax.experimental import pallas as pl
from jax.experimental.pallas import tpu as pltpu

F32 = jnp.float32
BF16 = jnp.bfloat16

N_HEADS = 8
HEAD_DIM = 64
N_KV_HEADS = 2
GROUP = N_HEADS // N_KV_HEADS
LRU_C = 8.0
CONV_WIDTH = 4
CMP_LEN = 32
CMP_STRIDE = 16
SEL_LEN = 64
SEL_TOPK = 16
N_LOCAL_BLOCKS = 2
WINDOW = 512
FORCE_LOCAL = 2.0e4
FORCE_INIT = 1.0e4
REL_BUCKETS = 32
REL_MAX_DIST = 128
PEER_HEADS = 8
PEER_N_KEYS = 128
PEER_TOPK = 16
EPS = 1e-6
NEG = -1e30
REMOVED = -3.0e38

LANES = 128
BF16_ROWS = 16
VMEM_LIMIT = 56 * 1024 * 1024

ATT_TILE = 128
CMP_TILE = 256
N_GATE_COLS = GROUP * 3


def _dot(a, b):
    return jnp.dot(a, b, preferred_element_type=F32)


def _dot_nt(a, b):
    return lax.dot_general(a, b, (((1,), (1,)), ((), ())), preferred_element_type=F32)


def _gelu_tanh(x):
    c = math.sqrt(2.0 / math.pi)
    half = 0.5 * x
    return half + half * jnp.tanh(x * (c + (c * 0.044715) * (x * x)))


def _rms(x, g):
    return x * lax.rsqrt(jnp.mean(x * x, axis=-1, keepdims=True) + EPS) * g


def _params(sem, vmem=VMEM_LIMIT, flags=None):
    return pltpu.CompilerParams(dimension_semantics=sem, vmem_limit_bytes=vmem, flags=flags)


def _in_proj_kernel(x_ref, g_ref, wf_ref, wb_ref, xrg_ref, kvc_ref, gates_ref, q4_ref, kk_ref, vv_ref):
    xb = _rms(x_ref[...], g_ref[...]).astype(BF16)
    yf = _dot(xb, wf_ref[...])
    xrg_ref[...] = yf[:, :1024]
    for j in range(4):
        kvc_ref[j] = yf[:, 1024 + 64 * j:1024 + 64 * (j + 1)]
    gates_ref[0] = yf[:, 1280:1408]
    gates_ref[1] = yf[:, 1408:1536]
    yb = _dot(xb, wb_ref[...])
    for h in range(N_HEADS):
        q4_ref[h] = (yb[:, 64 * h:64 * (h + 1)] * (HEAD_DIM ** -0.5)).astype(BF16)
    ones_col = (lax.broadcasted_iota(jnp.int32, (yb.shape[0], 64), 1) == 0).astype(F32)
    for j in range(4):
        kk_ref[j] = yb[:, 512 + 64 * j:512 + 64 * (j + 1)].astype(BF16)
        vv_ref[j] = jnp.concatenate([yb[:, 768 + 64 * j:768 + 64 * (j + 1)], ones_col], axis=1).astype(BF16)


def _in_proj(x2, g, wf, wb, tm):
    n, d = x2.shape
    return pl.pallas_call(
        _in_proj_kernel,
        grid=(n // tm,),
        in_specs=[
            pl.BlockSpec((tm, d), lambda i: (i, 0)),
            pl.BlockSpec((1, d), lambda i: (0, 0)),
            pl.BlockSpec(wf.shape, lambda i: (0, 0)),
            pl.BlockSpec(wb.shape, lambda i: (0, 0)),
        ],
        out_specs=[
            pl.BlockSpec((tm, 1024), lambda i: (i, 0)),
            pl.BlockSpec((4, tm, 64), lambda i: (0, i, 0)),
            pl.BlockSpec((2, tm, LANES), lambda i: (0, i, 0)),
            pl.BlockSpec((8, tm, 64), lambda i: (0, i, 0)),
            pl.BlockSpec((4, tm, 64), lambda i: (0, i, 0)),
            pl.BlockSpec((4, tm, LANES), lambda i: (0, i, 0)),
        ],
        out_shape=[
            jax.ShapeDtypeStruct((n, 1024), F32),
            jax.ShapeDtypeStruct((4, n, 64), F32),
            jax.ShapeDtypeStruct((2, n, LANES), F32),
            jax.ShapeDtypeStruct((8, n, 64), BF16),
            jax.ShapeDtypeStruct((4, n, 64), BF16),
            jax.ShapeDtypeStruct((4, n, LANES), BF16),
        ],
        compiler_params=_params(("parallel",)),
        name="in_proj",
    )(x2, g, wf, wb)


def _compress_kernel(r_ref, pe_ref, w1_ref, w2_ref, o_ref):
    r = r_ref[0, 0]
    n_rows = r.shape[0]
    half = r.shape[1]
    pe = pe_ref[0]
    top = _dot((r + pe[:, :half]).astype(BF16), w1_ref[0, :half, :])
    bot = _dot((r + pe[:, half:]).astype(BF16), w1_ref[0, half:, :])
    hid = top + pltpu.roll(bot, n_rows - 1, 0)
    act = jax.nn.gelu(hid)
    out = _dot(act.astype(BF16), w2_ref[0])
    row = lax.broadcasted_iota(jnp.int32, out.shape, 0)
    o_ref[0, 0] = jnp.where(row < n_rows - 1, out, 0.0).astype(BF16)


def _compress(r4, pe2, w1s, w2s):
    four, b, n_rows, width = r4.shape
    return pl.pallas_call(
        _compress_kernel,
        grid=(four, b),
        in_specs=[
            pl.BlockSpec((1, 1, n_rows, width), lambda j, bb: (j, bb, 0, 0)),
            pl.BlockSpec((1, 1, 2 * width), lambda j, bb: (j // 2, 0, 0)),
            pl.BlockSpec((1, 2 * width, w1s.shape[2]), lambda j, bb: (j // 2, 0, 0)),
            pl.BlockSpec((1, w2s.shape[1], 64), lambda j, bb: (j // 2, 0, 0)),
        ],
        out_specs=pl.BlockSpec((1, 1, n_rows, 64), lambda j, bb: (j, bb, 0, 0)),
        out_shape=jax.ShapeDtypeStruct((four, b, n_rows, 64), BF16),
        compiler_params=_params(("parallel", "parallel")),
        name="compress",
    )(r4, pe2, w1s, w2s)


def _bucket_breakpoints():
    n = np.arange(0, 4 * REL_MAX_DIST)
    max_exact = REL_BUCKETS // 2
    nf = np.maximum(n, max_exact).astype(np.float32)
    large = max_exact + (np.log(nf / max_exact) / np.float32(math.log(REL_MAX_DIST / max_exact))
                         * (REL_BUCKETS - max_exact)).astype(np.int32)
    large = np.minimum(large, REL_BUCKETS - 1)
    bucket = np.where(n < max_exact, n, large)
    return [int(np.argmax(bucket >= j)) for j in range(1, REL_BUCKETS)]


_BREAKS = _bucket_breakpoints()


def _bias_of_dist(dist, tab_ref, h):
    v = jnp.full(dist.shape, tab_ref[0, h], F32)
    for j, bp in enumerate(_BREAKS, start=1):
        v = jnp.where(dist >= bp, tab_ref[j, h], v)
    v = v - tab_ref[REL_BUCKETS - 1, h]
    return jnp.where(dist >= 0, v, NEG)


WIN_SPAN = WINDOW + ATT_TILE
NEAR_SPAN = 2 * ATT_TILE


def _bias_near_kernel(tab_ref, win_ref, sel_ref):
    h = pl.program_id(0)
    t = ATT_TILE
    for ref, span in ((win_ref, WIN_SPAN), (sel_ref, NEAR_SPAN)):
        r = lax.broadcasted_iota(jnp.int32, (t, span), 0)
        c = lax.broadcasted_iota(jnp.int32, (t, span), 1)
        dist = r - c + (span - t)
        ref[0] = jnp.where(dist < WINDOW, _bias_of_dist(dist, tab_ref, h), NEG)


def _bias_near(rel_table):
    t = ATT_TILE
    return pl.pallas_call(
        _bias_near_kernel,
        grid=(N_HEADS,),
        in_specs=[pl.BlockSpec(memory_space=pltpu.SMEM)],
        out_specs=[pl.BlockSpec((1, t, WIN_SPAN), lambda h: (h, 0, 0)),
                   pl.BlockSpec((1, t, NEAR_SPAN), lambda h: (h, 0, 0))],
        out_shape=[jax.ShapeDtypeStruct((N_HEADS, t, WIN_SPAN), F32),
                   jax.ShapeDtypeStruct((N_HEADS, t, NEAR_SPAN), F32)],
        compiler_params=_params(("arbitrary",)),
        name="bias_near",
    )(rel_table)


def _bias_cmp_kernel(tab_ref, o_ref, *, n_cmp):
    h = pl.program_id(0)
    i = pl.program_id(1)
    tq, nc = o_ref.shape[1], o_ref.shape[2]
    t = lax.broadcasted_iota(jnp.int32, (tq, nc), 0) + i * tq
    c = lax.broadcasted_iota(jnp.int32, (tq, nc), 1)
    dist = t - (c * CMP_STRIDE + CMP_LEN - 1)
    o_ref[0] = jnp.where(c < n_cmp, _bias_of_dist(dist, tab_ref, h), NEG)


def _bias_cmp(rel_table, seq, n_rows, tq):
    return pl.pallas_call(
        functools.partial(_bias_cmp_kernel, n_cmp=n_rows - 1),
        grid=(N_HEADS, seq // tq),
        in_specs=[pl.BlockSpec(memory_space=pltpu.SMEM)],
        out_specs=pl.BlockSpec((1, tq, n_rows), lambda h, i: (h, i, 0)),
        out_shape=jax.ShapeDtypeStruct((N_HEADS, seq, n_rows), F32),
        compiler_params=_params(("arbitrary", "arbitrary")),
        name="bias_cmp",
    )(rel_table)


def _cmp_attn_kernel(q_ref, kc_ref, vc_ref, addc_ref, ovlt_ref, eye_ref, oc_ref, sel_ref):
    i = pl.program_id(1)
    tq = q_ref.shape[1]
    n_sel = ovlt_ref.shape[0]
    kc = kc_ref[0, 0]
    vc = vc_ref[0, 0]
    t_col = lax.broadcasted_iota(jnp.int32, (tq, 1), 0) + i * tq
    row_ok = (t_col >= CMP_LEN - 1).astype(F32)
    psum = None
    for hh in range(GROUP):
        logit = _dot_nt(q_ref[hh], kc) + addc_ref[hh]
        m = jnp.max(logit, axis=-1, keepdims=True)
        e = jnp.exp(logit - m)
        p = e / jnp.sum(e, axis=-1, keepdims=True)
        oc_ref[:, 64 * hh:64 * (hh + 1)] = _dot(p.astype(BF16), vc) * row_ok
        psum = p if psum is None else psum + p
    psum = psum * row_ok
    hi = psum.astype(BF16)
    lo = (psum - hi.astype(F32)).astype(BF16)
    ovlt = ovlt_ref[...]
    imp = _dot_nt(ovlt, hi) + _dot_nt(ovlt, lo)
    s_idx = lax.broadcasted_iota(jnp.int32, (n_sel, tq), 0)
    t_blk = (lax.broadcasted_iota(jnp.int32, (n_sel, tq), 1) + i * tq) // SEL_LEN
    d_blk = t_blk - s_idx
    local = (d_blk >= 0) & (d_blk < N_LOCAL_BLOCKS)
    v = jnp.where(local, FORCE_LOCAL,
                  jnp.where(s_idx == 0, FORCE_INIT, jnp.where(d_blk >= 0, imp, -1.0)))
    rows = s_idx.astype(F32)
    sel = jnp.zeros((n_sel, tq), F32)
    for _ in range(min(SEL_TOPK, n_sel)):
        m = jnp.max(v, axis=0, keepdims=True)
        first = jnp.min(jnp.where(v == m, rows, float(n_sel)), axis=0, keepdims=True)
        pick = rows == first
        sel = jnp.where(pick, 1.0, sel)
        v = jnp.where(pick, REMOVED, v)
    sel_ref[0] = _dot_nt(eye_ref[...], sel.astype(BF16)).astype(BF16)


def _cmp_attn(q4, cmp4, addc, ovlt, batch, seq):
    tq = min(CMP_TILE, seq)
    nt = seq // tq
    n_rows = cmp4.shape[2]
    n_sel = ovlt.shape[0]
    eye = jnp.eye(tq, dtype=BF16)
    return pl.pallas_call(
        _cmp_attn_kernel,
        grid=(batch * N_KV_HEADS, nt),
        in_specs=[
            pl.BlockSpec((GROUP, tq, 64), lambda g, i: (g % 2, (g // 2) * nt + i, 0)),
            pl.BlockSpec((1, 1, n_rows, 64), lambda g, i: (g % 2, g // 2, 0, 0)),
            pl.BlockSpec((1, 1, n_rows, 64), lambda g, i: (2 + g % 2, g // 2, 0, 0)),
            pl.BlockSpec((GROUP, tq, n_rows), lambda g, i: (g % 2, i, 0)),
            pl.BlockSpec(ovlt.shape, lambda g, i: (0, 0)),
            pl.BlockSpec((tq, tq), lambda g, i: (0, 0)),
        ],
        out_specs=[
            pl.BlockSpec((tq, GROUP * 64), lambda g, i: ((g // 2) * nt + i, g % 2)),
            pl.BlockSpec((1, tq, n_sel), lambda g, i: (g, i, 0)),
        ],
        out_shape=[
            jax.ShapeDtypeStruct((batch * seq, N_HEADS * 64), F32),
            jax.ShapeDtypeStruct((batch * N_KV_HEADS, seq, n_sel), BF16),
        ],
        compiler_params=_params(("parallel", "arbitrary")),
        name="cmp_attn",
    )(q4, cmp4, cmp4, addc, ovlt, eye)


def _softmax_step(qs, k, v1, mask, add, carry):
    tq = qs.shape[0] // GROUP
    s = _dot_nt(qs, k).reshape(GROUP, tq, k.shape[0])
    if mask is not None:
        s = jnp.where(mask[None] > 0.5, s, NEG)
    if add is not None:
        s = s + add
    m_tile = jnp.max(s, axis=-1, keepdims=True)
    if carry is None:
        m_new = m_tile
    else:
        m, acc = carry
        m_new = jnp.maximum(m, m_tile)
    p = jnp.exp(s - m_new)
    pv = _dot(p.reshape(GROUP * tq, k.shape[0]).astype(BF16), v1).reshape(GROUP, tq, v1.shape[1])
    if carry is None:
        return m_new, pv
    return m_new, jnp.exp(m - m_new) * acc + pv


FAR_SLAB = 4 * ATT_TILE


def _sel_win_kernel(q_ref, ks_ref, vs_ref, kw_ref, vw_ref, sel_ref, enear_ref, eslab_ref,
                    winadd_ref, seladd_ref, oc_ref, gates_ref, o_ref):
    i = pl.program_id(1)
    tq = q_ref.shape[1]
    qs = q_ref[...].reshape(GROUP * tq, 64)
    sel = sel_ref[0]

    def rows(ref, start, size):
        return ref[0, 0, pl.ds(pl.multiple_of(start, ATT_TILE), size), :]

    def hide_padding(add, n_pad):
        col = lax.broadcasted_iota(jnp.int32, (1, 1, add.shape[2]), 2)
        return jnp.where(col < n_pad, NEG, add)

    w0 = i * tq
    _, acc_w = _softmax_step(qs, rows(kw_ref, w0, WIN_SPAN), rows(vw_ref, w0, WIN_SPAN), None,
                             hide_padding(winadd_ref[...], WINDOW - i * tq), None)

    n0 = w0 + WINDOW - tq
    carry = _softmax_step(qs, rows(ks_ref, n0, NEAR_SPAN), rows(vs_ref, n0, NEAR_SPAN),
                          _dot(sel, enear_ref[i]), hide_padding(seladd_ref[...], tq - i * tq), None)
    far_end = jnp.maximum(i - 1, 0) * tq
    col = lax.broadcasted_iota(jnp.int32, (1, FAR_SLAB), 1)

    def far_slab(j, c):
        start = WINDOW + j * FAR_SLAB
        mask = jnp.where(col < far_end - j * FAR_SLAB, _dot(sel, eslab_ref[j]), 0.0)
        return _softmax_step(qs, rows(ks_ref, start, FAR_SLAB), rows(vs_ref, start, FAR_SLAB), mask, None, c)

    _, acc_s = lax.fori_loop(0, (far_end + FAR_SLAB - 1) // FAR_SLAB, far_slab, carry)
    o_s = acc_s[..., :64] / acc_s[..., 64:65]
    o_w = acc_w[..., :64] / acc_w[..., 64:65]

    g = jax.nn.sigmoid(gates_ref[0])
    for hh in range(GROUP):
        o_c = oc_ref[:, 64 * hh:64 * (hh + 1)]
        o_ref[:, 64 * hh:64 * (hh + 1)] = (g[:, 3 * hh:3 * hh + 1] * o_c
                                            + g[:, 3 * hh + 1:3 * hh + 2] * o_s[hh]
                                            + g[:, 3 * hh + 2:3 * hh + 3] * o_w[hh])


def _sel_win_attn(q4, kk5, vv5, sel, enear, eslab, winadd, seladd, o_c, gates2, batch, seq):
    tq = ATT_TILE
    nt = seq // tq
    n_sel = sel.shape[2]
    padded = kk5.shape[2]

    def kv_spec(arr, base):
        return pl.BlockSpec((1, 1, padded, arr.shape[3]), lambda g, i: (base + g % 2, g // 2, 0, 0))

    whole = lambda arr: pl.BlockSpec(arr.shape, lambda g, i: (0,) * arr.ndim)
    return pl.pallas_call(
        _sel_win_kernel,
        grid=(batch * N_KV_HEADS, nt),
        in_specs=[
            pl.BlockSpec((GROUP, tq, 64), lambda g, i: (g % 2, (g // 2) * nt + i, 0)),
            kv_spec(kk5, 0), kv_spec(vv5, 0), kv_spec(kk5, 2), kv_spec(vv5, 2),
            pl.BlockSpec((1, tq, n_sel), lambda g, i: (g, i, 0)),
            whole(enear), whole(eslab),
            pl.BlockSpec((GROUP, tq, WIN_SPAN), lambda g, i: (g % 2, 0, 0)),
            pl.BlockSpec((GROUP, tq, NEAR_SPAN), lambda g, i: (g % 2, 0, 0)),
            pl.BlockSpec((tq, GROUP * 64), lambda g, i: ((g // 2) * nt + i, g % 2)),
            pl.BlockSpec((1, tq, LANES), lambda g, i: (g % 2, (g // 2) * nt + i, 0)),
        ],
        out_specs=pl.BlockSpec((tq, GROUP * 64), lambda g, i: ((g // 2) * nt + i, g % 2)),
        out_shape=jax.ShapeDtypeStruct((batch * seq, N_HEADS * 64), F32),
        compiler_params=_params(("parallel", "arbitrary")),
        name="sel_win_attn",
    )(q4, kk5, vv5, kk5, vv5, sel, enear, eslab, winadd, seladd, o_c, gates2)


def _rglru_kernel(xrg_ref, xg_ref, cw_ref, cb_ref, wa_ref, ba_ref, wx_ref, bx_ref, lam_ref, gn_ref,
                  o_ref, xs_ref, h_ref):
    step = pl.program_id(1)
    tc = xrg_ref.shape[0]

    @pl.when(step == 0)
    def _():
        xs_ref[0:8, :] = jnp.zeros((8, xs_ref.shape[1]), F32)
        h_ref[...] = jnp.zeros(h_ref.shape, F32)

    xs_ref[8:, :] = xrg_ref[...]
    xc = cb_ref[...] + cw_ref[CONV_WIDTH - 1:CONV_WIDTH, :] * xs_ref[8:, :]
    for k in range(1, CONV_WIDTH):
        xc = xc + cw_ref[CONV_WIDTH - 1 - k:CONV_WIDTH - k, :] * xs_ref[pl.ds(8 - k, tc), :]
    xs_ref[0:8, :] = xs_ref[tc:tc + 8, :]

    xcb = xc.astype(BF16)
    r = jax.nn.sigmoid(_dot(xcb, wa_ref[...]) + ba_ref[...])
    gi = jax.nn.sigmoid(_dot(xcb, wx_ref[...]) + bx_ref[...])
    lam = lam_ref[...]
    softplus_neg = jnp.maximum(-lam, 0.0) + jnp.log1p(jnp.exp(-jnp.abs(lam)))
    log_a = -LRU_C * r * softplus_neg
    a = jnp.exp(log_a)
    b = jnp.sqrt(1.0 - jnp.exp(2.0 * log_a)) * (gi * xc)

    row = lax.broadcasted_iota(jnp.int32, (tc, 1), 0)
    s = 1
    while s < tc:
        valid = row >= s
        b = jnp.where(valid, a * pltpu.roll(b, s, 0) + b, b)
        a = jnp.where(valid, a * pltpu.roll(a, s, 0), a)
        s *= 2
    h = b + a * h_ref[0:1, :]
    h_ref[0:1, :] = h[tc - 1:tc, :]
    o_ref[...] = _rms(h * jax.nn.gelu(xg_ref[...]), gn_ref[...]).astype(BF16)


def _rglru(xrg, cw, cb, wa, ba, wx, bx, lam, gn, batch, seq, tc):
    nt = seq // tc
    w = cw.shape[1]
    vec = pl.BlockSpec((1, w), lambda b, i: (0, 0))
    return pl.pallas_call(
        _rglru_kernel,
        grid=(batch, nt),
        in_specs=[
            pl.BlockSpec((tc, w), lambda b, i: (b * nt + i, 0)),
            pl.BlockSpec((tc, w), lambda b, i: (b * nt + i, 1)),
            pl.BlockSpec((CONV_WIDTH, w), lambda b, i: (0, 0)),
            vec,
            pl.BlockSpec((w, w), lambda b, i: (0, 0)), vec,
            pl.BlockSpec((w, w), lambda b, i: (0, 0)), vec,
            vec, vec,
        ],
        out_specs=pl.BlockSpec((tc, w), lambda b, i: (b * nt + i, 0)),
        out_shape=jax.ShapeDtypeStruct((batch * seq, w), BF16),
        scratch_shapes=[pltpu.VMEM((tc + 8, w), F32), pltpu.VMEM((8, w), F32)],
        compiler_params=_params(("parallel", "arbitrary")),
        name="rglru",
    )(xrg, xrg, cw, cb, wa, ba, wx, bx, lam, gn)


def _out_proj_kernel(a_ref, l_ref, x_ref, ga_ref, woa_ref, wol_ref, gf_ref, wq_ref, sk_ref,
                     h1_ref, xt_ref, st_ref):
    an = _rms(a_ref[...], ga_ref[...]).astype(BF16)
    h1 = x_ref[...] + _dot(an, woa_ref[...]) + _dot(l_ref[...], wol_ref[...])
    h1_ref[...] = h1
    xn = _rms(h1, gf_ref[...])
    xt_ref[...] = xn.T.astype(BF16)
    q = _dot(xn.astype(BF16), wq_ref[...])
    for hp in range(2 * PEER_HEADS):
        qs = q[:, LANES * hp:LANES * (hp + 1)].astype(BF16)
        st_ref[hp] = _dot_nt(sk_ref[hp % 2], qs)


def _out_proj(a_out, l_n, x2, ga, woa, wol, gf, wq, sk, tm):
    n, d = x2.shape
    full = lambda arr: pl.BlockSpec(arr.shape, lambda i: (0,) * arr.ndim)
    return pl.pallas_call(
        _out_proj_kernel,
        grid=(n // tm,),
        in_specs=[
            pl.BlockSpec((tm, a_out.shape[1]), lambda i: (i, 0)),
            pl.BlockSpec((tm, l_n.shape[1]), lambda i: (i, 0)),
            pl.BlockSpec((tm, d), lambda i: (i, 0)),
            full(ga), full(woa), full(wol), full(gf), full(wq), full(sk),
        ],
        out_specs=[
            pl.BlockSpec((tm, d), lambda i: (i, 0)),
            pl.BlockSpec((d, tm), lambda i: (0, i)),
            pl.BlockSpec((2 * PEER_HEADS, PEER_N_KEYS, tm), lambda i: (0, 0, i)),
        ],
        out_shape=[
            jax.ShapeDtypeStruct((n, d), F32),
            jax.ShapeDtypeStruct((d, n), BF16),
            jax.ShapeDtypeStruct((2 * PEER_HEADS, PEER_N_KEYS, n), F32),
        ],
        compiler_params=_params(("parallel",)),
        name="out_proj",
    )(a_out, l_n, x2, ga, woa, wol, gf, wq, sk)


N_TOP = PEER_TOPK
_PAIRS = [(i, j) for i in range(N_TOP) for j in range(N_TOP) if (i + 1) * (j + 1) <= N_TOP]


def _top_values(s):
    rows = lax.broadcasted_iota(jnp.int32, s.shape, 0).astype(F32)
    rank = jnp.full(s.shape, float(PEER_TOPK), F32)
    out = []
    for r in range(N_TOP):
        m = jnp.max(s, axis=0, keepdims=True)
        out.append(m)
        pick = rows == jnp.min(jnp.where(s == m, rows, float(s.shape[0])), axis=0, keepdims=True)
        s = jnp.where(pick, REMOVED, s)
        rank = jnp.where(pick, float(r), rank)
    return out, rank


def _peer_stats_kernel(st_ref, r2_ref, e2_ref, l_ref, e1_ref):
    tops = [[], []]
    rank1 = []
    for h in range(PEER_HEADS):
        vals, rank = _top_values(st_ref[2 * h])
        tops[0].append(vals)
        rank1.append(rank)
        vals, rank = _top_values(st_ref[2 * h + 1])
        tops[1].append(vals)
        r2_ref[h] = rank.astype(BF16)
    a = [jnp.concatenate([tops[0][h][i] for h in range(PEER_HEADS)], axis=0) for i in range(N_TOP)]
    b = [jnp.concatenate([tops[1][h][j] for h in range(PEER_HEADS)], axis=0) for j in range(N_TOP)]
    cand = [a[i] + b[j] for (i, j) in _PAIRS]
    work = list(cand)
    kth = []
    for _ in range(N_TOP):
        m = functools.reduce(jnp.maximum, work)
        kth.append(m)
        found = jnp.zeros(m.shape, jnp.bool_)
        for k in range(len(work)):
            hit = (work[k] == m) & jnp.logical_not(found)
            work[k] = jnp.where(hit, REMOVED, work[k])
            found = found | hit
    thr = kth[PEER_TOPK - 1]
    top = a[0] + b[0]
    z = functools.reduce(lambda u, w: u + w, [jnp.where(c >= thr, jnp.exp(c - top), 0.0) for c in cand])
    n_sel = [functools.reduce(lambda u, w: u + w,
                              [(a[r] + b[j] >= thr).astype(F32) for j in range(PEER_TOPK)])
             for r in range(PEER_TOPK)]
    for h in range(PEER_HEADS):
        row = slice(h, h + 1)
        l_ref[h] = functools.reduce(
            lambda u, w: u + w, [jnp.where(rank1[h] == float(r), n_sel[r][row], 0.0) for r in range(PEER_TOPK)])
        e1_ref[h] = jnp.exp(st_ref[2 * h] - a[0][row])
        e2_ref[h] = (jnp.exp(st_ref[2 * h + 1] - b[0][row]) / z[row]).astype(BF16)


def _peer_stats(st, tl):
    hp, nk, n = st.shape
    out_spec = pl.BlockSpec((PEER_HEADS, nk, tl), lambda i: (0, 0, i))
    return pl.pallas_call(
        _peer_stats_kernel,
        grid=(n // tl,),
        in_specs=[pl.BlockSpec((hp, nk, tl), lambda i: (0, 0, i))],
        out_specs=[out_spec] * 4,
        out_shape=[
            jax.ShapeDtypeStruct((PEER_HEADS, nk, n), BF16),
            jax.ShapeDtypeStruct((PEER_HEADS, nk, n), BF16),
            jax.ShapeDtypeStruct((PEER_HEADS, nk, n), F32),
            jax.ShapeDtypeStruct((PEER_HEADS, nk, n), F32),
        ],
        compiler_params=_params(("parallel",)),
        name="peer_stats",
    )(st)


def _peer_main_kernel(u_ref, vt_ref, xt_ref, r2_ref, e2_ref, l_ref, e1_ref, h1_ref, o_ref, acc_ref):
    c = pl.program_id(1)
    ce = u_ref.shape[0]
    tm = xt_ref.shape[1]
    nk = PEER_N_KEYS

    @pl.when(c == 0)
    def _():
        acc_ref[...] = jnp.zeros(acc_ref.shape, F32)

    act = _gelu_tanh(_dot(u_ref[...], xt_ref[...]))

    def rows_bf16(ref, h, i1):
        packed = jnp.broadcast_to(ref[h, pl.ds(i1, 1), :], (BF16_ROWS, tm)).astype(BF16)
        return jnp.concatenate([packed] * (nk // BF16_ROWS), axis=0)

    w_parts = []
    for ii in range(ce // nk):
        i1 = c * (ce // nk) + ii
        g = None
        for h in range(PEER_HEADS):
            e2 = e2_ref[h]
            term = jnp.where(r2_ref[h] < rows_bf16(l_ref, h, i1), e2, jnp.zeros_like(e2)) * rows_bf16(e1_ref, h, i1)
            g = term if g is None else g + term
        w_parts.append(act[ii * nk:(ii + 1) * nk, :].astype(BF16) * g)
    w = jnp.concatenate(w_parts, axis=0)
    acc_ref[...] += _dot(vt_ref[...], w)

    @pl.when(c == pl.num_programs(1) - 1)
    def _():
        o_ref[...] = h1_ref[...] + acc_ref[...].T


def _peer_main(u_b, vt_b, xt, r2, e2, n_sel, e1, h1, tm, ce):
    ne, d = u_b.shape
    n = xt.shape[1]
    per_tok = pl.BlockSpec((PEER_HEADS, PEER_N_KEYS, tm), lambda t, c: (0, 0, t))
    return pl.pallas_call(
        _peer_main_kernel,
        grid=(n // tm, ne // ce),
        in_specs=[
            pl.BlockSpec((ce, d), lambda t, c: (c, 0)),
            pl.BlockSpec((d, ce), lambda t, c: (0, c)),
            pl.BlockSpec((d, tm), lambda t, c: (0, t)),
            per_tok, per_tok, per_tok, per_tok,
            pl.BlockSpec((tm, d), lambda t, c: (t, 0)),
        ],
        out_specs=pl.BlockSpec((tm, d), lambda t, c: (t, 0)),
        out_shape=jax.ShapeDtypeStruct((n, d), F32),
        scratch_shapes=[pltpu.VMEM((d, tm), F32)],
        compiler_params=_params(("parallel", "arbitrary")),
        name="peer_main",
    )(u_b, vt_b, xt, r2, e2, n_sel, e1, h1)


def _ple_final_kernel(h_ref, p_ref, gp_ref, wg_ref, bg_ref, wp_ref, gfin_ref, o_ref):
    h = h_ref[...]
    gate = jax.nn.sigmoid(_dot(_rms(h, gp_ref[...]).astype(BF16), wg_ref[...]) + bg_ref[...])
    h = h + gate * _dot(p_ref[...].astype(BF16), wp_ref[...])
    o_ref[...] = _rms(h, gfin_ref[...])


def _ple_final(h2, p2, gp, wg, bg, wp, gfin, tm):
    n, d = h2.shape
    full = lambda arr: pl.BlockSpec(arr.shape, lambda i: (0,) * arr.ndim)
    return pl.pallas_call(
        _ple_final_kernel,
        grid=(n // tm,),
        in_specs=[
            pl.BlockSpec((tm, d), lambda i: (i, 0)),
            pl.BlockSpec((tm, p2.shape[1]), lambda i: (i, 0)),
            full(gp), full(wg), full(bg), full(wp), full(gfin),
        ],
        out_specs=pl.BlockSpec((tm, d), lambda i: (i, 0)),
        out_shape=jax.ShapeDtypeStruct((n, d), F32),
        compiler_params=_params(("parallel",)),
        name="ple_final",
    )(h2, p2, gp, wg, bg, wp, gfin)


def _overlap_t(seq, n_rows):
    n_cmp = n_rows - 1
    cmp_start = np.arange(n_cmp) * CMP_STRIDE
    sel_start = np.arange(seq // SEL_LEN) * SEL_LEN
    ov = np.clip(np.minimum(cmp_start[:, None] + CMP_LEN, sel_start[None, :] + SEL_LEN)
                 - np.maximum(cmp_start[:, None], sel_start[None, :]), 0, None).astype(np.float32) / CMP_LEN
    out = np.zeros((seq // SEL_LEN, n_rows), np.float32)
    out[:, :n_cmp] = ov.T
    return out


def _block_expand(seq, starts, width):
    pos = np.asarray(starts).reshape(-1, 1, 1) + np.arange(width).reshape(1, 1, width)
    blk = np.where(pos >= 0, pos // SEL_LEN, -1)
    return (blk == np.arange(seq // SEL_LEN).reshape(1, -1, 1)).astype(np.float32)


def _block_diag(w):
    nb, bd, _ = w.shape
    out = jnp.zeros((nb * bd, nb * bd), w.dtype)
    for j in range(nb):
        out = out.at[j * bd:(j + 1) * bd, j * bd:(j + 1) * bd].set(w[j])
    return out


def _layer(h, p_i, attn_norm, w_in, cmp_k_pe, cmp_k_w1, cmp_k_w2, cmp_v_pe, cmp_v_w1, cmp_v_w2,
           rel_table, conv_w, conv_b, lru_wa, lru_ba, lru_wx, lru_bx, lru_lambda,
           grp_norm_attn, grp_norm_lru, w_out, ffn_norm, peer_wq, peer_subkeys, peer_u, peer_v,
           ple_norm, ple_wgate, ple_bgate, ple_proj, *, tm, tc, tl, ce):
    batch, seq, d = h.shape
    n = batch * seq
    x2 = h.reshape(n, d)
    aw = N_HEADS * HEAD_DIM
    kvw = N_KV_HEADS * HEAD_DIM
    lw = d - aw

    o = 0
    cols = {}
    for name, width in (("q", aw), ("kc", kvw), ("vc", kvw), ("ks", kvw), ("vs", kvw), ("kw", kvw),
                        ("vw", kvw), ("gates", N_HEADS * 3), ("xr", lw), ("xg", lw)):
        cols[name] = w_in[:, o:o + width]
        o += width
    gate_pad = [jnp.pad(cols["gates"][:, N_GATE_COLS * k:N_GATE_COLS * (k + 1)], ((0, 0), (0, LANES - N_GATE_COLS)))
                for k in range(N_KV_HEADS)]
    wf = jnp.concatenate([cols["xr"], cols["xg"], cols["kc"], cols["vc"]] + gate_pad, axis=1).astype(BF16)
    wb = jnp.concatenate([cols["q"], cols["ks"], cols["kw"], cols["vs"], cols["vw"]], axis=1).astype(BF16)

    xrg, kvc, gates2, q4, kk, vv = _in_proj(x2, attn_norm.reshape(1, d), wf, wb, tm)

    n_rows = seq // CMP_STRIDE
    r4 = kvc.reshape(4, batch, n_rows, CMP_STRIDE * HEAD_DIM)
    pe2 = jnp.stack([cmp_k_pe.reshape(1, -1), cmp_v_pe.reshape(1, -1)])
    w1s = jnp.stack([cmp_k_w1, cmp_v_w1]).astype(BF16)
    w2s = jnp.stack([cmp_k_w2, cmp_v_w2]).astype(BF16)
    cmp4 = _compress(r4, pe2, w1s, w2s)

    winadd, seladd = _bias_near(rel_table)
    addc = _bias_cmp(rel_table, seq, n_rows, min(seq, 512))
    ovlt = jnp.asarray(_overlap_t(seq, n_rows), BF16)
    o_c, sel = _cmp_attn(q4, cmp4, addc, ovlt, batch, seq)

    nt = seq // ATT_TILE
    tile_starts = np.arange(nt) * ATT_TILE
    enear = jnp.asarray(_block_expand(seq, tile_starts - ATT_TILE, NEAR_SPAN), BF16)
    eslab = jnp.asarray(_block_expand(seq, np.arange(max(seq // FAR_SLAB, 1)) * FAR_SLAB, FAR_SLAB), BF16)
    front = ((0, 0), (0, 0), (WINDOW, 0), (0, 0))
    kk5 = jnp.pad(kk.reshape(4, batch, seq, HEAD_DIM), front)
    vv5 = jnp.pad(vv.reshape(4, batch, seq, LANES), front)
    a_out = _sel_win_attn(q4, kk5, vv5, sel, enear, eslab, winadd, seladd, o_c, gates2, batch, seq)

    l_n = _rglru(xrg, conv_w.reshape(CONV_WIDTH, lw), conv_b.reshape(1, lw),
                 _block_diag(lru_wa).astype(BF16), lru_ba.reshape(1, lw),
                 _block_diag(lru_wx).astype(BF16), lru_bx.reshape(1, lw),
                 lru_lambda.reshape(1, lw), grp_norm_lru.reshape(1, lw), batch, seq, tc)

    w_out_b = w_out.astype(BF16)
    h1, xt, st = _out_proj(a_out, l_n, x2, grp_norm_attn.reshape(1, aw), w_out_b[:aw], w_out_b[aw:],
                           ffn_norm.reshape(1, d), peer_wq.astype(BF16), peer_subkeys.astype(BF16), tm)
    r2, e2, n_sel, e1 = _peer_stats(st, tl)
    h2 = _peer_main(peer_u.astype(BF16), peer_v.T.astype(BF16), xt, r2, e2, n_sel, e1, h1, tm, ce)
    ple_args = (h2, p_i.reshape(n, -1), ple_norm.reshape(1, d), ple_wgate.astype(BF16), ple_bgate.reshape(1, d),
                ple_proj.astype(BF16))
    return ple_args, dict(a_out=a_out, l_n=l_n, h1=h1)


def kernel(x, p, attn_norm, w_in, cmp_k_pe, cmp_k_w1, cmp_k_w2, cmp_v_pe, cmp_v_w1, cmp_v_w2, rel_table, conv_w, conv_b, lru_wa, lru_ba, lru_wx, lru_bx, lru_lambda, grp_norm_attn, grp_norm_lru, w_out, ffn_norm, peer_wq, peer_subkeys, peer_u, peer_v, ple_norm, ple_wgate, ple_bgate, ple_proj, final_norm):
    batch, seq, d = x.shape
    assert w_in.shape[0] == 1, "single-layer stack only"
    tm = min(512, seq)
    (h2, p2, gp, wg, bg, wp), _ = _layer(
        x, p[0], attn_norm[0], w_in[0], cmp_k_pe[0], cmp_k_w1[0], cmp_k_w2[0], cmp_v_pe[0], cmp_v_w1[0],
        cmp_v_w2[0], rel_table, conv_w[0], conv_b[0], lru_wa[0], lru_ba[0], lru_wx[0], lru_bx[0],
        lru_lambda[0], grp_norm_attn[0], grp_norm_lru[0], w_out[0], ffn_norm[0], peer_wq[0],
        peer_subkeys[0], peer_u[0], peer_v[0], ple_norm[0], ple_wgate[0], ple_bgate[0], ple_proj[0],
        tm=tm, tc=min(256, seq), tl=min(256, seq), ce=512)
    out = _ple_final(h2, p2, gp, wg, bg, wp, final_norm.reshape(1, d), tm)
    return out.reshape(batch, seq, d)
```

```python
import functools
import math

import numpy as np
import jax
import jax.numpy as jnp
from jax import lax
from jax.experimental import pallas as pl
from jax.experimental.pallas import tpu as pltpu

F32 = jnp.float32
BF16 = jnp.bfloat16

N_HEADS = 8
HEAD_DIM = 64
N_KV_HEADS = 2
GROUP = N_HEADS // N_KV_HEADS
LRU_C = 8.0
CONV_WIDTH = 4
CMP_LEN = 32
CMP_STRIDE = 16
SEL_LEN = 64
SEL_TOPK = 16
N_LOCAL_BLOCKS = 2
WINDOW = 512
FORCE_LOCAL = 2.0e4
FORCE_INIT = 1.0e4
REL_BUCKETS = 32
REL_MAX_DIST = 128
PEER_HEADS = 8
PEER_N_KEYS = 128
PEER_TOPK = 16
EPS = 1e-6
NEG = -1e30
REMOVED = -3.0e38
TAKEN = -2.0 ** 120
TAKEN_STEPS = 32.0

LANES = 128
SUBLANES = 8
BF16_ROWS = 16
VMEM_LIMIT = 56 * 1024 * 1024

ATT_TILE = 128
CMP_TILE = 256
N_GATE_COLS = GROUP * 3


def _dot(a, b):
    return jnp.dot(a, b, preferred_element_type=F32)


def _dot_nt(a, b):
    return lax.dot_general(a, b, (((1,), (1,)), ((), ())), preferred_element_type=F32)


def _gelu_tanh(x):
    c = math.sqrt(2.0 / math.pi)
    half = 0.5 * x
    return half + half * jnp.tanh(x * (c + (c * 0.044715) * (x * x)))


def _rms(x, g):
    return x * lax.rsqrt(jnp.mean(x * x, axis=-1, keepdims=True) + EPS) * g


def _params(sem, vmem=VMEM_LIMIT, flags=None):
    return pltpu.CompilerParams(dimension_semantics=sem, vmem_limit_bytes=vmem, flags=flags)


def _in_proj_kernel(x_ref, g_ref, wf_ref, wb_ref, xrg_ref, kvc_ref, gates_ref, q4_ref, kk_ref, vv_ref):
    xb = _rms(x_ref[...], g_ref[...]).astype(BF16)
    yf = _dot(xb, wf_ref[...])
    xrg_ref[...] = yf[:, :1024]
    for j in range(4):
        kvc_ref[j] = yf[:, 1024 + 64 * j:1024 + 64 * (j + 1)]
    gates_ref[0] = yf[:, 1280:1408]
    gates_ref[1] = yf[:, 1408:1536]
    yb = _dot(xb, wb_ref[...])
    for h in range(N_HEADS):
        q4_ref[h] = (yb[:, 64 * h:64 * (h + 1)] * (HEAD_DIM ** -0.5)).astype(BF16)
    ones_col = (lax.broadcasted_iota(jnp.int32, (yb.shape[0], 64), 1) == 0).astype(F32)
    for j in range(4):
        kk_ref[j] = yb[:, 512 + 64 * j:512 + 64 * (j + 1)].astype(BF16)
        vv_ref[j] = jnp.concatenate([yb[:, 768 + 64 * j:768 + 64 * (j + 1)], ones_col], axis=1).astype(BF16)


def _in_proj(x2, g, wf, wb, tm):
    n, d = x2.shape
    return pl.pallas_call(
        _in_proj_kernel,
        grid=(n // tm,),
        in_specs=[
            pl.BlockSpec((tm, d), lambda i: (i, 0)),
            pl.BlockSpec((1, d), lambda i: (0, 0)),
            pl.BlockSpec(wf.shape, lambda i: (0, 0)),
            pl.BlockSpec(wb.shape, lambda i: (0, 0)),
        ],
        out_specs=[
            pl.BlockSpec((tm, 1024), lambda i: (i, 0)),
            pl.BlockSpec((4, tm, 64), lambda i: (0, i, 0)),
            pl.BlockSpec((2, tm, LANES), lambda i: (0, i, 0)),
            pl.BlockSpec((8, tm, 64), lambda i: (0, i, 0)),
            pl.BlockSpec((4, tm, 64), lambda i: (0, i, 0)),
            pl.BlockSpec((4, tm, LANES), lambda i: (0, i, 0)),
        ],
        out_shape=[
            jax.ShapeDtypeStruct((n, 1024), F32),
            jax.ShapeDtypeStruct((4, n, 64), F32),
            jax.ShapeDtypeStruct((2, n, LANES), F32),
            jax.ShapeDtypeStruct((8, n, 64), BF16),
            jax.ShapeDtypeStruct((4, n, 64), BF16),
            jax.ShapeDtypeStruct((4, n, LANES), BF16),
        ],
        compiler_params=_params(("parallel",)),
        name="in_proj",
    )(x2, g, wf, wb)


def _compress_kernel(r_ref, pe_ref, w1_ref, w2_ref, o_ref):
    r = r_ref[0, 0]
    n_rows = r.shape[0]
    half = r.shape[1]
    pe = pe_ref[0]
    top = _dot((r + pe[:, :half]).astype(BF16), w1_ref[0, :half, :])
    bot = _dot((r + pe[:, half:]).astype(BF16), w1_ref[0, half:, :])
    hid = top + pltpu.roll(bot, n_rows - 1, 0)
    act = jax.nn.gelu(hid)
    out = _dot(act.astype(BF16), w2_ref[0])
    row = lax.broadcasted_iota(jnp.int32, out.shape, 0)
    o_ref[0, 0] = jnp.where(row < n_rows - 1, out, 0.0).astype(BF16)


def _compress(r4, pe2, w1s, w2s):
    four, b, n_rows, width = r4.shape
    return pl.pallas_call(
        _compress_kernel,
        grid=(four, b),
        in_specs=[
            pl.BlockSpec((1, 1, n_rows, width), lambda j, bb: (j, bb, 0, 0)),
            pl.BlockSpec((1, 1, 2 * width), lambda j, bb: (j // 2, 0, 0)),
            pl.BlockSpec((1, 2 * width, w1s.shape[2]), lambda j, bb: (j // 2, 0, 0)),
            pl.BlockSpec((1, w2s.shape[1], 64), lambda j, bb: (j // 2, 0, 0)),
        ],
        out_specs=pl.BlockSpec((1, 1, n_rows, 64), lambda j, bb: (j, bb, 0, 0)),
        out_shape=jax.ShapeDtypeStruct((four, b, n_rows, 64), BF16),
        compiler_params=_params(("parallel", "parallel")),
        name="compress",
    )(r4, pe2, w1s, w2s)


def _bucket_breakpoints():
    n = np.arange(0, 4 * REL_MAX_DIST)
    max_exact = REL_BUCKETS // 2
    nf = np.maximum(n, max_exact).astype(np.float32)
    large = max_exact + (np.log(nf / max_exact) / np.float32(math.log(REL_MAX_DIST / max_exact))
                         * (REL_BUCKETS - max_exact)).astype(np.int32)
    large = np.minimum(large, REL_BUCKETS - 1)
    bucket = np.where(n < max_exact, n, large)
    return [int(np.argmax(bucket >= j)) for j in range(1, REL_BUCKETS)]


_BREAKS = _bucket_breakpoints()


def _bias_of_dist(dist, tab_ref, h):
    v = jnp.full(dist.shape, tab_ref[0, h], F32)
    for j, bp in enumerate(_BREAKS, start=1):
        v = jnp.where(dist >= bp, tab_ref[j, h], v)
    v = v - tab_ref[REL_BUCKETS - 1, h]
    return jnp.where(dist >= 0, v, NEG)


WIN_SPAN = WINDOW + ATT_TILE
NEAR_SPAN = 2 * ATT_TILE


def _bias_near_kernel(tab_ref, win_ref, sel_ref):
    h = pl.program_id(0)
    t = ATT_TILE
    for ref, span in ((win_ref, WIN_SPAN), (sel_ref, NEAR_SPAN)):
        r = lax.broadcasted_iota(jnp.int32, (t, span), 0)
        c = lax.broadcasted_iota(jnp.int32, (t, span), 1)
        dist = r - c + (span - t)
        ref[0] = jnp.where(dist < WINDOW, _bias_of_dist(dist, tab_ref, h), NEG)


def _bias_near(rel_table):
    t = ATT_TILE
    return pl.pallas_call(
        _bias_near_kernel,
        grid=(N_HEADS,),
        in_specs=[pl.BlockSpec(memory_space=pltpu.SMEM)],
        out_specs=[pl.BlockSpec((1, t, WIN_SPAN), lambda h: (h, 0, 0)),
                   pl.BlockSpec((1, t, NEAR_SPAN), lambda h: (h, 0, 0))],
        out_shape=[jax.ShapeDtypeStruct((N_HEADS, t, WIN_SPAN), F32),
                   jax.ShapeDtypeStruct((N_HEADS, t, NEAR_SPAN), F32)],
        compiler_params=_params(("arbitrary",)),
        name="bias_near",
    )(rel_table)


def _bias_cmp_kernel(tab_ref, o_ref, *, n_cmp):
    h = pl.program_id(0)
    i = pl.program_id(1)
    tq, nc = o_ref.shape[1], o_ref.shape[2]
    t = lax.broadcasted_iota(jnp.int32, (tq, nc), 0) + i * tq
    c = lax.broadcasted_iota(jnp.int32, (tq, nc), 1)
    dist = t - (c * CMP_STRIDE + CMP_LEN - 1)
    o_ref[0] = jnp.where(c < n_cmp, _bias_of_dist(dist, tab_ref, h), NEG)


def _bias_cmp(rel_table, seq, n_rows, tq):
    return pl.pallas_call(
        functools.partial(_bias_cmp_kernel, n_cmp=n_rows - 1),
        grid=(N_HEADS, seq // tq),
        in_specs=[pl.BlockSpec(memory_space=pltpu.SMEM)],
        out_specs=pl.BlockSpec((1, tq, n_rows), lambda h, i: (h, i, 0)),
        out_shape=jax.ShapeDtypeStruct((N_HEADS, seq, n_rows), F32),
        compiler_params=_params(("arbitrary", "arbitrary")),
        name="bias_cmp",
    )(rel_table)


def _cmp_attn_kernel(q_ref, kc_ref, vc_ref, addc_ref, ovlt_ref, eye_ref, oc_ref, sel_ref):
    i = pl.program_id(1)
    tq = q_ref.shape[1]
    n_sel = ovlt_ref.shape[0]
    kc = kc_ref[0, 0]
    vc = vc_ref[0, 0]
    t_col = lax.broadcasted_iota(jnp.int32, (tq, 1), 0) + i * tq
    row_ok = (t_col >= CMP_LEN - 1).astype(F32)
    psum = None
    for hh in range(GROUP):
        logit = _dot_nt(q_ref[hh], kc) + addc_ref[hh]
        m = jnp.max(logit, axis=-1, keepdims=True)
        e = jnp.exp(logit - m)
        p = e / jnp.sum(e, axis=-1, keepdims=True)
        oc_ref[:, 64 * hh:64 * (hh + 1)] = _dot(p.astype(BF16), vc) * row_ok
        psum = p if psum is None else psum + p
    psum = psum * row_ok
    hi = psum.astype(BF16)
    lo = (psum - hi.astype(F32)).astype(BF16)
    ovlt = ovlt_ref[...]
    imp = _dot_nt(ovlt, hi) + _dot_nt(ovlt, lo)
    s_idx = lax.broadcasted_iota(jnp.int32, (n_sel, tq), 0)
    t_blk = (lax.broadcasted_iota(jnp.int32, (n_sel, tq), 1) + i * tq) // SEL_LEN
    d_blk = t_blk - s_idx
    local = (d_blk >= 0) & (d_blk < N_LOCAL_BLOCKS)
    v = jnp.where(local, FORCE_LOCAL,
                  jnp.where(s_idx == 0, FORCE_INIT, jnp.where(d_blk >= 0, imp, -1.0)))
    rows = s_idx.astype(F32)
    sel = jnp.zeros((n_sel, tq), F32)
    for _ in range(min(SEL_TOPK, n_sel)):
        m = jnp.max(v, axis=0, keepdims=True)
        first = jnp.min(jnp.where(v == m, rows, float(n_sel)), axis=0, keepdims=True)
        pick = rows == first
        sel = jnp.where(pick, 1.0, sel)
        v = jnp.where(pick, REMOVED, v)
    sel_ref[0] = _dot_nt(eye_ref[...], sel.astype(BF16)).astype(BF16)


def _cmp_attn(q4, cmp4, addc, ovlt, batch, seq):
    tq = min(CMP_TILE, seq)
    nt = seq // tq
    n_rows = cmp4.shape[2]
    n_sel = ovlt.shape[0]
    eye = jnp.eye(tq, dtype=BF16)
    return pl.pallas_call(
        _cmp_attn_kernel,
        grid=(batch * N_KV_HEADS, nt),
        in_specs=[
            pl.BlockSpec((GROUP, tq, 64), lambda g, i: (g % 2, (g // 2) * nt + i, 0)),
            pl.BlockSpec((1, 1, n_rows, 64), lambda g, i: (g % 2, g // 2, 0, 0)),
            pl.BlockSpec((1, 1, n_rows, 64), lambda g, i: (2 + g % 2, g // 2, 0, 0)),
            pl.BlockSpec((GROUP, tq, n_rows), lambda g, i: (g % 2, i, 0)),
            pl.BlockSpec(ovlt.shape, lambda g, i: (0, 0)),
            pl.BlockSpec((tq, tq), lambda g, i: (0, 0)),
        ],
        out_specs=[
            pl.BlockSpec((tq, GROUP * 64), lambda g, i: ((g // 2) * nt + i, g % 2)),
            pl.BlockSpec((1, tq, n_sel), lambda g, i: (g, i, 0)),
        ],
        out_shape=[
            jax.ShapeDtypeStruct((batch * seq, N_HEADS * 64), F32),
            jax.ShapeDtypeStruct((batch * N_KV_HEADS, seq, n_sel), BF16),
        ],
        compiler_params=_params(("parallel", "arbitrary")),
        name="cmp_attn",
    )(q4, cmp4, cmp4, addc, ovlt, eye)


def _softmax_step(qs, k, v1, mask, add, carry):
    tq = qs.shape[0] // GROUP
    s = _dot_nt(qs, k).reshape(GROUP, tq, k.shape[0])
    if mask is not None:
        s = jnp.where(mask[None] > 0.5, s, NEG)
    if add is not None:
        s = s + add
    m_tile = jnp.max(s, axis=-1, keepdims=True)
    if carry is None:
        m_new = m_tile
    else:
        m, acc = carry
        m_new = jnp.maximum(m, m_tile)
    p = jnp.exp(s - m_new)
    pv = _dot(p.reshape(GROUP * tq, k.shape[0]).astype(BF16), v1).reshape(GROUP, tq, v1.shape[1])
    if carry is None:
        return m_new, pv
    return m_new, jnp.exp(m - m_new) * acc + pv


FAR_SLAB = 4 * ATT_TILE


def _sel_win_kernel(q_ref, ks_ref, vs_ref, kw_ref, vw_ref, sel_ref, enear_ref, eslab_ref,
                    winadd_ref, seladd_ref, oc_ref, gates_ref, o_ref):
    i = pl.program_id(1)
    tq = q_ref.shape[1]
    qs = q_ref[...].reshape(GROUP * tq, 64)
    sel = sel_ref[0]

    def rows(ref, start, size):
        return ref[0, 0, pl.ds(pl.multiple_of(start, ATT_TILE), size), :]

    def hide_padding(add, n_pad):
        col = lax.broadcasted_iota(jnp.int32, (1, 1, add.shape[2]), 2)
        return jnp.where(col < n_pad, NEG, add)

    w0 = i * tq
    _, acc_w = _softmax_step(qs, rows(kw_ref, w0, WIN_SPAN), rows(vw_ref, w0, WIN_SPAN), None,
                             hide_padding(winadd_ref[...], WINDOW - i * tq), None)

    n0 = w0 + WINDOW - tq
    carry = _softmax_step(qs, rows(ks_ref, n0, NEAR_SPAN), rows(vs_ref, n0, NEAR_SPAN),
                          _dot(sel, enear_ref[i]), hide_padding(seladd_ref[...], tq - i * tq), None)
    far_end = jnp.maximum(i - 1, 0) * tq
    col = lax.broadcasted_iota(jnp.int32, (1, FAR_SLAB), 1)

    def far_slab(j, c):
        start = WINDOW + j * FAR_SLAB
        mask = jnp.where(col < far_end - j * FAR_SLAB, _dot(sel, eslab_ref[j]), 0.0)
        return _softmax_step(qs, rows(ks_ref, start, FAR_SLAB), rows(vs_ref, start, FAR_SLAB), mask, None, c)

    _, acc_s = lax.fori_loop(0, (far_end + FAR_SLAB - 1) // FAR_SLAB, far_slab, carry)
    o_s = acc_s[..., :64] / acc_s[..., 64:65]
    o_w = acc_w[..., :64] / acc_w[..., 64:65]

    g = jax.nn.sigmoid(gates_ref[0])
    for hh in range(GROUP):
        o_c = oc_ref[:, 64 * hh:64 * (hh + 1)]
        o_ref[:, 64 * hh:64 * (hh + 1)] = (g[:, 3 * hh:3 * hh + 1] * o_c
                                            + g[:, 3 * hh + 1:3 * hh + 2] * o_s[hh]
                                            + g[:, 3 * hh + 2:3 * hh + 3] * o_w[hh])


def _sel_win_attn(q4, kk5, vv5, sel, enear, eslab, winadd, seladd, o_c, gates2, batch, seq):
    tq = ATT_TILE
    nt = seq // tq
    n_sel = sel.shape[2]
    padded = kk5.shape[2]

    def kv_spec(arr, base):
        return pl.BlockSpec((1, 1, padded, arr.shape[3]), lambda g, i: (base + g % 2, g // 2, 0, 0))

    whole = lambda arr: pl.BlockSpec(arr.shape, lambda g, i: (0,) * arr.ndim)
    return pl.pallas_call(
        _sel_win_kernel,
        grid=(batch * N_KV_HEADS, nt),
        in_specs=[
            pl.BlockSpec((GROUP, tq, 64), lambda g, i: (g % 2, (g // 2) * nt + i, 0)),
            kv_spec(kk5, 0), kv_spec(vv5, 0), kv_spec(kk5, 2), kv_spec(vv5, 2),
            pl.BlockSpec((1, tq, n_sel), lambda g, i: (g, i, 0)),
            whole(enear), whole(eslab),
            pl.BlockSpec((GROUP, tq, WIN_SPAN), lambda g, i: (g % 2, 0, 0)),
            pl.BlockSpec((GROUP, tq, NEAR_SPAN), lambda g, i: (g % 2, 0, 0)),
            pl.BlockSpec((tq, GROUP * 64), lambda g, i: ((g // 2) * nt + i, g % 2)),
            pl.BlockSpec((1, tq, LANES), lambda g, i: (g % 2, (g // 2) * nt + i, 0)),
        ],
        out_specs=pl.BlockSpec((tq, GROUP * 64), lambda g, i: ((g // 2) * nt + i, g % 2)),
        out_shape=jax.ShapeDtypeStruct((batch * seq, N_HEADS * 64), F32),
        compiler_params=_params(("parallel", "arbitrary")),
        name="sel_win_attn",
    )(q4, kk5, vv5, kk5, vv5, sel, enear, eslab, winadd, seladd, o_c, gates2)


def _rglru_kernel(xrg_ref, xg_ref, cw_ref, cb_ref, wa_ref, ba_ref, wx_ref, bx_ref, lam_ref, gn_ref,
                  o_ref, xs_ref, h_ref):
    step = pl.program_id(1)
    tc = xrg_ref.shape[0]

    @pl.when(step == 0)
    def _():
        xs_ref[0:8, :] = jnp.zeros((8, xs_ref.shape[1]), F32)
        h_ref[...] = jnp.zeros(h_ref.shape, F32)

    xs_ref[8:, :] = xrg_ref[...]
    xc = cb_ref[...] + cw_ref[CONV_WIDTH - 1:CONV_WIDTH, :] * xs_ref[8:, :]
    for k in range(1, CONV_WIDTH):
        xc = xc + cw_ref[CONV_WIDTH - 1 - k:CONV_WIDTH - k, :] * xs_ref[pl.ds(8 - k, tc), :]
    xs_ref[0:8, :] = xs_ref[tc:tc + 8, :]

    xcb = xc.astype(BF16)
    r = jax.nn.sigmoid(_dot(xcb, wa_ref[...]) + ba_ref[...])
    gi = jax.nn.sigmoid(_dot(xcb, wx_ref[...]) + bx_ref[...])
    lam = lam_ref[...]
    softplus_neg = jnp.maximum(-lam, 0.0) + jnp.log1p(jnp.exp(-jnp.abs(lam)))
    log_a = -LRU_C * r * softplus_neg
    a = jnp.exp(log_a)
    b = jnp.sqrt(1.0 - jnp.exp(2.0 * log_a)) * (gi * xc)

    row = lax.broadcasted_iota(jnp.int32, (tc, 1), 0)
    s = 1
    while s < tc:
        valid = row >= s
        b = jnp.where(valid, a * pltpu.roll(b, s, 0) + b, b)
        a = jnp.where(valid, a * pltpu.roll(a, s, 0), a)
        s *= 2
    h = b + a * h_ref[0:1, :]
    h_ref[0:1, :] = h[tc - 1:tc, :]
    o_ref[...] = _rms(h * jax.nn.gelu(xg_ref[...]), gn_ref[...]).astype(BF16)


def _rglru(xrg, cw, cb, wa, ba, wx, bx, lam, gn, batch, seq, tc):
    nt = seq // tc
    w = cw.shape[1]
    vec = pl.BlockSpec((1, w), lambda b, i: (0, 0))
    return pl.pallas_call(
        _rglru_kernel,
        grid=(batch, nt),
        in_specs=[
            pl.BlockSpec((tc, w), lambda b, i: (b * nt + i, 0)),
            pl.BlockSpec((tc, w), lambda b, i: (b * nt + i, 1)),
            pl.BlockSpec((CONV_WIDTH, w), lambda b, i: (0, 0)),
            vec,
            pl.BlockSpec((w, w), lambda b, i: (0, 0)), vec,
            pl.BlockSpec((w, w), lambda b, i: (0, 0)), vec,
            vec, vec,
        ],
        out_specs=pl.BlockSpec((tc, w), lambda b, i: (b * nt + i, 0)),
        out_shape=jax.ShapeDtypeStruct((batch * seq, w), BF16),
        scratch_shapes=[pltpu.VMEM((tc + 8, w), F32), pltpu.VMEM((8, w), F32)],
        compiler_params=_params(("parallel", "arbitrary")),
        name="rglru",
    )(xrg, xrg, cw, cb, wa, ba, wx, bx, lam, gn)


def _out_proj_kernel(a_ref, l_ref, x_ref, ga_ref, woa_ref, wol_ref, gf_ref, wq_ref, sk_ref,
                     h1_ref, xt_ref, st_ref):
    an = _rms(a_ref[...], ga_ref[...]).astype(BF16)
    h1 = x_ref[...] + _dot(an, woa_ref[...]) + _dot(l_ref[...], wol_ref[...])
    h1_ref[...] = h1
    xn = _rms(h1, gf_ref[...])
    xt_ref[...] = xn.T.astype(BF16)
    q = _dot(xn.astype(BF16), wq_ref[...])
    for hp in range(2 * PEER_HEADS):
        qs = q[:, LANES * hp:LANES * (hp + 1)].astype(BF16)
        st_ref[hp] = _dot_nt(sk_ref[hp % 2], qs)


def _out_proj(a_out, l_n, x2, ga, woa, wol, gf, wq, sk, tm):
    n, d = x2.shape
    full = lambda arr: pl.BlockSpec(arr.shape, lambda i: (0,) * arr.ndim)
    return pl.pallas_call(
        _out_proj_kernel,
        grid=(n // tm,),
        in_specs=[
            pl.BlockSpec((tm, a_out.shape[1]), lambda i: (i, 0)),
            pl.BlockSpec((tm, l_n.shape[1]), lambda i: (i, 0)),
            pl.BlockSpec((tm, d), lambda i: (i, 0)),
            full(ga), full(woa), full(wol), full(gf), full(wq), full(sk),
        ],
        out_specs=[
            pl.BlockSpec((tm, d), lambda i: (i, 0)),
            pl.BlockSpec((d, tm), lambda i: (0, i)),
            pl.BlockSpec((2 * PEER_HEADS, PEER_N_KEYS, tm), lambda i: (0, 0, i)),
        ],
        out_shape=[
            jax.ShapeDtypeStruct((n, d), F32),
            jax.ShapeDtypeStruct((d, n), BF16),
            jax.ShapeDtypeStruct((2 * PEER_HEADS, PEER_N_KEYS, n), F32),
        ],
        compiler_params=_params(("parallel",)),
        name="out_proj",
    )(a_out, l_n, x2, ga, woa, wol, gf, wq, sk)


N_TOP = PEER_TOPK
_PAIRS = [(i, j) for i in range(N_TOP) for j in range(N_TOP) if (i + 1) * (j + 1) <= N_TOP]


def _top_values(s):
    n, tl = s.shape
    parts = [s[SUBLANES * k:SUBLANES * (k + 1)] for k in range(n // SUBLANES)]
    base = lax.broadcasted_iota(jnp.int32, (SUBLANES, tl), 0).astype(F32)
    rows = [base + float(SUBLANES * k) for k in range(len(parts))]
    out = []
    for r in range(N_TOP):
        m = jnp.max(functools.reduce(jnp.maximum, parts), axis=0, keepdims=True)
        out.append(m)
        first = jnp.full((SUBLANES, tl), float(n), F32)
        for k in reversed(range(len(parts))):
            first = jnp.where(parts[k] == m, rows[k], first)
        first = jnp.min(first, axis=0, keepdims=True)
        taken = TAKEN * (1.0 + r / TAKEN_STEPS)
        parts = [jnp.where(rows[k] == first, taken, parts[k]) for k in range(len(parts))]
    rank = jnp.concatenate(parts, axis=0)
    rank = jnp.where(rank <= TAKEN, (rank * (1.0 / TAKEN) - 1.0) * TAKEN_STEPS, float(PEER_TOPK))
    return out, rank


def _peer_stats_kernel(st_ref, r2_ref, e2_ref, l_ref, e1_ref):
    tops = [[], []]
    rank1 = []
    for h in range(PEER_HEADS):
        vals, rank = _top_values(st_ref[2 * h])
        tops[0].append(vals)
        rank1.append(rank)
        vals, rank = _top_values(st_ref[2 * h + 1])
        tops[1].append(vals)
        r2_ref[h] = rank.astype(BF16)
    a = [jnp.concatenate([tops[0][h][i] for h in range(PEER_HEADS)], axis=0) for i in range(N_TOP)]
    b = [jnp.concatenate([tops[1][h][j] for h in range(PEER_HEADS)], axis=0) for j in range(N_TOP)]
    cand = [a[i] + b[j] for (i, j) in _PAIRS]
    work = list(cand)
    kth = []
    for _ in range(N_TOP):
        m = functools.reduce(jnp.maximum, work)
        kth.append(m)
        found = jnp.zeros(m.shape, jnp.bool_)
        for k in range(len(work)):
            hit = (work[k] == m) & jnp.logical_not(found)
            work[k] = jnp.where(hit, REMOVED, work[k])
            found = found | hit
    thr = kth[PEER_TOPK - 1]
    top = a[0] + b[0]
    z = functools.reduce(lambda u, w: u + w, [jnp.where(c >= thr, jnp.exp(c - top), 0.0) for c in cand])
    n_sel = [functools.reduce(lambda u, w: u + w,
                              [(a[r] + b[j] >= thr).astype(F32) for j in range(PEER_TOPK)])
             for r in range(PEER_TOPK)]
    for h in range(PEER_HEADS):
        row = slice(h, h + 1)
        prefix = jnp.zeros(rank1[h].shape, F32)
        for r in range(PEER_TOPK):
            prefix = jnp.where(rank1[h] == float(r), n_sel[r][row], prefix)
        l_ref[h] = prefix
        e1_ref[h] = jnp.exp(st_ref[2 * h] - a[0][row])
        e2_ref[h] = (jnp.exp(st_ref[2 * h + 1] - b[0][row]) / z[row]).astype(BF16)


def _peer_stats(st, tl):
    hp, nk, n = st.shape
    out_spec = pl.BlockSpec((PEER_HEADS, nk, tl), lambda i: (0, 0, i))
    return pl.pallas_call(
        _peer_stats_kernel,
        grid=(n // tl,),
        in_specs=[pl.BlockSpec((hp, nk, tl), lambda i: (0, 0, i))],
        out_specs=[out_spec] * 4,
        out_shape=[
            jax.ShapeDtypeStruct((PEER_HEADS, nk, n), BF16),
            jax.ShapeDtypeStruct((PEER_HEADS, nk, n), BF16),
            jax.ShapeDtypeStruct((PEER_HEADS, nk, n), F32),
            jax.ShapeDtypeStruct((PEER_HEADS, nk, n), F32),
        ],
        compiler_params=_params(("parallel",)),
        name="peer_stats",
    )(st)


def _peer_main_kernel(u_ref, vt_ref, xt_ref, r2_ref, e2_ref, l_ref, e1_ref, h1_ref, o_ref, acc_ref):
    c = pl.program_id(1)
    ce = u_ref.shape[0]
    tm = xt_ref.shape[1]
    nk = PEER_N_KEYS

    @pl.when(c == 0)
    def _():
        acc_ref[...] = jnp.zeros(acc_ref.shape, F32)

    act = _gelu_tanh(_dot(u_ref[...], xt_ref[...]))

    def rows_bf16(ref, h, i1):
        packed = jnp.broadcast_to(ref[h, pl.ds(i1, 1), :], (BF16_ROWS, tm)).astype(BF16)
        return jnp.concatenate([packed] * (nk // BF16_ROWS), axis=0)

    w_parts = []
    for ii in range(ce // nk):
        i1 = c * (ce // nk) + ii
        g = None
        for h in range(PEER_HEADS):
            e2 = e2_ref[h]
            term = jnp.where(r2_ref[h] < rows_bf16(l_ref, h, i1), e2, jnp.zeros_like(e2)) * rows_bf16(e1_ref, h, i1)
            g = term if g is None else g + term
        w_parts.append(act[ii * nk:(ii + 1) * nk, :].astype(BF16) * g)
    w = jnp.concatenate(w_parts, axis=0)
    acc_ref[...] += _dot(vt_ref[...], w)

    @pl.when(c == pl.num_programs(1) - 1)
    def _():
        o_ref[...] = h1_ref[...] + acc_ref[...].T


def _peer_main(u_b, vt_b, xt, r2, e2, n_sel, e1, h1, tm, ce):
    ne, d = u_b.shape
    n = xt.shape[1]
    per_tok = pl.BlockSpec((PEER_HEADS, PEER_N_KEYS, tm), lambda t, c: (0, 0, t))
    return pl.pallas_call(
        _peer_main_kernel,
        grid=(n // tm, ne // ce),
        in_specs=[
            pl.BlockSpec((ce, d), lambda t, c: (c, 0)),
            pl.BlockSpec((d, ce), lambda t, c: (0, c)),
            pl.BlockSpec((d, tm), lambda t, c: (0, t)),
            per_tok, per_tok, per_tok, per_tok,
            pl.BlockSpec((tm, d), lambda t, c: (t, 0)),
        ],
        out_specs=pl.BlockSpec((tm, d), lambda t, c: (t, 0)),
        out_shape=jax.ShapeDtypeStruct((n, d), F32),
        scratch_shapes=[pltpu.VMEM((d, tm), F32)],
        compiler_params=_params(("parallel", "arbitrary")),
        name="peer_main",
    )(u_b, vt_b, xt, r2, e2, n_sel, e1, h1)


def _ple_final_kernel(h_ref, p_ref, gp_ref, wg_ref, bg_ref, wp_ref, gfin_ref, o_ref):
    h = h_ref[...]
    gate = jax.nn.sigmoid(_dot(_rms(h, gp_ref[...]).astype(BF16), wg_ref[...]) + bg_ref[...])
    h = h + gate * _dot(p_ref[...].astype(BF16), wp_ref[...])
    o_ref[...] = _rms(h, gfin_ref[...])


def _ple_final(h2, p2, gp, wg, bg, wp, gfin, tm):
    n, d = h2.shape
    full = lambda arr: pl.BlockSpec(arr.shape, lambda i: (0,) * arr.ndim)
    return pl.pallas_call(
        _ple_final_kernel,
        grid=(n // tm,),
        in_specs=[
            pl.BlockSpec((tm, d), lambda i: (i, 0)),
            pl.BlockSpec((tm, p2.shape[1]), lambda i: (i, 0)),
            full(gp), full(wg), full(bg), full(wp), full(gfin),
        ],
        out_specs=pl.BlockSpec((tm, d), lambda i: (i, 0)),
        out_shape=jax.ShapeDtypeStruct((n, d), F32),
        compiler_params=_params(("parallel",)),
        name="ple_final",
    )(h2, p2, gp, wg, bg, wp, gfin)


def _overlap_t(seq, n_rows):
    n_cmp = n_rows - 1
    cmp_start = np.arange(n_cmp) * CMP_STRIDE
    sel_start = np.arange(seq // SEL_LEN) * SEL_LEN
    ov = np.clip(np.minimum(cmp_start[:, None] + CMP_LEN, sel_start[None, :] + SEL_LEN)
                 - np.maximum(cmp_start[:, None], sel_start[None, :]), 0, None).astype(np.float32) / CMP_LEN
    out = np.zeros((seq // SEL_LEN, n_rows), np.float32)
    out[:, :n_cmp] = ov.T
    return out


def _block_expand(seq, starts, width):
    pos = np.asarray(starts).reshape(-1, 1, 1) + np.arange(width).reshape(1, 1, width)
    blk = np.where(pos >= 0, pos // SEL_LEN, -1)
    return (blk == np.arange(seq // SEL_LEN).reshape(1, -1, 1)).astype(np.float32)


def _block_diag(w):
    nb, bd, _ = w.shape
    out = jnp.zeros((nb * bd, nb * bd), w.dtype)
    for j in range(nb):
        out = out.at[j * bd:(j + 1) * bd, j * bd:(j + 1) * bd].set(w[j])
    return out


def _layer(h, p_i, attn_norm, w_in, cmp_k_pe, cmp_k_w1, cmp_k_w2, cmp_v_pe, cmp_v_w1, cmp_v_w2,
           rel_table, conv_w, conv_b, lru_wa, lru_ba, lru_wx, lru_bx, lru_lambda,
           grp_norm_attn, grp_norm_lru, w_out, ffn_norm, peer_wq, peer_subkeys, peer_u, peer_v,
           ple_norm, ple_wgate, ple_bgate, ple_proj, *, tm, tc, tl, ce):
    batch, seq, d = h.shape
    n = batch * seq
    x2 = h.reshape(n, d)
    aw = N_HEADS * HEAD_DIM
    kvw = N_KV_HEADS * HEAD_DIM
    lw = d - aw

    o = 0
    cols = {}
    for name, width in (("q", aw), ("kc", kvw), ("vc", kvw), ("ks", kvw), ("vs", kvw), ("kw", kvw),
                        ("vw", kvw), ("gates", N_HEADS * 3), ("xr", lw), ("xg", lw)):
        cols[name] = w_in[:, o:o + width]
        o += width
    gate_pad = [jnp.pad(cols["gates"][:, N_GATE_COLS * k:N_GATE_COLS * (k + 1)], ((0, 0), (0, LANES - N_GATE_COLS)))
                for k in range(N_KV_HEADS)]
    wf = jnp.concatenate([cols["xr"], cols["xg"], cols["kc"], cols["vc"]] + gate_pad, axis=1).astype(BF16)
    wb = jnp.concatenate([cols["q"], cols["ks"], cols["kw"], cols["vs"], cols["vw"]], axis=1).astype(BF16)

    xrg, kvc, gates2, q4, kk, vv = _in_proj(x2, attn_norm.reshape(1, d), wf, wb, tm)

    n_rows = seq // CMP_STRIDE
    r4 = kvc.reshape(4, batch, n_rows, CMP_STRIDE * HEAD_DIM)
    pe2 = jnp.stack([cmp_k_pe.reshape(1, -1), cmp_v_pe.reshape(1, -1)])
    w1s = jnp.stack([cmp_k_w1, cmp_v_w1]).astype(BF16)
    w2s = jnp.stack([cmp_k_w2, cmp_v_w2]).astype(BF16)
    cmp4 = _compress(r4, pe2, w1s, w2s)

    winadd, seladd = _bias_near(rel_table)
    addc = _bias_cmp(rel_table, seq, n_rows, min(seq, 512))
    ovlt = jnp.asarray(_overlap_t(seq, n_rows), BF16)
    o_c, sel = _cmp_attn(q4, cmp4, addc, ovlt, batch, seq)

    nt = seq // ATT_TILE
    tile_starts = np.arange(nt) * ATT_TILE
    enear = jnp.asarray(_block_expand(seq, tile_starts - ATT_TILE, NEAR_SPAN), BF16)
    eslab = jnp.asarray(_block_expand(seq, np.arange(max(seq // FAR_SLAB, 1)) * FAR_SLAB, FAR_SLAB), BF16)
    front = ((0, 0), (0, 0), (WINDOW, 0), (0, 0))
    kk5 = jnp.pad(kk.reshape(4, batch, seq, HEAD_DIM), front)
    vv5 = jnp.pad(vv.reshape(4, batch, seq, LANES), front)
    a_out = _sel_win_attn(q4, kk5, vv5, sel, enear, eslab, winadd, seladd, o_c, gates2, batch, seq)

    l_n = _rglru(xrg, conv_w.reshape(CONV_WIDTH, lw), conv_b.reshape(1, lw),
                 _block_diag(lru_wa).astype(BF16), lru_ba.reshape(1, lw),
                 _block_diag(lru_wx).astype(BF16), lru_bx.reshape(1, lw),
                 lru_lambda.reshape(1, lw), grp_norm_lru.reshape(1, lw), batch, seq, tc)

    w_out_b = w_out.astype(BF16)
    h1, xt, st = _out_proj(a_out, l_n, x2, grp_norm_attn.reshape(1, aw), w_out_b[:aw], w_out_b[aw:],
                           ffn_norm.reshape(1, d), peer_wq.astype(BF16), peer_subkeys.astype(BF16), tm)
    r2, e2, n_sel, e1 = _peer_stats(st, tl)
    h2 = _peer_main(peer_u.astype(BF16), peer_v.T.astype(BF16), xt, r2, e2, n_sel, e1, h1, tm, ce)
    ple_args = (h2, p_i.reshape(n, -1), ple_norm.reshape(1, d), ple_wgate.astype(BF16), ple_bgate.reshape(1, d),
                ple_proj.astype(BF16))
    return ple_args, dict(a_out=a_out, l_n=l_n, h1=h1)


def kernel(x, p, attn_norm, w_in, cmp_k_pe, cmp_k_w1, cmp_k_w2, cmp_v_pe, cmp_v_w1, cmp_v_w2, rel_table, conv_w, conv_b, lru_wa, lru_ba, lru_wx, lru_bx, lru_lambda, grp_norm_attn, grp_norm_lru, w_out, ffn_norm, peer_wq, peer_subkeys, peer_u, peer_v, ple_norm, ple_wgate, ple_bgate, ple_proj, final_norm):
    batch, seq, d = x.shape
    assert w_in.shape[0] == 1, "single-layer stack only"
    tm = min(512, seq)
    (h2, p2, gp, wg, bg, wp), _ = _layer(
        x, p[0], attn_norm[0], w_in[0], cmp_k_pe[0], cmp_k_w1[0], cmp_k_w2[0], cmp_v_pe[0], cmp_v_w1[0],
        cmp_v_w2[0], rel_table, conv_w[0], conv_b[0], lru_wa[0], lru_ba[0], lru_wx[0], lru_bx[0],
        lru_lambda[0], grp_norm_attn[0], grp_norm_lru[0], w_out[0], ffn_norm[0], peer_wq[0],
        peer_subkeys[0], peer_u[0], peer_v[0], ple_norm[0], ple_wgate[0], ple_bgate[0], ple_proj[0],
        tm=tm, tc=min(256, seq), tl=min(256, seq), ce=2048)
    out = _ple_final(h2, p2, gp, wg, bg, wp, final_norm.reshape(1, d), tm)
    return out.reshape(batch, seq, d)
```

```python
import functools
import math

import numpy as np
import jax
import jax.numpy as jnp
from jax import lax
from jax.experimental import pallas as pl
from jax.experimental.pallas import tpu as pltpu

F32 = jnp.float32
BF16 = jnp.bfloat16

N_HEADS = 8
HEAD_DIM = 64
N_KV_HEADS = 2
GROUP = N_HEADS // N_KV_HEADS
LRU_C = 8.0
CONV_WIDTH = 4
CMP_LEN = 32
CMP_STRIDE = 16
SEL_LEN = 64
SEL_TOPK = 16
N_LOCAL_BLOCKS = 2
WINDOW = 512
FORCE_LOCAL = 2.0e4
FORCE_INIT = 1.0e4
REL_BUCKETS = 32
REL_MAX_DIST = 128
PEER_HEADS = 8
PEER_N_KEYS = 128
PEER_TOPK = 16
EPS = 1e-6
NEG = -1e30
REMOVED = -3.0e38
TAKEN = -2.0 ** 120
TAKEN_STEPS = 32.0

LANES = 128
SUBLANES = 8
BF16_ROWS = 16
VMEM_LIMIT = 56 * 1024 * 1024

ATT_TILE = 256
CMP_TILE = 256
N_GATE_COLS = GROUP * 3


def _dot(a, b):
    return jnp.dot(a, b, preferred_element_type=F32)


def _dot_nt(a, b):
    return lax.dot_general(a, b, (((1,), (1,)), ((), ())), preferred_element_type=F32)


def _gelu_tanh(x):
    c = math.sqrt(2.0 / math.pi)
    half = 0.5 * x
    return half + half * jnp.tanh(x * (c + (c * 0.044715) * (x * x)))


def _rms(x, g):
    return x * lax.rsqrt(jnp.mean(x * x, axis=-1, keepdims=True) + EPS) * g


def _params(sem, vmem=VMEM_LIMIT, flags=None):
    return pltpu.CompilerParams(dimension_semantics=sem, vmem_limit_bytes=vmem, flags=flags)


def _in_proj_kernel(x_ref, g_ref, wf_ref, wb_ref, xrg_ref, kvc_ref, gates_ref, q4_ref, kk_ref, vv_ref):
    xb = _rms(x_ref[...], g_ref[...]).astype(BF16)
    yf = _dot(xb, wf_ref[...])
    xrg_ref[...] = yf[:, :1024]
    for j in range(4):
        kvc_ref[j] = yf[:, 1024 + 64 * j:1024 + 64 * (j + 1)]
    gates_ref[0] = yf[:, 1280:1408]
    gates_ref[1] = yf[:, 1408:1536]
    yb = _dot(xb, wb_ref[...])
    for h in range(N_HEADS):
        q4_ref[h] = (yb[:, 64 * h:64 * (h + 1)] * (HEAD_DIM ** -0.5)).astype(BF16)
    ones_col = (lax.broadcasted_iota(jnp.int32, (yb.shape[0], 64), 1) == 0).astype(F32)
    for j in range(4):
        kk_ref[j] = yb[:, 512 + 64 * j:512 + 64 * (j + 1)].astype(BF16)
        vv_ref[j] = jnp.concatenate([yb[:, 768 + 64 * j:768 + 64 * (j + 1)], ones_col], axis=1).astype(BF16)


def _in_proj(x2, g, wf, wb, tm):
    n, d = x2.shape
    return pl.pallas_call(
        _in_proj_kernel,
        grid=(n // tm,),
        in_specs=[
            pl.BlockSpec((tm, d), lambda i: (i, 0)),
            pl.BlockSpec((1, d), lambda i: (0, 0)),
            pl.BlockSpec(wf.shape, lambda i: (0, 0)),
            pl.BlockSpec(wb.shape, lambda i: (0, 0)),
        ],
        out_specs=[
            pl.BlockSpec((tm, 1024), lambda i: (i, 0)),
            pl.BlockSpec((4, tm, 64), lambda i: (0, i, 0)),
            pl.BlockSpec((2, tm, LANES), lambda i: (0, i, 0)),
            pl.BlockSpec((8, tm, 64), lambda i: (0, i, 0)),
            pl.BlockSpec((4, tm, 64), lambda i: (0, i, 0)),
            pl.BlockSpec((4, tm, LANES), lambda i: (0, i, 0)),
        ],
        out_shape=[
            jax.ShapeDtypeStruct((n, 1024), F32),
            jax.ShapeDtypeStruct((4, n, 64), F32),
            jax.ShapeDtypeStruct((2, n, LANES), F32),
            jax.ShapeDtypeStruct((8, n, 64), BF16),
            jax.ShapeDtypeStruct((4, n, 64), BF16),
            jax.ShapeDtypeStruct((4, n, LANES), BF16),
        ],
        compiler_params=_params(("parallel",)),
        name="in_proj",
    )(x2, g, wf, wb)


def _compress_kernel(r_ref, pe_ref, w1_ref, w2_ref, o_ref):
    r = r_ref[0, 0]
    n_rows = r.shape[0]
    half = r.shape[1]
    pe = pe_ref[0]
    top = _dot((r + pe[:, :half]).astype(BF16), w1_ref[0, :half, :])
    bot = _dot((r + pe[:, half:]).astype(BF16), w1_ref[0, half:, :])
    hid = top + pltpu.roll(bot, n_rows - 1, 0)
    act = jax.nn.gelu(hid)
    out = _dot(act.astype(BF16), w2_ref[0])
    row = lax.broadcasted_iota(jnp.int32, out.shape, 0)
    o_ref[0, 0] = jnp.where(row < n_rows - 1, out, 0.0).astype(BF16)


def _compress(r4, pe2, w1s, w2s):
    four, b, n_rows, width = r4.shape
    return pl.pallas_call(
        _compress_kernel,
        grid=(four, b),
        in_specs=[
            pl.BlockSpec((1, 1, n_rows, width), lambda j, bb: (j, bb, 0, 0)),
            pl.BlockSpec((1, 1, 2 * width), lambda j, bb: (j // 2, 0, 0)),
            pl.BlockSpec((1, 2 * width, w1s.shape[2]), lambda j, bb: (j // 2, 0, 0)),
            pl.BlockSpec((1, w2s.shape[1], 64), lambda j, bb: (j // 2, 0, 0)),
        ],
        out_specs=pl.BlockSpec((1, 1, n_rows, 64), lambda j, bb: (j, bb, 0, 0)),
        out_shape=jax.ShapeDtypeStruct((four, b, n_rows, 64), BF16),
        compiler_params=_params(("parallel", "parallel")),
        name="compress",
    )(r4, pe2, w1s, w2s)


def _bucket_breakpoints():
    n = np.arange(0, 4 * REL_MAX_DIST)
    max_exact = REL_BUCKETS // 2
    nf = np.maximum(n, max_exact).astype(np.float32)
    large = max_exact + (np.log(nf / max_exact) / np.float32(math.log(REL_MAX_DIST / max_exact))
                         * (REL_BUCKETS - max_exact)).astype(np.int32)
    large = np.minimum(large, REL_BUCKETS - 1)
    bucket = np.where(n < max_exact, n, large)
    return [int(np.argmax(bucket >= j)) for j in range(1, REL_BUCKETS)]


_BREAKS = _bucket_breakpoints()


def _bias_of_dist(dist, tab_ref, h):
    v = jnp.full(dist.shape, tab_ref[0, h], F32)
    for j, bp in enumerate(_BREAKS, start=1):
        v = jnp.where(dist >= bp, tab_ref[j, h], v)
    v = v - tab_ref[REL_BUCKETS - 1, h]
    return jnp.where(dist >= 0, v, NEG)


WIN_SPAN = WINDOW + ATT_TILE
NEAR_SPAN = 2 * ATT_TILE


def _bias_near_kernel(tab_ref, win_ref, sel_ref):
    h = pl.program_id(0)
    t = ATT_TILE
    for ref, span in ((win_ref, WIN_SPAN), (sel_ref, NEAR_SPAN)):
        r = lax.broadcasted_iota(jnp.int32, (t, span), 0)
        c = lax.broadcasted_iota(jnp.int32, (t, span), 1)
        dist = r - c + (span - t)
        ref[0] = jnp.where(dist < WINDOW, _bias_of_dist(dist, tab_ref, h), NEG)


def _bias_near(rel_table):
    t = ATT_TILE
    return pl.pallas_call(
        _bias_near_kernel,
        grid=(N_HEADS,),
        in_specs=[pl.BlockSpec(memory_space=pltpu.SMEM)],
        out_specs=[pl.BlockSpec((1, t, WIN_SPAN), lambda h: (h, 0, 0)),
                   pl.BlockSpec((1, t, NEAR_SPAN), lambda h: (h, 0, 0))],
        out_shape=[jax.ShapeDtypeStruct((N_HEADS, t, WIN_SPAN), F32),
                   jax.ShapeDtypeStruct((N_HEADS, t, NEAR_SPAN), F32)],
        compiler_params=_params(("arbitrary",)),
        name="bias_near",
    )(rel_table)


def _bias_cmp_kernel(tab_ref, o_ref, *, n_cmp):
    h = pl.program_id(0)
    i = pl.program_id(1)
    tq, nc = o_ref.shape[1], o_ref.shape[2]
    t = lax.broadcasted_iota(jnp.int32, (tq, nc), 0) + i * tq
    c = lax.broadcasted_iota(jnp.int32, (tq, nc), 1)
    dist = t - (c * CMP_STRIDE + CMP_LEN - 1)
    o_ref[0] = jnp.where(c < n_cmp, _bias_of_dist(dist, tab_ref, h), NEG)


def _bias_cmp(rel_table, seq, n_rows, tq):
    return pl.pallas_call(
        functools.partial(_bias_cmp_kernel, n_cmp=n_rows - 1),
        grid=(N_HEADS, seq // tq),
        in_specs=[pl.BlockSpec(memory_space=pltpu.SMEM)],
        out_specs=pl.BlockSpec((1, tq, n_rows), lambda h, i: (h, i, 0)),
        out_shape=jax.ShapeDtypeStruct((N_HEADS, seq, n_rows), F32),
        compiler_params=_params(("arbitrary", "arbitrary")),
        name="bias_cmp",
    )(rel_table)


def _cmp_attn_kernel(q_ref, kc_ref, vc_ref, addc_ref, ovlt_ref, eye_ref, oc_ref, sel_ref):
    i = pl.program_id(1)
    tq = q_ref.shape[1]
    n_sel = ovlt_ref.shape[0]
    kc = kc_ref[0, 0]
    vc = vc_ref[0, 0]
    t_col = lax.broadcasted_iota(jnp.int32, (tq, 1), 0) + i * tq
    row_ok = (t_col >= CMP_LEN - 1).astype(F32)
    psum = None
    for hh in range(GROUP):
        logit = _dot_nt(q_ref[hh], kc) + addc_ref[hh]
        m = jnp.max(logit, axis=-1, keepdims=True)
        e = jnp.exp(logit - m)
        p = e / jnp.sum(e, axis=-1, keepdims=True)
        oc_ref[:, 64 * hh:64 * (hh + 1)] = _dot(p.astype(BF16), vc) * row_ok
        psum = p if psum is None else psum + p
    psum = psum * row_ok
    hi = psum.astype(BF16)
    lo = (psum - hi.astype(F32)).astype(BF16)
    ovlt = ovlt_ref[...]
    imp = _dot_nt(ovlt, hi) + _dot_nt(ovlt, lo)
    s_idx = lax.broadcasted_iota(jnp.int32, (n_sel, tq), 0)
    t_blk = (lax.broadcasted_iota(jnp.int32, (n_sel, tq), 1) + i * tq) // SEL_LEN
    d_blk = t_blk - s_idx
    local = (d_blk >= 0) & (d_blk < N_LOCAL_BLOCKS)
    v = jnp.where(local, FORCE_LOCAL,
                  jnp.where(s_idx == 0, FORCE_INIT, jnp.where(d_blk >= 0, imp, -1.0)))
    rows = s_idx.astype(F32)
    sel = jnp.zeros((n_sel, tq), F32)
    for _ in range(min(SEL_TOPK, n_sel)):
        m = jnp.max(v, axis=0, keepdims=True)
        first = jnp.min(jnp.where(v == m, rows, float(n_sel)), axis=0, keepdims=True)
        pick = rows == first
        sel = jnp.where(pick, 1.0, sel)
        v = jnp.where(pick, REMOVED, v)
    sel_ref[0] = _dot_nt(eye_ref[...], sel.astype(BF16)).astype(BF16)


def _cmp_attn(q4, cmp4, addc, ovlt, batch, seq):
    tq = min(CMP_TILE, seq)
    nt = seq // tq
    n_rows = cmp4.shape[2]
    n_sel = ovlt.shape[0]
    eye = jnp.eye(tq, dtype=BF16)
    return pl.pallas_call(
        _cmp_attn_kernel,
        grid=(batch * N_KV_HEADS, nt),
        in_specs=[
            pl.BlockSpec((GROUP, tq, 64), lambda g, i: (g % 2, (g // 2) * nt + i, 0)),
            pl.BlockSpec((1, 1, n_rows, 64), lambda g, i: (g % 2, g // 2, 0, 0)),
            pl.BlockSpec((1, 1, n_rows, 64), lambda g, i: (2 + g % 2, g // 2, 0, 0)),
            pl.BlockSpec((GROUP, tq, n_rows), lambda g, i: (g % 2, i, 0)),
            pl.BlockSpec(ovlt.shape, lambda g, i: (0, 0)),
            pl.BlockSpec((tq, tq), lambda g, i: (0, 0)),
        ],
        out_specs=[
            pl.BlockSpec((tq, GROUP * 64), lambda g, i: ((g // 2) * nt + i, g % 2)),
            pl.BlockSpec((1, tq, n_sel), lambda g, i: (g, i, 0)),
        ],
        out_shape=[
            jax.ShapeDtypeStruct((batch * seq, N_HEADS * 64), F32),
            jax.ShapeDtypeStruct((batch * N_KV_HEADS, seq, n_sel), BF16),
        ],
        compiler_params=_params(("parallel", "arbitrary")),
        name="cmp_attn",
    )(q4, cmp4, cmp4, addc, ovlt, eye)


def _softmax_step(qs, k, v1, mask, add, carry):
    tq = qs.shape[0] // GROUP
    s = _dot_nt(qs, k).reshape(GROUP, tq, k.shape[0])
    if mask is not None:
        s = jnp.where(mask[None] > 0.5, s, NEG)
    if add is not None:
        s = s + add
    m_tile = jnp.max(s, axis=-1, keepdims=True)
    if carry is None:
        m_new = m_tile
    else:
        m, acc = carry
        m_new = jnp.maximum(m, m_tile)
    p = jnp.exp(s - m_new)
    pv = _dot(p.reshape(GROUP * tq, k.shape[0]).astype(BF16), v1).reshape(GROUP, tq, v1.shape[1])
    if carry is None:
        return m_new, pv
    return m_new, jnp.exp(m - m_new) * acc + pv


FAR_SLAB = 4 * ATT_TILE


def _sel_win_kernel(q_ref, ks_ref, vs_ref, kw_ref, vw_ref, sel_ref, enear_ref, eslab_ref,
                    winadd_ref, seladd_ref, oc_ref, gates_ref, o_ref):
    i = pl.program_id(1)
    tq = q_ref.shape[1]
    qs = q_ref[...].reshape(GROUP * tq, 64)
    sel = sel_ref[0]

    def rows(ref, start, size):
        return ref[0, 0, pl.ds(pl.multiple_of(start, ATT_TILE), size), :]

    def hide_padding(add, n_pad):
        col = lax.broadcasted_iota(jnp.int32, (1, 1, add.shape[2]), 2)
        return jnp.where(col < n_pad, NEG, add)

    w0 = i * tq
    _, acc_w = _softmax_step(qs, rows(kw_ref, w0, WIN_SPAN), rows(vw_ref, w0, WIN_SPAN), None,
                             hide_padding(winadd_ref[...], WINDOW - i * tq), None)

    n0 = w0 + WINDOW - tq
    carry = _softmax_step(qs, rows(ks_ref, n0, NEAR_SPAN), rows(vs_ref, n0, NEAR_SPAN),
                          _dot(sel, enear_ref[i]), hide_padding(seladd_ref[...], tq - i * tq), None)
    far_end = jnp.maximum(i - 1, 0) * tq
    col = lax.broadcasted_iota(jnp.int32, (1, FAR_SLAB), 1)

    def far_slab(j, c):
        start = WINDOW + j * FAR_SLAB
        mask = jnp.where(col < far_end - j * FAR_SLAB, _dot(sel, eslab_ref[j]), 0.0)
        return _softmax_step(qs, rows(ks_ref, start, FAR_SLAB), rows(vs_ref, start, FAR_SLAB), mask, None, c)

    _, acc_s = lax.fori_loop(0, (far_end + FAR_SLAB - 1) // FAR_SLAB, far_slab, carry)
    o_s = acc_s[..., :64] / acc_s[..., 64:65]
    o_w = acc_w[..., :64] / acc_w[..., 64:65]

    g = jax.nn.sigmoid(gates_ref[0])
    for hh in range(GROUP):
        o_c = oc_ref[:, 64 * hh:64 * (hh + 1)]
        o_ref[:, 64 * hh:64 * (hh + 1)] = (g[:, 3 * hh:3 * hh + 1] * o_c
                                            + g[:, 3 * hh + 1:3 * hh + 2] * o_s[hh]
                                            + g[:, 3 * hh + 2:3 * hh + 3] * o_w[hh])


def _sel_win_attn(q4, kk5, vv5, sel, enear, eslab, winadd, seladd, o_c, gates2, batch, seq):
    tq = ATT_TILE
    nt = seq // tq
    n_sel = sel.shape[2]
    padded = kk5.shape[2]

    def kv_spec(arr, base):
        return pl.BlockSpec((1, 1, padded, arr.shape[3]), lambda g, i: (base + g % 2, g // 2, 0, 0))

    whole = lambda arr: pl.BlockSpec(arr.shape, lambda g, i: (0,) * arr.ndim)
    return pl.pallas_call(
        _sel_win_kernel,
        grid=(batch * N_KV_HEADS, nt),
        in_specs=[
            pl.BlockSpec((GROUP, tq, 64), lambda g, i: (g % 2, (g // 2) * nt + i, 0)),
            kv_spec(kk5, 0), kv_spec(vv5, 0), kv_spec(kk5, 2), kv_spec(vv5, 2),
            pl.BlockSpec((1, tq, n_sel), lambda g, i: (g, i, 0)),
            whole(enear), whole(eslab),
            pl.BlockSpec((GROUP, tq, WIN_SPAN), lambda g, i: (g % 2, 0, 0)),
            pl.BlockSpec((GROUP, tq, NEAR_SPAN), lambda g, i: (g % 2, 0, 0)),
            pl.BlockSpec((tq, GROUP * 64), lambda g, i: ((g // 2) * nt + i, g % 2)),
            pl.BlockSpec((1, tq, LANES), lambda g, i: (g % 2, (g // 2) * nt + i, 0)),
        ],
        out_specs=pl.BlockSpec((tq, GROUP * 64), lambda g, i: ((g // 2) * nt + i, g % 2)),
        out_shape=jax.ShapeDtypeStruct((batch * seq, N_HEADS * 64), F32),
        compiler_params=_params(("parallel", "arbitrary")),
        name="sel_win_attn",
    )(q4, kk5, vv5, kk5, vv5, sel, enear, eslab, winadd, seladd, o_c, gates2)


def _rglru_kernel(xrg_ref, xg_ref, cw_ref, cb_ref, wa_ref, ba_ref, wx_ref, bx_ref, lam_ref, gn_ref,
                  o_ref, xs_ref, h_ref):
    step = pl.program_id(1)
    tc = xrg_ref.shape[0]

    @pl.when(step == 0)
    def _():
        xs_ref[0:8, :] = jnp.zeros((8, xs_ref.shape[1]), F32)
        h_ref[...] = jnp.zeros(h_ref.shape, F32)

    xs_ref[8:, :] = xrg_ref[...]
    xc = cb_ref[...] + cw_ref[CONV_WIDTH - 1:CONV_WIDTH, :] * xs_ref[8:, :]
    for k in range(1, CONV_WIDTH):
        xc = xc + cw_ref[CONV_WIDTH - 1 - k:CONV_WIDTH - k, :] * xs_ref[pl.ds(8 - k, tc), :]
    xs_ref[0:8, :] = xs_ref[tc:tc + 8, :]

    xcb = xc.astype(BF16)
    r = jax.nn.sigmoid(_dot(xcb, wa_ref[...]) + ba_ref[...])
    gi = jax.nn.sigmoid(_dot(xcb, wx_ref[...]) + bx_ref[...])
    lam = lam_ref[...]
    softplus_neg = jnp.maximum(-lam, 0.0) + jnp.log1p(jnp.exp(-jnp.abs(lam)))
    log_a = -LRU_C * r * softplus_neg
    a = jnp.exp(log_a)
    b = jnp.sqrt(1.0 - jnp.exp(2.0 * log_a)) * (gi * xc)

    row = lax.broadcasted_iota(jnp.int32, (tc, 1), 0)
    s = 1
    while s < tc:
        valid = row >= s
        b = jnp.where(valid, a * pltpu.roll(b, s, 0) + b, b)
        a = jnp.where(valid, a * pltpu.roll(a, s, 0), a)
        s *= 2
    h = b + a * h_ref[0:1, :]
    h_ref[0:1, :] = h[tc - 1:tc, :]
    o_ref[...] = _rms(h * jax.nn.gelu(xg_ref[...]), gn_ref[...]).astype(BF16)


def _rglru(xrg, cw, cb, wa, ba, wx, bx, lam, gn, batch, seq, tc):
    nt = seq // tc
    w = cw.shape[1]
    vec = pl.BlockSpec((1, w), lambda b, i: (0, 0))
    return pl.pallas_call(
        _rglru_kernel,
        grid=(batch, nt),
        in_specs=[
            pl.BlockSpec((tc, w), lambda b, i: (b * nt + i, 0)),
            pl.BlockSpec((tc, w), lambda b, i: (b * nt + i, 1)),
            pl.BlockSpec((CONV_WIDTH, w), lambda b, i: (0, 0)),
            vec,
            pl.BlockSpec((w, w), lambda b, i: (0, 0)), vec,
            pl.BlockSpec((w, w), lambda b, i: (0, 0)), vec,
            vec, vec,
        ],
        out_specs=pl.BlockSpec((tc, w), lambda b, i: (b * nt + i, 0)),
        out_shape=jax.ShapeDtypeStruct((batch * seq, w), BF16),
        scratch_shapes=[pltpu.VMEM((tc + 8, w), F32), pltpu.VMEM((8, w), F32)],
        compiler_params=_params(("parallel", "arbitrary")),
        name="rglru",
    )(xrg, xrg, cw, cb, wa, ba, wx, bx, lam, gn)


def _out_proj_kernel(a_ref, l_ref, x_ref, ga_ref, woa_ref, wol_ref, gf_ref, wq_ref, sk_ref,
                     h1_ref, xt_ref, st_ref):
    an = _rms(a_ref[...], ga_ref[...]).astype(BF16)
    h1 = x_ref[...] + _dot(an, woa_ref[...]) + _dot(l_ref[...], wol_ref[...])
    h1_ref[...] = h1
    xn = _rms(h1, gf_ref[...])
    xt_ref[...] = xn.T.astype(BF16)
    q = _dot(xn.astype(BF16), wq_ref[...])
    for hp in range(2 * PEER_HEADS):
        qs = q[:, LANES * hp:LANES * (hp + 1)].astype(BF16)
        st_ref[hp] = _dot_nt(sk_ref[hp % 2], qs)


def _out_proj(a_out, l_n, x2, ga, woa, wol, gf, wq, sk, tm):
    n, d = x2.shape
    full = lambda arr: pl.BlockSpec(arr.shape, lambda i: (0,) * arr.ndim)
    return pl.pallas_call(
        _out_proj_kernel,
        grid=(n // tm,),
        in_specs=[
            pl.BlockSpec((tm, a_out.shape[1]), lambda i: (i, 0)),
            pl.BlockSpec((tm, l_n.shape[1]), lambda i: (i, 0)),
            pl.BlockSpec((tm, d), lambda i: (i, 0)),
            full(ga), full(woa), full(wol), full(gf), full(wq), full(sk),
        ],
        out_specs=[
            pl.BlockSpec((tm, d), lambda i: (i, 0)),
            pl.BlockSpec((d, tm), lambda i: (0, i)),
            pl.BlockSpec((2 * PEER_HEADS, PEER_N_KEYS, tm), lambda i: (0, 0, i)),
        ],
        out_shape=[
            jax.ShapeDtypeStruct((n, d), F32),
            jax.ShapeDtypeStruct((d, n), BF16),
            jax.ShapeDtypeStruct((2 * PEER_HEADS, PEER_N_KEYS, n), F32),
        ],
        compiler_params=_params(("parallel",)),
        name="out_proj",
    )(a_out, l_n, x2, ga, woa, wol, gf, wq, sk)


N_TOP = PEER_TOPK
_PAIRS = [(i, j) for i in range(N_TOP) for j in range(N_TOP) if (i + 1) * (j + 1) <= N_TOP]


def _top_values(s):
    n, tl = s.shape
    parts = [s[SUBLANES * k:SUBLANES * (k + 1)] for k in range(n // SUBLANES)]
    base = lax.broadcasted_iota(jnp.int32, (SUBLANES, tl), 0).astype(F32)
    rows = [base + float(SUBLANES * k) for k in range(len(parts))]
    out = []
    for r in range(N_TOP):
        m = jnp.max(functools.reduce(jnp.maximum, parts), axis=0, keepdims=True)
        out.append(m)
        first = jnp.full((SUBLANES, tl), float(n), F32)
        for k in reversed(range(len(parts))):
            first = jnp.where(parts[k] == m, rows[k], first)
        first = jnp.min(first, axis=0, keepdims=True)
        taken = TAKEN * (1.0 + r / TAKEN_STEPS)
        parts = [jnp.where(rows[k] == first, taken, parts[k]) for k in range(len(parts))]
    rank = jnp.concatenate(parts, axis=0)
    rank = jnp.where(rank <= TAKEN, (rank * (1.0 / TAKEN) - 1.0) * TAKEN_STEPS, float(PEER_TOPK))
    return out, rank


def _peer_stats_kernel(st_ref, r2_ref, e2_ref, l_ref, e1_ref):
    tops = [[], []]
    rank1 = []
    for h in range(PEER_HEADS):
        vals, rank = _top_values(st_ref[2 * h])
        tops[0].append(vals)
        rank1.append(rank)
        vals, rank = _top_values(st_ref[2 * h + 1])
        tops[1].append(vals)
        r2_ref[h] = rank.astype(BF16)
    a = [jnp.concatenate([tops[0][h][i] for h in range(PEER_HEADS)], axis=0) for i in range(N_TOP)]
    b = [jnp.concatenate([tops[1][h][j] for h in range(PEER_HEADS)], axis=0) for j in range(N_TOP)]
    cand = [a[i] + b[j] for (i, j) in _PAIRS]
    work = list(cand)
    kth = []
    for _ in range(N_TOP):
        m = functools.reduce(jnp.maximum, work)
        kth.append(m)
        found = jnp.zeros(m.shape, jnp.bool_)
        for k in range(len(work)):
            hit = (work[k] == m) & jnp.logical_not(found)
            work[k] = jnp.where(hit, REMOVED, work[k])
            found = found | hit
    thr = kth[PEER_TOPK - 1]
    top = a[0] + b[0]
    z = functools.reduce(lambda u, w: u + w, [jnp.where(c >= thr, jnp.exp(c - top), 0.0) for c in cand])
    n_sel = [functools.reduce(lambda u, w: u + w,
                              [(a[r] + b[j] >= thr).astype(F32) for j in range(PEER_TOPK)])
             for r in range(PEER_TOPK)]
    for h in range(PEER_HEADS):
        row = slice(h, h + 1)
        prefix = jnp.zeros(rank1[h].shape, F32)
        for r in range(PEER_TOPK):
            prefix = jnp.where(rank1[h] == float(r), n_sel[r][row], prefix)
        l_ref[h] = prefix
        e1_ref[h] = jnp.exp(st_ref[2 * h] - a[0][row])
        e2_ref[h] = (jnp.exp(st_ref[2 * h + 1] - b[0][row]) / z[row]).astype(BF16)


def _peer_stats(st, tl):
    hp, nk, n = st.shape
    out_spec = pl.BlockSpec((PEER_HEADS, nk, tl), lambda i: (0, 0, i))
    return pl.pallas_call(
        _peer_stats_kernel,
        grid=(n // tl,),
        in_specs=[pl.BlockSpec((hp, nk, tl), lambda i: (0, 0, i))],
        out_specs=[out_spec] * 4,
        out_shape=[
            jax.ShapeDtypeStruct((PEER_HEADS, nk, n), BF16),
            jax.ShapeDtypeStruct((PEER_HEADS, nk, n), BF16),
            jax.ShapeDtypeStruct((PEER_HEADS, nk, n), F32),
            jax.ShapeDtypeStruct((PEER_HEADS, nk, n), F32),
        ],
        compiler_params=_params(("parallel",)),
        name="peer_stats",
    )(st)


def _peer_main_kernel(u_ref, vt_ref, xt_ref, r2_ref, e2_ref, l_ref, e1_ref, h1_ref, o_ref, acc_ref):
    c = pl.program_id(1)
    ce = u_ref.shape[0]
    tm = xt_ref.shape[1]
    nk = PEER_N_KEYS

    @pl.when(c == 0)
    def _():
        acc_ref[...] = jnp.zeros(acc_ref.shape, F32)

    act = _gelu_tanh(_dot(u_ref[...], xt_ref[...]))

    def rows_bf16(ref, h, i1):
        packed = jnp.broadcast_to(ref[h, pl.ds(i1, 1), :], (BF16_ROWS, tm)).astype(BF16)
        return jnp.concatenate([packed] * (nk // BF16_ROWS), axis=0)

    w_parts = []
    for ii in range(ce // nk):
        i1 = c * (ce // nk) + ii
        g = None
        for h in range(PEER_HEADS):
            e2 = e2_ref[h]
            term = jnp.where(r2_ref[h] < rows_bf16(l_ref, h, i1), e2, jnp.zeros_like(e2)) * rows_bf16(e1_ref, h, i1)
            g = term if g is None else g + term
        w_parts.append(act[ii * nk:(ii + 1) * nk, :].astype(BF16) * g)
    w = jnp.concatenate(w_parts, axis=0)
    acc_ref[...] += _dot(vt_ref[...], w)

    @pl.when(c == pl.num_programs(1) - 1)
    def _():
        o_ref[...] = h1_ref[...] + acc_ref[...].T


def _peer_main(u_b, vt_b, xt, r2, e2, n_sel, e1, h1, tm, ce):
    ne, d = u_b.shape
    n = xt.shape[1]
    per_tok = pl.BlockSpec((PEER_HEADS, PEER_N_KEYS, tm), lambda t, c: (0, 0, t))
    return pl.pallas_call(
        _peer_main_kernel,
        grid=(n // tm, ne // ce),
        in_specs=[
            pl.BlockSpec((ce, d), lambda t, c: (c, 0)),
            pl.BlockSpec((d, ce), lambda t, c: (0, c)),
            pl.BlockSpec((d, tm), lambda t, c: (0, t)),
            per_tok, per_tok, per_tok, per_tok,
            pl.BlockSpec((tm, d), lambda t, c: (t, 0)),
        ],
        out_specs=pl.BlockSpec((tm, d), lambda t, c: (t, 0)),
        out_shape=jax.ShapeDtypeStruct((n, d), F32),
        scratch_shapes=[pltpu.VMEM((d, tm), F32)],
        compiler_params=_params(("parallel", "arbitrary")),
        name="peer_main",
    )(u_b, vt_b, xt, r2, e2, n_sel, e1, h1)


def _ple_final_kernel(h_ref, p_ref, gp_ref, wg_ref, bg_ref, wp_ref, gfin_ref, o_ref):
    h = h_ref[...]
    gate = jax.nn.sigmoid(_dot(_rms(h, gp_ref[...]).astype(BF16), wg_ref[...]) + bg_ref[...])
    h = h + gate * _dot(p_ref[...].astype(BF16), wp_ref[...])
    o_ref[...] = _rms(h, gfin_ref[...])


def _ple_final(h2, p2, gp, wg, bg, wp, gfin, tm):
    n, d = h2.shape
    full = lambda arr: pl.BlockSpec(arr.shape, lambda i: (0,) * arr.ndim)
    return pl.pallas_call(
        _ple_final_kernel,
        grid=(n // tm,),
        in_specs=[
            pl.BlockSpec((tm, d), lambda i: (i, 0)),
            pl.BlockSpec((tm, p2.shape[1]), lambda i: (i, 0)),
            full(gp), full(wg), full(bg), full(wp), full(gfin),
        ],
        out_specs=pl.BlockSpec((tm, d), lambda i: (i, 0)),
        out_shape=jax.ShapeDtypeStruct((n, d), F32),
        compiler_params=_params(("parallel",)),
        name="ple_final",
    )(h2, p2, gp, wg, bg, wp, gfin)


def _overlap_t(seq, n_rows):
    n_cmp = n_rows - 1
    cmp_start = np.arange(n_cmp) * CMP_STRIDE
    sel_start = np.arange(seq // SEL_LEN) * SEL_LEN
    ov = np.clip(np.minimum(cmp_start[:, None] + CMP_LEN, sel_start[None, :] + SEL_LEN)
                 - np.maximum(cmp_start[:, None], sel_start[None, :]), 0, None).astype(np.float32) / CMP_LEN
    out = np.zeros((seq // SEL_LEN, n_rows), np.float32)
    out[:, :n_cmp] = ov.T
    return out


def _block_expand(seq, starts, width):
    pos = np.asarray(starts).reshape(-1, 1, 1) + np.arange(width).reshape(1, 1, width)
    blk = np.where(pos >= 0, pos // SEL_LEN, -1)
    return (blk == np.arange(seq // SEL_LEN).reshape(1, -1, 1)).astype(np.float32)


def _block_diag(w):
    nb, bd, _ = w.shape
    out = jnp.zeros((nb * bd, nb * bd), w.dtype)
    for j in range(nb):
        out = out.at[j * bd:(j + 1) * bd, j * bd:(j + 1) * bd].set(w[j])
    return out


def _layer(h, p_i, attn_norm, w_in, cmp_k_pe, cmp_k_w1, cmp_k_w2, cmp_v_pe, cmp_v_w1, cmp_v_w2,
           rel_table, conv_w, conv_b, lru_wa, lru_ba, lru_wx, lru_bx, lru_lambda,
           grp_norm_attn, grp_norm_lru, w_out, ffn_norm, peer_wq, peer_subkeys, peer_u, peer_v,
           ple_norm, ple_wgate, ple_bgate, ple_proj, *, tm, tc, tl, ce):
    batch, seq, d = h.shape
    n = batch * seq
    x2 = h.reshape(n, d)
    aw = N_HEADS * HEAD_DIM
    kvw = N_KV_HEADS * HEAD_DIM
    lw = d - aw

    o = 0
    cols = {}
    for name, width in (("q", aw), ("kc", kvw), ("vc", kvw), ("ks", kvw), ("vs", kvw), ("kw", kvw),
                        ("vw", kvw), ("gates", N_HEADS * 3), ("xr", lw), ("xg", lw)):
        cols[name] = w_in[:, o:o + width]
        o += width
    gate_pad = [jnp.pad(cols["gates"][:, N_GATE_COLS * k:N_GATE_COLS * (k + 1)], ((0, 0), (0, LANES - N_GATE_COLS)))
                for k in range(N_KV_HEADS)]
    wf = jnp.concatenate([cols["xr"], cols["xg"], cols["kc"], cols["vc"]] + gate_pad, axis=1).astype(BF16)
    wb = jnp.concatenate([cols["q"], cols["ks"], cols["kw"], cols["vs"], cols["vw"]], axis=1).astype(BF16)

    xrg, kvc, gates2, q4, kk, vv = _in_proj(x2, attn_norm.reshape(1, d), wf, wb, tm)

    n_rows = seq // CMP_STRIDE
    r4 = kvc.reshape(4, batch, n_rows, CMP_STRIDE * HEAD_DIM)
    pe2 = jnp.stack([cmp_k_pe.reshape(1, -1), cmp_v_pe.reshape(1, -1)])
    w1s = jnp.stack([cmp_k_w1, cmp_v_w1]).astype(BF16)
    w2s = jnp.stack([cmp_k_w2, cmp_v_w2]).astype(BF16)
    cmp4 = _compress(r4, pe2, w1s, w2s)

    winadd, seladd = _bias_near(rel_table)
    addc = _bias_cmp(rel_table, seq, n_rows, min(seq, 512))
    ovlt = jnp.asarray(_overlap_t(seq, n_rows), BF16)
    o_c, sel = _cmp_attn(q4, cmp4, addc, ovlt, batch, seq)

    nt = seq // ATT_TILE
    tile_starts = np.arange(nt) * ATT_TILE
    enear = jnp.asarray(_block_expand(seq, tile_starts - ATT_TILE, NEAR_SPAN), BF16)
    eslab = jnp.asarray(_block_expand(seq, np.arange(max(seq // FAR_SLAB, 1)) * FAR_SLAB, FAR_SLAB), BF16)
    front = ((0, 0), (0, 0), (WINDOW, 0), (0, 0))
    kk5 = jnp.pad(kk.reshape(4, batch, seq, HEAD_DIM), front)
    vv5 = jnp.pad(vv.reshape(4, batch, seq, LANES), front)
    a_out = _sel_win_attn(q4, kk5, vv5, sel, enear, eslab, winadd, seladd, o_c, gates2, batch, seq)

    l_n = _rglru(xrg, conv_w.reshape(CONV_WIDTH, lw), conv_b.reshape(1, lw),
                 _block_diag(lru_wa).astype(BF16), lru_ba.reshape(1, lw),
                 _block_diag(lru_wx).astype(BF16), lru_bx.reshape(1, lw),
                 lru_lambda.reshape(1, lw), grp_norm_lru.reshape(1, lw), batch, seq, tc)

    w_out_b = w_out.astype(BF16)
    h1, xt, st = _out_proj(a_out, l_n, x2, grp_norm_attn.reshape(1, aw), w_out_b[:aw], w_out_b[aw:],
                           ffn_norm.reshape(1, d), peer_wq.astype(BF16), peer_subkeys.astype(BF16), tm)
    r2, e2, n_sel, e1 = _peer_stats(st, tl)
    h2 = _peer_main(peer_u.astype(BF16), peer_v.T.astype(BF16), xt, r2, e2, n_sel, e1, h1, tm, ce)
    ple_args = (h2, p_i.reshape(n, -1), ple_norm.reshape(1, d), ple_wgate.astype(BF16), ple_bgate.reshape(1, d),
                ple_proj.astype(BF16))
    return ple_args, dict(a_out=a_out, l_n=l_n, h1=h1)


def kernel(x, p, attn_norm, w_in, cmp_k_pe, cmp_k_w1, cmp_k_w2, cmp_v_pe, cmp_v_w1, cmp_v_w2, rel_table, conv_w, conv_b, lru_wa, lru_ba, lru_wx, lru_bx, lru_lambda, grp_norm_attn, grp_norm_lru, w_out, ffn_norm, peer_wq, peer_subkeys, peer_u, peer_v, ple_norm, ple_wgate, ple_bgate, ple_proj, final_norm):
    batch, seq, d = x.shape
    assert w_in.shape[0] == 1, "single-layer stack only"
    tm = min(512, seq)
    (h2, p2, gp, wg, bg, wp), _ = _layer(
        x, p[0], attn_norm[0], w_in[0], cmp_k_pe[0], cmp_k_w1[0], cmp_k_w2[0], cmp_v_pe[0], cmp_v_w1[0],
        cmp_v_w2[0], rel_table, conv_w[0], conv_b[0], lru_wa[0], lru_ba[0], lru_wx[0], lru_bx[0],
        lru_lambda[0], grp_norm_attn[0], grp_norm_lru[0], w_out[0], ffn_norm[0], peer_wq[0],
        peer_subkeys[0], peer_u[0], peer_v[0], ple_norm[0], ple_wgate[0], ple_bgate[0], ple_proj[0],
        tm=tm, tc=min(256, seq), tl=min(256, seq), ce=2048)
    out = _ple_final(h2, p2, gp, wg, bg, wp, final_norm.reshape(1, d), tm)
    return out.reshape(batch, seq, d)
```

```python
import functools
import math

import numpy as np
import jax
import jax.numpy as jnp
from jax import lax
from jax.experimental import pallas as pl
from jax.experimental.pallas import tpu as pltpu

F32 = jnp.float32
BF16 = jnp.bfloat16

N_HEADS = 8
HEAD_DIM = 64
N_KV_HEADS = 2
GROUP = N_HEADS // N_KV_HEADS
LRU_C = 8.0
CONV_WIDTH = 4
CMP_LEN = 32
CMP_STRIDE = 16
SEL_LEN = 64
SEL_TOPK = 16
N_LOCAL_BLOCKS = 2
WINDOW = 512
FORCE_LOCAL = 2.0e4
FORCE_INIT = 1.0e4
REL_BUCKETS = 32
REL_MAX_DIST = 128
PEER_HEADS = 8
PEER_N_KEYS = 128
PEER_TOPK = 16
EPS = 1e-6
NEG = -1e30
LOG2E = math.log2(math.e)
Q_SCALE = HEAD_DIM ** -0.5 * LOG2E
REMOVED = -3.0e38
TAKEN = -2.0 ** 120
TAKEN_STEPS = 32.0

LANES = 128
SUBLANES = 8
BF16_ROWS = 16
VMEM_LIMIT = 56 * 1024 * 1024

ATT_TILE = 256
CMP_TILE = 512
N_GATE_COLS = GROUP * 3


def _dot(a, b):
    return jnp.dot(a, b, preferred_element_type=F32)


def _dot_nt(a, b):
    return lax.dot_general(a, b, (((1,), (1,)), ((), ())), preferred_element_type=F32)


def _gelu_tanh(x):
    c = math.sqrt(2.0 / math.pi)
    half = 0.5 * x
    return half + half * jnp.tanh(x * (c + (c * 0.044715) * (x * x)))


def _rms(x, g):
    return x * lax.rsqrt(jnp.mean(x * x, axis=-1, keepdims=True) + EPS) * g


def _params(sem, vmem=VMEM_LIMIT):
    return pltpu.CompilerParams(dimension_semantics=sem, vmem_limit_bytes=vmem)


def _in_proj_kernel(x_ref, g_ref, wf_ref, wb_ref, xrg_ref, kvc_ref, gates_ref, q4_ref, kk_ref, vv_ref):
    dh = HEAD_DIM
    n_kv = 2 * N_KV_HEADS
    xb = _rms(x_ref[...], g_ref[...]).astype(BF16)
    yf = _dot(xb, wf_ref[...])
    kv0 = xrg_ref.shape[1]
    g0 = kv0 + n_kv * dh
    xrg_ref[...] = yf[:, :kv0]
    for j in range(n_kv):
        kvc_ref[j] = yf[:, kv0 + dh * j:kv0 + dh * (j + 1)]
    for k in range(N_KV_HEADS):
        gates_ref[k] = yf[:, g0 + LANES * k:g0 + LANES * (k + 1)]
    yb = _dot(xb, wb_ref[...])
    for h in range(N_HEADS):
        q4_ref[h] = (yb[:, dh * h:dh * (h + 1)] * Q_SCALE).astype(BF16)
    k0 = N_HEADS * dh
    v0 = k0 + n_kv * dh
    ones_col = (lax.broadcasted_iota(jnp.int32, (yb.shape[0], LANES - dh), 1) == 0).astype(F32)
    for j in range(n_kv):
        kk_ref[j] = yb[:, k0 + dh * j:k0 + dh * (j + 1)].astype(BF16)
        vv_ref[j] = jnp.concatenate([yb[:, v0 + dh * j:v0 + dh * (j + 1)], ones_col], axis=1).astype(BF16)


def _in_proj(x2, g, wf, wb, tm):
    n, d = x2.shape
    dh = HEAD_DIM
    n_kv = 2 * N_KV_HEADS
    lru2 = wf.shape[1] - n_kv * dh - N_KV_HEADS * LANES
    shapes = [
        ((n, lru2), F32),
        ((n_kv, n, dh), F32),
        ((N_KV_HEADS, n, LANES), F32),
        ((N_HEADS, n, dh), BF16),
        ((n_kv, n, dh), BF16),
        ((n_kv, n, LANES), BF16),
    ]

    def spec(shape):
        if len(shape) == 2:
            return pl.BlockSpec((tm, shape[1]), lambda i: (i, 0))
        return pl.BlockSpec((shape[0], tm, shape[2]), lambda i: (0, i, 0))

    return pl.pallas_call(
        _in_proj_kernel,
        grid=(n // tm,),
        in_specs=[
            pl.BlockSpec((tm, d), lambda i: (i, 0)),
            pl.BlockSpec((1, d), lambda i: (0, 0)),
            pl.BlockSpec(wf.shape, lambda i: (0, 0)),
            pl.BlockSpec(wb.shape, lambda i: (0, 0)),
        ],
        out_specs=[spec(shape) for shape, _ in shapes],
        out_shape=[jax.ShapeDtypeStruct(shape, dtype) for shape, dtype in shapes],
        compiler_params=_params(("parallel",)),
        name="in_proj",
    )(x2, g, wf, wb)


def _compress_kernel(r_ref, pe_ref, w1_ref, w2_ref, o_ref):
    r = r_ref[0, 0]
    n_rows = r.shape[0]
    half = r.shape[1]
    pe = pe_ref[0]
    top = _dot((r + pe[:, :half]).astype(BF16), w1_ref[0, :half, :])
    bot = _dot((r + pe[:, half:]).astype(BF16), w1_ref[0, half:, :])
    hid = top + pltpu.roll(bot, n_rows - 1, 0)
    act = jax.nn.gelu(hid)
    out = _dot(act.astype(BF16), w2_ref[0])
    row = lax.broadcasted_iota(jnp.int32, out.shape, 0)
    o_ref[0, 0] = jnp.where(row < n_rows - 1, out, 0.0).astype(BF16)


def _compress(r4, pe2, w1s, w2s):
    four, b, n_rows, width = r4.shape
    return pl.pallas_call(
        _compress_kernel,
        grid=(four, b),
        in_specs=[
            pl.BlockSpec((1, 1, n_rows, width), lambda j, bb: (j, bb, 0, 0)),
            pl.BlockSpec((1, 1, 2 * width), lambda j, bb: (j // 2, 0, 0)),
            pl.BlockSpec((1, 2 * width, w1s.shape[2]), lambda j, bb: (j // 2, 0, 0)),
            pl.BlockSpec((1, w2s.shape[1], 64), lambda j, bb: (j // 2, 0, 0)),
        ],
        out_specs=pl.BlockSpec((1, 1, n_rows, 64), lambda j, bb: (j, bb, 0, 0)),
        out_shape=jax.ShapeDtypeStruct((four, b, n_rows, 64), BF16),
        compiler_params=_params(("parallel", "parallel")),
        name="compress",
    )(r4, pe2, w1s, w2s)


def _bucket_breakpoints():
    n = np.arange(0, 4 * REL_MAX_DIST)
    max_exact = REL_BUCKETS // 2
    nf = np.maximum(n, max_exact).astype(np.float32)
    large = max_exact + (np.log(nf / max_exact) / np.float32(math.log(REL_MAX_DIST / max_exact))
                         * (REL_BUCKETS - max_exact)).astype(np.int32)
    large = np.minimum(large, REL_BUCKETS - 1)
    bucket = np.where(n < max_exact, n, large)
    return [int(np.argmax(bucket >= j)) for j in range(1, REL_BUCKETS)]


_BREAKS = _bucket_breakpoints()


def _bias_of_dist(dist, tab_ref, h):
    v = jnp.full(dist.shape, tab_ref[0, h], F32)
    for j, bp in enumerate(_BREAKS, start=1):
        v = jnp.where(dist >= bp, tab_ref[j, h], v)
    v = (v - tab_ref[REL_BUCKETS - 1, h]) * LOG2E
    return jnp.where(dist >= 0, v, NEG)


WIN_SPAN = WINDOW + ATT_TILE
NEAR_SPAN = 2 * ATT_TILE


def _bias_near_kernel(tab_ref, win_ref, sel_ref):
    h = pl.program_id(0)
    t = ATT_TILE
    for ref, span in ((win_ref, WIN_SPAN), (sel_ref, NEAR_SPAN)):
        r = lax.broadcasted_iota(jnp.int32, (t, span), 0)
        c = lax.broadcasted_iota(jnp.int32, (t, span), 1)
        dist = r - c + (span - t)
        ref[0] = jnp.where(dist < WINDOW, _bias_of_dist(dist, tab_ref, h), NEG)


def _bias_near(rel_table):
    t = ATT_TILE
    return pl.pallas_call(
        _bias_near_kernel,
        grid=(N_HEADS,),
        in_specs=[pl.BlockSpec(memory_space=pltpu.SMEM)],
        out_specs=[pl.BlockSpec((1, t, WIN_SPAN), lambda h: (h, 0, 0)),
                   pl.BlockSpec((1, t, NEAR_SPAN), lambda h: (h, 0, 0))],
        out_shape=[jax.ShapeDtypeStruct((N_HEADS, t, WIN_SPAN), F32),
                   jax.ShapeDtypeStruct((N_HEADS, t, NEAR_SPAN), F32)],
        compiler_params=_params(("arbitrary",)),
        name="bias_near",
    )(rel_table)


def _bias_cmp_kernel(tab_ref, o_ref, *, n_cmp):
    h = pl.program_id(0)
    i = pl.program_id(1)
    tq, nc = o_ref.shape[1], o_ref.shape[2]
    t = lax.broadcasted_iota(jnp.int32, (tq, nc), 0) + i * tq
    c = lax.broadcasted_iota(jnp.int32, (tq, nc), 1)
    dist = t - (c * CMP_STRIDE + CMP_LEN - 1)
    o_ref[0] = jnp.where(c < n_cmp, _bias_of_dist(dist, tab_ref, h), NEG)


def _bias_cmp(rel_table, seq, n_rows, tq):
    return pl.pallas_call(
        functools.partial(_bias_cmp_kernel, n_cmp=n_rows - 1),
        grid=(N_HEADS, seq // tq),
        in_specs=[pl.BlockSpec(memory_space=pltpu.SMEM)],
        out_specs=pl.BlockSpec((1, tq, n_rows), lambda h, i: (h, i, 0)),
        out_shape=jax.ShapeDtypeStruct((N_HEADS, seq, n_rows), F32),
        compiler_params=_params(("arbitrary", "arbitrary")),
        name="bias_cmp",
    )(rel_table)


def _cmp_attn_kernel(q_ref, kc_ref, vc_ref, addc_ref, ovlt_ref, eye_ref, oc_ref, sel_ref):
    i = pl.program_id(1)
    tq = q_ref.shape[1]
    n_sel = ovlt_ref.shape[0]
    kc = kc_ref[0, 0]
    vc = vc_ref[0, 0]
    t_col = lax.broadcasted_iota(jnp.int32, (tq, 1), 0) + i * tq
    row_ok = (t_col >= CMP_LEN - 1).astype(F32)
    psum = None
    for hh in range(GROUP):
        logit = _dot_nt(q_ref[hh], kc) + addc_ref[hh]
        m = jnp.max(logit, axis=-1, keepdims=True)
        e = jnp.exp2(logit - m)
        p = e / jnp.sum(e, axis=-1, keepdims=True)
        oc_ref[:, 64 * hh:64 * (hh + 1)] = _dot(p.astype(BF16), vc) * row_ok
        psum = p if psum is None else psum + p
    psum = psum * row_ok
    hi = psum.astype(BF16)
    lo = (psum - hi.astype(F32)).astype(BF16)
    ovlt = ovlt_ref[...]
    imp = _dot_nt(ovlt, hi) + _dot_nt(ovlt, lo)
    s_idx = lax.broadcasted_iota(jnp.int32, (n_sel, tq), 0)
    t_blk = (lax.broadcasted_iota(jnp.int32, (n_sel, tq), 1) + i * tq) // SEL_LEN
    d_blk = t_blk - s_idx
    local = (d_blk >= 0) & (d_blk < N_LOCAL_BLOCKS)
    v = jnp.where(local, FORCE_LOCAL,
                  jnp.where(s_idx == 0, FORCE_INIT, jnp.where(d_blk >= 0, imp, -1.0)))
    rows = s_idx.astype(F32)
    sel = jnp.zeros((n_sel, tq), F32)
    for _ in range(min(SEL_TOPK, n_sel)):
        m = jnp.max(v, axis=0, keepdims=True)
        first = jnp.min(jnp.where(v == m, rows, float(n_sel)), axis=0, keepdims=True)
        pick = rows == first
        sel = jnp.where(pick, 1.0, sel)
        v = jnp.where(pick, REMOVED, v)
    sel_ref[0] = _dot_nt(eye_ref[...], sel.astype(BF16)).astype(BF16)


def _cmp_attn(q4, cmp4, addc, ovlt, batch, seq):
    tq = min(CMP_TILE, seq)
    nt = seq // tq
    n_rows = cmp4.shape[2]
    n_sel = ovlt.shape[0]
    eye = jnp.eye(tq, dtype=BF16)
    return pl.pallas_call(
        _cmp_attn_kernel,
        grid=(batch * N_KV_HEADS, nt),
        in_specs=[
            pl.BlockSpec((GROUP, tq, 64), lambda g, i: (g % 2, (g // 2) * nt + i, 0)),
            pl.BlockSpec((1, 1, n_rows, 64), lambda g, i: (g % 2, g // 2, 0, 0)),
            pl.BlockSpec((1, 1, n_rows, 64), lambda g, i: (2 + g % 2, g // 2, 0, 0)),
            pl.BlockSpec((GROUP, tq, n_rows), lambda g, i: (g % 2, i, 0)),
            pl.BlockSpec(ovlt.shape, lambda g, i: (0, 0)),
            pl.BlockSpec((tq, tq), lambda g, i: (0, 0)),
        ],
        out_specs=[
            pl.BlockSpec((tq, GROUP * 64), lambda g, i: ((g // 2) * nt + i, g % 2)),
            pl.BlockSpec((1, tq, n_sel), lambda g, i: (g, i, 0)),
        ],
        out_shape=[
            jax.ShapeDtypeStruct((batch * seq, N_HEADS * 64), F32),
            jax.ShapeDtypeStruct((batch * N_KV_HEADS, seq, n_sel), BF16),
        ],
        compiler_params=_params(("parallel", "arbitrary")),
        name="cmp_attn",
    )(q4, cmp4, cmp4, addc, ovlt, eye)


def _softmax_step(qs, k, v1, mask, add, carry):
    tq = qs.shape[0] // GROUP
    s = _dot_nt(qs, k).reshape(GROUP, tq, k.shape[0])
    if mask is not None:
        s = jnp.where(mask[None] > 0.5, s, NEG)
    if add is not None:
        s = s + add
    m_tile = jnp.max(s, axis=-1, keepdims=True)
    if carry is None:
        m_new = m_tile
    else:
        m, acc = carry
        m_new = jnp.maximum(m, m_tile)
    p = jnp.exp2(s - m_new)
    pv = _dot(p.reshape(GROUP * tq, k.shape[0]).astype(BF16), v1).reshape(GROUP, tq, v1.shape[1])
    if carry is None:
        return m_new, pv
    return m_new, jnp.exp2(m - m_new) * acc + pv


FAR_SLAB = 4 * ATT_TILE


def _sel_win_kernel(q_ref, ks_ref, vs_ref, kw_ref, vw_ref, sel_ref, enear_ref, eslab_ref,
                    winadd_ref, seladd_ref, oc_ref, gates_ref, o_ref):
    i = pl.program_id(1)
    tq = q_ref.shape[1]
    qs = q_ref[...].reshape(GROUP * tq, 64)
    sel = sel_ref[0]

    def rows(ref, start, size):
        return ref[0, 0, pl.ds(pl.multiple_of(start, ATT_TILE), size), :]

    def hide_padding(add, n_pad):
        col = lax.broadcasted_iota(jnp.int32, (1, 1, add.shape[2]), 2)
        return jnp.where(col < n_pad, NEG, add)

    w0 = i * tq
    _, acc_w = _softmax_step(qs, rows(kw_ref, w0, WIN_SPAN), rows(vw_ref, w0, WIN_SPAN), None,
                             hide_padding(winadd_ref[...], WINDOW - i * tq), None)

    n0 = w0 + WINDOW - tq
    carry = _softmax_step(qs, rows(ks_ref, n0, NEAR_SPAN), rows(vs_ref, n0, NEAR_SPAN),
                          _dot(sel, enear_ref[i]), hide_padding(seladd_ref[...], tq - i * tq), None)
    far_end = jnp.maximum(i - 1, 0) * tq
    col = lax.broadcasted_iota(jnp.int32, (1, FAR_SLAB), 1)

    def far_slab(j, c):
        start = WINDOW + j * FAR_SLAB
        mask = jnp.where(col < far_end - j * FAR_SLAB, _dot(sel, eslab_ref[j]), 0.0)
        return _softmax_step(qs, rows(ks_ref, start, FAR_SLAB), rows(vs_ref, start, FAR_SLAB), mask, None, c)

    _, acc_s = lax.fori_loop(0, (far_end + FAR_SLAB - 1) // FAR_SLAB, far_slab, carry)
    o_s = acc_s[..., :64] / acc_s[..., 64:65]
    o_w = acc_w[..., :64] / acc_w[..., 64:65]

    g = jax.nn.sigmoid(gates_ref[0])
    for hh in range(GROUP):
        o_c = oc_ref[:, 64 * hh:64 * (hh + 1)]
        o_ref[:, 64 * hh:64 * (hh + 1)] = (g[:, 3 * hh:3 * hh + 1] * o_c
                                            + g[:, 3 * hh + 1:3 * hh + 2] * o_s[hh]
                                            + g[:, 3 * hh + 2:3 * hh + 3] * o_w[hh])


def _sel_win_attn(q4, kk5, vv5, sel, enear, eslab, winadd, seladd, o_c, gates2, batch, seq):
    tq = ATT_TILE
    nt = seq // tq
    n_sel = sel.shape[2]
    padded = kk5.shape[2]

    def kv_spec(arr, base):
        return pl.BlockSpec((1, 1, padded, arr.shape[3]), lambda g, i: (base + g % 2, g // 2, 0, 0))

    whole = lambda arr: pl.BlockSpec(arr.shape, lambda g, i: (0,) * arr.ndim)
    return pl.pallas_call(
        _sel_win_kernel,
        grid=(batch * N_KV_HEADS, nt),
        in_specs=[
            pl.BlockSpec((GROUP, tq, 64), lambda g, i: (g % 2, (g // 2) * nt + i, 0)),
            kv_spec(kk5, 0), kv_spec(vv5, 0), kv_spec(kk5, 2), kv_spec(vv5, 2),
            pl.BlockSpec((1, tq, n_sel), lambda g, i: (g, i, 0)),
            whole(enear), whole(eslab),
            pl.BlockSpec((GROUP, tq, WIN_SPAN), lambda g, i: (g % 2, 0, 0)),
            pl.BlockSpec((GROUP, tq, NEAR_SPAN), lambda g, i: (g % 2, 0, 0)),
            pl.BlockSpec((tq, GROUP * 64), lambda g, i: ((g // 2) * nt + i, g % 2)),
            pl.BlockSpec((1, tq, LANES), lambda g, i: (g % 2, (g // 2) * nt + i, 0)),
        ],
        out_specs=pl.BlockSpec((tq, GROUP * 64), lambda g, i: ((g // 2) * nt + i, g % 2)),
        out_shape=jax.ShapeDtypeStruct((batch * seq, N_HEADS * 64), F32),
        compiler_params=_params(("parallel", "arbitrary")),
        name="sel_win_attn",
    )(q4, kk5, vv5, kk5, vv5, sel, enear, eslab, winadd, seladd, o_c, gates2)


def _rglru_kernel(xrg_ref, xg_ref, cw_ref, cb_ref, wa_ref, ba_ref, wx_ref, bx_ref, lam_ref, gn_ref,
                  o_ref, xs_ref, h_ref):
    step = pl.program_id(1)
    tc = xrg_ref.shape[0]

    @pl.when(step == 0)
    def _():
        xs_ref[0:8, :] = jnp.zeros((8, xs_ref.shape[1]), F32)
        h_ref[...] = jnp.zeros(h_ref.shape, F32)

    xs_ref[8:, :] = xrg_ref[...]
    xc = cb_ref[...] + cw_ref[CONV_WIDTH - 1:CONV_WIDTH, :] * xs_ref[8:, :]
    for k in range(1, CONV_WIDTH):
        xc = xc + cw_ref[CONV_WIDTH - 1 - k:CONV_WIDTH - k, :] * xs_ref[pl.ds(8 - k, tc), :]
    xs_ref[0:8, :] = xs_ref[tc:tc + 8, :]

    xcb = xc.astype(BF16)
    r = jax.nn.sigmoid(_dot(xcb, wa_ref[...]) + ba_ref[...])
    gi = jax.nn.sigmoid(_dot(xcb, wx_ref[...]) + bx_ref[...])
    lam = lam_ref[...]
    softplus_neg = jnp.maximum(-lam, 0.0) + jnp.log1p(jnp.exp(-jnp.abs(lam)))
    log_a = -LRU_C * r * softplus_neg
    a = jnp.exp(log_a)
    b = jnp.sqrt(1.0 - jnp.exp(2.0 * log_a)) * (gi * xc)

    row = lax.broadcasted_iota(jnp.int32, (tc, 1), 0)
    s = 1
    while s < tc:
        valid = row >= s
        b = jnp.where(valid, a * pltpu.roll(b, s, 0) + b, b)
        a = jnp.where(valid, a * pltpu.roll(a, s, 0), a)
        s *= 2
    h = b + a * h_ref[0:1, :]
    h_ref[0:1, :] = h[tc - 1:tc, :]
    o_ref[...] = _rms(h * jax.nn.gelu(xg_ref[...]), gn_ref[...]).astype(BF16)


def _rglru(xrg, cw, cb, wa, ba, wx, bx, lam, gn, batch, seq, tc):
    nt = seq // tc
    w = cw.shape[1]
    vec = pl.BlockSpec((1, w), lambda b, i: (0, 0))
    return pl.pallas_call(
        _rglru_kernel,
        grid=(batch, nt),
        in_specs=[
            pl.BlockSpec((tc, w), lambda b, i: (b * nt + i, 0)),
            pl.BlockSpec((tc, w), lambda b, i: (b * nt + i, 1)),
            pl.BlockSpec((CONV_WIDTH, w), lambda b, i: (0, 0)),
            vec,
            pl.BlockSpec((w, w), lambda b, i: (0, 0)), vec,
            pl.BlockSpec((w, w), lambda b, i: (0, 0)), vec,
            vec, vec,
        ],
        out_specs=pl.BlockSpec((tc, w), lambda b, i: (b * nt + i, 0)),
        out_shape=jax.ShapeDtypeStruct((batch * seq, w), BF16),
        scratch_shapes=[pltpu.VMEM((tc + 8, w), F32), pltpu.VMEM((8, w), F32)],
        compiler_params=_params(("parallel", "arbitrary")),
        name="rglru",
    )(xrg, xrg, cw, cb, wa, ba, wx, bx, lam, gn)


def _out_proj_kernel(a_ref, l_ref, x_ref, ga_ref, woa_ref, wol_ref, gf_ref, wq_ref, sk_ref,
                     h1_ref, xt_ref, st_ref):
    an = _rms(a_ref[...], ga_ref[...]).astype(BF16)
    h1 = x_ref[...] + _dot(an, woa_ref[...]) + _dot(l_ref[...], wol_ref[...])
    h1_ref[...] = h1
    xn = _rms(h1, gf_ref[...])
    xt_ref[...] = xn.T.astype(BF16)
    q = _dot(xn.astype(BF16), wq_ref[...])
    for hp in range(2 * PEER_HEADS):
        qs = q[:, LANES * hp:LANES * (hp + 1)].astype(BF16)
        st_ref[hp] = _dot_nt(sk_ref[hp % 2], qs)


def _out_proj(a_out, l_n, x2, ga, woa, wol, gf, wq, sk, tm):
    n, d = x2.shape
    full = lambda arr: pl.BlockSpec(arr.shape, lambda i: (0,) * arr.ndim)
    return pl.pallas_call(
        _out_proj_kernel,
        grid=(n // tm,),
        in_specs=[
            pl.BlockSpec((tm, a_out.shape[1]), lambda i: (i, 0)),
            pl.BlockSpec((tm, l_n.shape[1]), lambda i: (i, 0)),
            pl.BlockSpec((tm, d), lambda i: (i, 0)),
            full(ga), full(woa), full(wol), full(gf), full(wq), full(sk),
        ],
        out_specs=[
            pl.BlockSpec((tm, d), lambda i: (i, 0)),
            pl.BlockSpec((d, tm), lambda i: (0, i)),
            pl.BlockSpec((2 * PEER_HEADS, PEER_N_KEYS, tm), lambda i: (0, 0, i)),
        ],
        out_shape=[
            jax.ShapeDtypeStruct((n, d), F32),
            jax.ShapeDtypeStruct((d, n), BF16),
            jax.ShapeDtypeStruct((2 * PEER_HEADS, PEER_N_KEYS, n), F32),
        ],
        compiler_params=_params(("parallel",)),
        name="out_proj",
    )(a_out, l_n, x2, ga, woa, wol, gf, wq, sk)


N_TOP = PEER_TOPK
_PAIRS = [(i, j) for i in range(N_TOP) for j in range(N_TOP) if (i + 1) * (j + 1) <= N_TOP]


def _top_values(s):
    n, tl = s.shape
    parts = [s[SUBLANES * k:SUBLANES * (k + 1)] for k in range(n // SUBLANES)]
    base = lax.broadcasted_iota(jnp.int32, (SUBLANES, tl), 0).astype(F32)
    rows = [base + float(SUBLANES * k) for k in range(len(parts))]
    out = []
    for r in range(N_TOP):
        m = jnp.max(functools.reduce(jnp.maximum, parts), axis=0, keepdims=True)
        out.append(m)
        first = jnp.full((SUBLANES, tl), float(n), F32)
        for k in reversed(range(len(parts))):
            first = jnp.where(parts[k] == m, rows[k], first)
        first = jnp.min(first, axis=0, keepdims=True)
        taken = TAKEN * (1.0 + r / TAKEN_STEPS)
        parts = [jnp.where(rows[k] == first, taken, parts[k]) for k in range(len(parts))]
    rank = jnp.concatenate(parts, axis=0)
    rank = jnp.where(rank <= TAKEN, (rank * (1.0 / TAKEN) - 1.0) * TAKEN_STEPS, float(PEER_TOPK))
    return out, rank


def _peer_stats_kernel(st_ref, r2_ref, e2_ref, l_ref, e1_ref):
    tops = [[], []]
    rank1 = []
    for h in range(PEER_HEADS):
        vals, rank = _top_values(st_ref[2 * h])
        tops[0].append(vals)
        rank1.append(rank)
        vals, rank = _top_values(st_ref[2 * h + 1])
        tops[1].append(vals)
        r2_ref[h] = rank.astype(BF16)
    a = [jnp.concatenate([tops[0][h][i] for h in range(PEER_HEADS)], axis=0) for i in range(N_TOP)]
    b = [jnp.concatenate([tops[1][h][j] for h in range(PEER_HEADS)], axis=0) for j in range(N_TOP)]
    cand = [a[i] + b[j] for (i, j) in _PAIRS]
    work = list(cand)
    kth = []
    for _ in range(N_TOP):
        m = functools.reduce(jnp.maximum, work)
        kth.append(m)
        found = jnp.zeros(m.shape, jnp.bool_)
        for k in range(len(work)):
            hit = (work[k] == m) & jnp.logical_not(found)
            work[k] = jnp.where(hit, REMOVED, work[k])
            found = found | hit
    thr = kth[PEER_TOPK - 1]
    top = a[0] + b[0]
    z = functools.reduce(lambda u, w: u + w, [jnp.where(c >= thr, jnp.exp(c - top), 0.0) for c in cand])
    n_sel = [functools.reduce(lambda u, w: u + w,
                              [(a[r] + b[j] >= thr).astype(F32) for j in range(PEER_TOPK)])
             for r in range(PEER_TOPK)]
    for h in range(PEER_HEADS):
        row = slice(h, h + 1)
        prefix = jnp.zeros(rank1[h].shape, F32)
        for r in range(PEER_TOPK):
            prefix = jnp.where(rank1[h] == float(r), n_sel[r][row], prefix)
        l_ref[h] = prefix
        e1_ref[h] = jnp.exp(st_ref[2 * h] - a[0][row])
        e2_ref[h] = (jnp.exp(st_ref[2 * h + 1] - b[0][row]) / z[row]).astype(BF16)


def _peer_stats(st, tl):
    hp, nk, n = st.shape
    out_spec = pl.BlockSpec((PEER_HEADS, nk, tl), lambda i: (0, 0, i))
    return pl.pallas_call(
        _peer_stats_kernel,
        grid=(n // tl,),
        in_specs=[pl.BlockSpec((hp, nk, tl), lambda i: (0, 0, i))],
        out_specs=[out_spec] * 4,
        out_shape=[
            jax.ShapeDtypeStruct((PEER_HEADS, nk, n), BF16),
            jax.ShapeDtypeStruct((PEER_HEADS, nk, n), BF16),
            jax.ShapeDtypeStruct((PEER_HEADS, nk, n), F32),
            jax.ShapeDtypeStruct((PEER_HEADS, nk, n), F32),
        ],
        compiler_params=_params(("parallel",)),
        name="peer_stats",
    )(st)


def _peer_main_kernel(u_ref, vt_ref, xt_ref, r2_ref, e2_ref, l_ref, e1_ref, h1_ref, o_ref, acc_ref):
    c = pl.program_id(1)
    ce = u_ref.shape[0]
    tm = xt_ref.shape[1]
    nk = PEER_N_KEYS

    @pl.when(c == 0)
    def _():
        acc_ref[...] = jnp.zeros(acc_ref.shape, F32)

    act = _gelu_tanh(_dot(u_ref[...], xt_ref[...]))

    def rows_bf16(ref, h, i1):
        packed = jnp.broadcast_to(ref[h, pl.ds(i1, 1), :], (BF16_ROWS, tm)).astype(BF16)
        return jnp.concatenate([packed] * (nk // BF16_ROWS), axis=0)

    w_parts = []
    for ii in range(ce // nk):
        i1 = c * (ce // nk) + ii
        g = None
        for h in range(PEER_HEADS):
            e2 = e2_ref[h]
            term = jnp.where(r2_ref[h] < rows_bf16(l_ref, h, i1), e2, jnp.zeros_like(e2)) * rows_bf16(e1_ref, h, i1)
            g = term if g is None else g + term
        w_parts.append(act[ii * nk:(ii + 1) * nk, :].astype(BF16) * g)
    w = jnp.concatenate(w_parts, axis=0)
    acc_ref[...] += _dot(vt_ref[...], w)

    @pl.when(c == pl.num_programs(1) - 1)
    def _():
        o_ref[...] = h1_ref[...] + acc_ref[...].T


def _peer_main(u_b, vt_b, xt, r2, e2, n_sel, e1, h1, tm, ce):
    ne, d = u_b.shape
    n = xt.shape[1]
    per_tok = pl.BlockSpec((PEER_HEADS, PEER_N_KEYS, tm), lambda t, c: (0, 0, t))
    return pl.pallas_call(
        _peer_main_kernel,
        grid=(n // tm, ne // ce),
        in_specs=[
            pl.BlockSpec((ce, d), lambda t, c: (c, 0)),
            pl.BlockSpec((d, ce), lambda t, c: (0, c)),
            pl.BlockSpec((d, tm), lambda t, c: (0, t)),
            per_tok, per_tok, per_tok, per_tok,
            pl.BlockSpec((tm, d), lambda t, c: (t, 0)),
        ],
        out_specs=pl.BlockSpec((tm, d), lambda t, c: (t, 0)),
        out_shape=jax.ShapeDtypeStruct((n, d), F32),
        scratch_shapes=[pltpu.VMEM((d, tm), F32)],
        compiler_params=_params(("parallel", "arbitrary")),
        name="peer_main",
    )(u_b, vt_b, xt, r2, e2, n_sel, e1, h1)


def _ple_final_kernel(h_ref, p_ref, gp_ref, wg_ref, bg_ref, wp_ref, gfin_ref, o_ref):
    h = h_ref[...]
    gate = jax.nn.sigmoid(_dot(_rms(h, gp_ref[...]).astype(BF16), wg_ref[...]) + bg_ref[...])
    h = h + gate * _dot(p_ref[...].astype(BF16), wp_ref[...])
    o_ref[...] = _rms(h, gfin_ref[...])


def _ple_final(h2, p2, gp, wg, bg, wp, gfin, tm):
    n, d = h2.shape
    full = lambda arr: pl.BlockSpec(arr.shape, lambda i: (0,) * arr.ndim)
    return pl.pallas_call(
        _ple_final_kernel,
        grid=(n // tm,),
        in_specs=[
            pl.BlockSpec((tm, d), lambda i: (i, 0)),
            pl.BlockSpec((tm, p2.shape[1]), lambda i: (i, 0)),
            full(gp), full(wg), full(bg), full(wp), full(gfin),
        ],
        out_specs=pl.BlockSpec((tm, d), lambda i: (i, 0)),
        out_shape=jax.ShapeDtypeStruct((n, d), F32),
        compiler_params=_params(("parallel",)),
        name="ple_final",
    )(h2, p2, gp, wg, bg, wp, gfin)


def _overlap_t(seq, n_rows):
    n_cmp = n_rows - 1
    cmp_start = np.arange(n_cmp) * CMP_STRIDE
    sel_start = np.arange(seq // SEL_LEN) * SEL_LEN
    ov = np.clip(np.minimum(cmp_start[:, None] + CMP_LEN, sel_start[None, :] + SEL_LEN)
                 - np.maximum(cmp_start[:, None], sel_start[None, :]), 0, None).astype(np.float32) / CMP_LEN
    out = np.zeros((seq // SEL_LEN, n_rows), np.float32)
    out[:, :n_cmp] = ov.T
    return out


def _block_expand(seq, starts, width):
    pos = np.asarray(starts).reshape(-1, 1, 1) + np.arange(width).reshape(1, 1, width)
    blk = np.where(pos >= 0, pos // SEL_LEN, -1)
    return (blk == np.arange(seq // SEL_LEN).reshape(1, -1, 1)).astype(np.float32)


def _block_diag(w):
    nb, bd, _ = w.shape
    out = jnp.zeros((nb * bd, nb * bd), w.dtype)
    for j in range(nb):
        out = out.at[j * bd:(j + 1) * bd, j * bd:(j + 1) * bd].set(w[j])
    return out


def _layer(h, p_i, attn_norm, w_in, cmp_k_pe, cmp_k_w1, cmp_k_w2, cmp_v_pe, cmp_v_w1, cmp_v_w2,
           rel_table, conv_w, conv_b, lru_wa, lru_ba, lru_wx, lru_bx, lru_lambda,
           grp_norm_attn, grp_norm_lru, w_out, ffn_norm, peer_wq, peer_subkeys, peer_u, peer_v,
           ple_norm, ple_wgate, ple_bgate, ple_proj, *, tm, tc, tl, ce):
    batch, seq, d = h.shape
    n = batch * seq
    x2 = h.reshape(n, d)
    aw = N_HEADS * HEAD_DIM
    kvw = N_KV_HEADS * HEAD_DIM
    lw = d - aw

    o = 0
    cols = {}
    for name, width in (("q", aw), ("kc", kvw), ("vc", kvw), ("ks", kvw), ("vs", kvw), ("kw", kvw),
                        ("vw", kvw), ("gates", N_HEADS * 3), ("xr", lw), ("xg", lw)):
        cols[name] = w_in[:, o:o + width]
        o += width
    gate_pad = [jnp.pad(cols["gates"][:, N_GATE_COLS * k:N_GATE_COLS * (k + 1)], ((0, 0), (0, LANES - N_GATE_COLS)))
                for k in range(N_KV_HEADS)]
    wf = jnp.concatenate([cols["xr"], cols["xg"], cols["kc"], cols["vc"]] + gate_pad, axis=1).astype(BF16)
    wb = jnp.concatenate([cols["q"], cols["ks"], cols["kw"], cols["vs"], cols["vw"]], axis=1).astype(BF16)

    xrg, kvc, gates2, q4, kk, vv = _in_proj(x2, attn_norm.reshape(1, d), wf, wb, tm)

    n_rows = seq // CMP_STRIDE
    r4 = kvc.reshape(4, batch, n_rows, CMP_STRIDE * HEAD_DIM)
    pe2 = jnp.stack([cmp_k_pe.reshape(1, -1), cmp_v_pe.reshape(1, -1)])
    w1s = jnp.stack([cmp_k_w1, cmp_v_w1]).astype(BF16)
    w2s = jnp.stack([cmp_k_w2, cmp_v_w2]).astype(BF16)
    cmp4 = _compress(r4, pe2, w1s, w2s)

    winadd, seladd = _bias_near(rel_table)
    addc = _bias_cmp(rel_table, seq, n_rows, min(seq, 512))
    ovlt = jnp.asarray(_overlap_t(seq, n_rows), BF16)
    o_c, sel = _cmp_attn(q4, cmp4, addc, ovlt, batch, seq)

    nt = seq // ATT_TILE
    tile_starts = np.arange(nt) * ATT_TILE
    enear = jnp.asarray(_block_expand(seq, tile_starts - ATT_TILE, NEAR_SPAN), BF16)
    eslab = jnp.asarray(_block_expand(seq, np.arange(max(seq // FAR_SLAB, 1)) * FAR_SLAB, FAR_SLAB), BF16)
    front = ((0, 0), (0, 0), (WINDOW, 0), (0, 0))
    kk5 = jnp.pad(kk.reshape(4, batch, seq, HEAD_DIM), front)
    vv5 = jnp.pad(vv.reshape(4, batch, seq, LANES), front)
    a_out = _sel_win_attn(q4, kk5, vv5, sel, enear, eslab, winadd, seladd, o_c, gates2, batch, seq)

    l_n = _rglru(xrg, conv_w.reshape(CONV_WIDTH, lw), conv_b.reshape(1, lw),
                 _block_diag(lru_wa).astype(BF16), lru_ba.reshape(1, lw),
                 _block_diag(lru_wx).astype(BF16), lru_bx.reshape(1, lw),
                 lru_lambda.reshape(1, lw), grp_norm_lru.reshape(1, lw), batch, seq, tc)

    w_out_b = w_out.astype(BF16)
    h1, xt, st = _out_proj(a_out, l_n, x2, grp_norm_attn.reshape(1, aw), w_out_b[:aw], w_out_b[aw:],
                           ffn_norm.reshape(1, d), peer_wq.astype(BF16), peer_subkeys.astype(BF16), tm)
    r2, e2, n_sel, e1 = _peer_stats(st, tl)
    h2 = _peer_main(peer_u.astype(BF16), peer_v.T.astype(BF16), xt, r2, e2, n_sel, e1, h1, tm, ce)
    ple_args = (h2, p_i.reshape(n, -1), ple_norm.reshape(1, d), ple_wgate.astype(BF16), ple_bgate.reshape(1, d),
                ple_proj.astype(BF16))
    return ple_args, dict(a_out=a_out, l_n=l_n, h1=h1)


def kernel(x, p, attn_norm, w_in, cmp_k_pe, cmp_k_w1, cmp_k_w2, cmp_v_pe, cmp_v_w1, cmp_v_w2, rel_table, conv_w, conv_b, lru_wa, lru_ba, lru_wx, lru_bx, lru_lambda, grp_norm_attn, grp_norm_lru, w_out, ffn_norm, peer_wq, peer_subkeys, peer_u, peer_v, ple_norm, ple_wgate, ple_bgate, ple_proj, final_norm):
    batch, seq, d = x.shape
    assert w_in.shape[0] == 1, "single-layer stack only"
    tm = min(512, seq)
    (h2, p2, gp, wg, bg, wp), _ = _layer(
        x, p[0], attn_norm[0], w_in[0], cmp_k_pe[0], cmp_k_w1[0], cmp_k_w2[0], cmp_v_pe[0], cmp_v_w1[0],
        cmp_v_w2[0], rel_table, conv_w[0], conv_b[0], lru_wa[0], lru_ba[0], lru_wx[0], lru_bx[0],
        lru_lambda[0], grp_norm_attn[0], grp_norm_lru[0], w_out[0], ffn_norm[0], peer_wq[0],
        peer_subkeys[0], peer_u[0], peer_v[0], ple_norm[0], ple_wgate[0], ple_bgate[0], ple_proj[0],
        tm=tm, tc=min(256, seq), tl=min(256, seq), ce=2048)
    out = _ple_final(h2, p2, gp, wg, bg, wp, final_norm.reshape(1, d), tm)
    return out.reshape(batch, seq, d)
```

```python
import functools
import math

import numpy as np
import jax
import jax.numpy as jnp
from jax import lax
from jax.experimental import pallas as pl
from jax.experimental.pallas import tpu as pltpu

F32 = jnp.float32
BF16 = jnp.bfloat16

N_HEADS = 8
HEAD_DIM = 64
N_KV_HEADS = 2
GROUP = N_HEADS // N_KV_HEADS
LRU_C = 8.0
CONV_WIDTH = 4
CMP_LEN = 32
CMP_STRIDE = 16
SEL_LEN = 64
SEL_TOPK = 16
N_LOCAL_BLOCKS = 2
WINDOW = 512
FORCE_LOCAL = 2.0e4
FORCE_INIT = 1.0e4
REL_BUCKETS = 32
REL_MAX_DIST = 128
PEER_HEADS = 8
PEER_N_KEYS = 128
PEER_TOPK = 16
EPS = 1e-6
NEG = -1e30
LOG2E = math.log2(math.e)
Q_SCALE = HEAD_DIM ** -0.5 * LOG2E
REMOVED = -3.0e38
TAKEN = -2.0 ** 120
TAKEN_STEPS = 32.0

LANES = 128
SUBLANES = 8
BF16_ROWS = 16
VMEM_LIMIT = 56 * 1024 * 1024

ATT_TILE = 256
CMP_TILE = 512
N_GATE_COLS = GROUP * 3


def _dot(a, b):
    return jnp.dot(a, b, preferred_element_type=F32)


def _dot_nt(a, b):
    return lax.dot_general(a, b, (((1,), (1,)), ((), ())), preferred_element_type=F32)


def _gelu_tanh(x):
    c = math.sqrt(2.0 / math.pi)
    half = 0.5 * x
    return half + half * jnp.tanh(x * (c + (c * 0.044715) * (x * x)))


def _rms(x, g):
    return x * lax.rsqrt(jnp.mean(x * x, axis=-1, keepdims=True) + EPS) * g


def _params(sem, vmem=VMEM_LIMIT):
    return pltpu.CompilerParams(dimension_semantics=sem, vmem_limit_bytes=vmem)


def _in_proj_kernel(x_ref, g_ref, wf_ref, wb_ref, xrg_ref, kvc_ref, gates_ref, q4_ref, kk_ref, vv_ref):
    dh = HEAD_DIM
    n_kv = 2 * N_KV_HEADS
    xb = _rms(x_ref[...], g_ref[...]).astype(BF16)
    yf = _dot(xb, wf_ref[...])
    kv0 = xrg_ref.shape[1]
    g0 = kv0 + n_kv * dh
    xrg_ref[...] = yf[:, :kv0]
    for j in range(n_kv):
        kvc_ref[j] = yf[:, kv0 + dh * j:kv0 + dh * (j + 1)]
    for k in range(N_KV_HEADS):
        gates_ref[k] = yf[:, g0 + LANES * k:g0 + LANES * (k + 1)]
    yb = _dot(xb, wb_ref[...])
    for h in range(N_HEADS):
        q4_ref[h] = (yb[:, dh * h:dh * (h + 1)] * Q_SCALE).astype(BF16)
    k0 = N_HEADS * dh
    v0 = k0 + n_kv * dh
    ones_col = (lax.broadcasted_iota(jnp.int32, (yb.shape[0], LANES - dh), 1) == 0).astype(F32)
    for j in range(n_kv):
        kk_ref[j] = yb[:, k0 + dh * j:k0 + dh * (j + 1)].astype(BF16)
        vv_ref[j] = jnp.concatenate([yb[:, v0 + dh * j:v0 + dh * (j + 1)], ones_col], axis=1).astype(BF16)


def _in_proj(x2, g, wf, wb, tm):
    n, d = x2.shape
    dh = HEAD_DIM
    n_kv = 2 * N_KV_HEADS
    lru2 = wf.shape[1] - n_kv * dh - N_KV_HEADS * LANES
    shapes = [
        ((n, lru2), F32),
        ((n_kv, n, dh), F32),
        ((N_KV_HEADS, n, LANES), F32),
        ((N_HEADS, n, dh), BF16),
        ((n_kv, n, dh), BF16),
        ((n_kv, n, LANES), BF16),
    ]

    def spec(shape):
        if len(shape) == 2:
            return pl.BlockSpec((tm, shape[1]), lambda i: (i, 0))
        return pl.BlockSpec((shape[0], tm, shape[2]), lambda i: (0, i, 0))

    return pl.pallas_call(
        _in_proj_kernel,
        grid=(n // tm,),
        in_specs=[
            pl.BlockSpec((tm, d), lambda i: (i, 0)),
            pl.BlockSpec((1, d), lambda i: (0, 0)),
            pl.BlockSpec(wf.shape, lambda i: (0, 0)),
            pl.BlockSpec(wb.shape, lambda i: (0, 0)),
        ],
        out_specs=[spec(shape) for shape, _ in shapes],
        out_shape=[jax.ShapeDtypeStruct(shape, dtype) for shape, dtype in shapes],
        compiler_params=_params(("parallel",)),
        name="in_proj",
    )(x2, g, wf, wb)


def _compress_kernel(r_ref, pe_ref, w1_ref, w2_ref, o_ref):
    r = r_ref[0, 0]
    n_rows = r.shape[0]
    half = r.shape[1]
    pe = pe_ref[0]
    top = _dot((r + pe[:, :half]).astype(BF16), w1_ref[0, :half, :])
    bot = _dot((r + pe[:, half:]).astype(BF16), w1_ref[0, half:, :])
    hid = top + pltpu.roll(bot, n_rows - 1, 0)
    act = jax.nn.gelu(hid)
    out = _dot(act.astype(BF16), w2_ref[0])
    row = lax.broadcasted_iota(jnp.int32, out.shape, 0)
    o_ref[0, 0] = jnp.where(row < n_rows - 1, out, 0.0).astype(BF16)


def _compress(r4, pe2, w1s, w2s):
    four, b, n_rows, width = r4.shape
    return pl.pallas_call(
        _compress_kernel,
        grid=(four, b),
        in_specs=[
            pl.BlockSpec((1, 1, n_rows, width), lambda j, bb: (j, bb, 0, 0)),
            pl.BlockSpec((1, 1, 2 * width), lambda j, bb: (j // 2, 0, 0)),
            pl.BlockSpec((1, 2 * width, w1s.shape[2]), lambda j, bb: (j // 2, 0, 0)),
            pl.BlockSpec((1, w2s.shape[1], 64), lambda j, bb: (j // 2, 0, 0)),
        ],
        out_specs=pl.BlockSpec((1, 1, n_rows, 64), lambda j, bb: (j, bb, 0, 0)),
        out_shape=jax.ShapeDtypeStruct((four, b, n_rows, 64), BF16),
        compiler_params=_params(("parallel", "parallel")),
        name="compress",
    )(r4, pe2, w1s, w2s)


def _bucket_breakpoints():
    n = np.arange(0, 4 * REL_MAX_DIST)
    max_exact = REL_BUCKETS // 2
    nf = np.maximum(n, max_exact).astype(np.float32)
    large = max_exact + (np.log(nf / max_exact) / np.float32(math.log(REL_MAX_DIST / max_exact))
                         * (REL_BUCKETS - max_exact)).astype(np.int32)
    large = np.minimum(large, REL_BUCKETS - 1)
    bucket = np.where(n < max_exact, n, large)
    return [int(np.argmax(bucket >= j)) for j in range(1, REL_BUCKETS)]


_BREAKS = _bucket_breakpoints()


def _bias_of_dist(dist, tab_ref, h):
    v = jnp.full(dist.shape, tab_ref[0, h], F32)
    for j, bp in enumerate(_BREAKS, start=1):
        v = jnp.where(dist >= bp, tab_ref[j, h], v)
    v = (v - tab_ref[REL_BUCKETS - 1, h]) * LOG2E
    return jnp.where(dist >= 0, v, NEG)


WIN_SPAN = WINDOW + ATT_TILE
NEAR_SPAN = 2 * ATT_TILE


def _bias_near_kernel(tab_ref, win_ref, sel_ref):
    h = pl.program_id(0)
    t = ATT_TILE
    for ref, span in ((win_ref, WIN_SPAN), (sel_ref, NEAR_SPAN)):
        r = lax.broadcasted_iota(jnp.int32, (t, span), 0)
        c = lax.broadcasted_iota(jnp.int32, (t, span), 1)
        dist = r - c + (span - t)
        ref[0] = jnp.where(dist < WINDOW, _bias_of_dist(dist, tab_ref, h), NEG)


def _bias_near(rel_table):
    t = ATT_TILE
    return pl.pallas_call(
        _bias_near_kernel,
        grid=(N_HEADS,),
        in_specs=[pl.BlockSpec(memory_space=pltpu.SMEM)],
        out_specs=[pl.BlockSpec((1, t, WIN_SPAN), lambda h: (h, 0, 0)),
                   pl.BlockSpec((1, t, NEAR_SPAN), lambda h: (h, 0, 0))],
        out_shape=[jax.ShapeDtypeStruct((N_HEADS, t, WIN_SPAN), F32),
                   jax.ShapeDtypeStruct((N_HEADS, t, NEAR_SPAN), F32)],
        compiler_params=_params(("arbitrary",)),
        name="bias_near",
    )(rel_table)


def _bias_cmp_kernel(tab_ref, o_ref, *, n_cmp):
    h = pl.program_id(0)
    i = pl.program_id(1)
    tq, nc = o_ref.shape[1], o_ref.shape[2]
    span = 2 * nc
    b = lax.broadcasted_iota(jnp.int32, (CMP_STRIDE, span), 0)
    u = lax.broadcasted_iota(jnp.int32, (CMP_STRIDE, span), 1)
    table = _bias_of_dist(CMP_STRIDE * (nc - 1 - u) + b - (CMP_LEN - 1), tab_ref, h)
    col = lax.broadcasted_iota(jnp.int32, (CMP_STRIDE, nc), 1)
    groups = tq // CMP_STRIDE
    for k in range(groups):
        a = i * groups + k
        win = pltpu.roll(table, (a + span - (nc - 1)) % span, 1)[:, :nc]
        o_ref[0, CMP_STRIDE * k:CMP_STRIDE * (k + 1), :] = jnp.where(col < n_cmp, win, NEG)


def _bias_cmp(rel_table, seq, n_rows, tq):
    return pl.pallas_call(
        functools.partial(_bias_cmp_kernel, n_cmp=n_rows - 1),
        grid=(N_HEADS, seq // tq),
        in_specs=[pl.BlockSpec(memory_space=pltpu.SMEM)],
        out_specs=pl.BlockSpec((1, tq, n_rows), lambda h, i: (h, i, 0)),
        out_shape=jax.ShapeDtypeStruct((N_HEADS, seq, n_rows), F32),
        compiler_params=_params(("arbitrary", "arbitrary")),
        name="bias_cmp",
    )(rel_table)


def _cmp_attn_kernel(q_ref, kc_ref, vc_ref, addc_ref, ovlt_ref, eye_ref, oc_ref, sel_ref):
    i = pl.program_id(1)
    tq = q_ref.shape[1]
    n_sel = ovlt_ref.shape[0]
    kc = kc_ref[0, 0]
    vc = vc_ref[0, 0]
    t_col = lax.broadcasted_iota(jnp.int32, (tq, 1), 0) + i * tq
    row_ok = (t_col >= CMP_LEN - 1).astype(F32)
    psum = None
    for hh in range(GROUP):
        logit = _dot_nt(q_ref[hh], kc) + addc_ref[hh]
        m = jnp.max(logit, axis=-1, keepdims=True)
        e = jnp.exp2(logit - m)
        p = e / jnp.sum(e, axis=-1, keepdims=True)
        oc_ref[:, 64 * hh:64 * (hh + 1)] = _dot(p.astype(BF16), vc) * row_ok
        psum = p if psum is None else psum + p
    psum = psum * row_ok
    hi = psum.astype(BF16)
    lo = (psum - hi.astype(F32)).astype(BF16)
    ovlt = ovlt_ref[...]
    imp = _dot_nt(ovlt, hi) + _dot_nt(ovlt, lo)
    s_idx = lax.broadcasted_iota(jnp.int32, (n_sel, tq), 0)
    t_blk = (lax.broadcasted_iota(jnp.int32, (n_sel, tq), 1) + i * tq) // SEL_LEN
    d_blk = t_blk - s_idx
    local = (d_blk >= 0) & (d_blk < N_LOCAL_BLOCKS)
    v = jnp.where(local, FORCE_LOCAL,
                  jnp.where(s_idx == 0, FORCE_INIT, jnp.where(d_blk >= 0, imp, -1.0)))
    rows = s_idx.astype(F32)
    sel = jnp.zeros((n_sel, tq), F32)
    for _ in range(min(SEL_TOPK, n_sel)):
        m = jnp.max(v, axis=0, keepdims=True)
        first = jnp.min(jnp.where(v == m, rows, float(n_sel)), axis=0, keepdims=True)
        pick = rows == first
        sel = jnp.where(pick, 1.0, sel)
        v = jnp.where(pick, REMOVED, v)
    sel_ref[0] = _dot_nt(eye_ref[...], sel.astype(BF16)).astype(BF16)


def _cmp_attn(q4, cmp4, addc, ovlt, batch, seq):
    tq = min(CMP_TILE, seq)
    nt = seq // tq
    n_rows = cmp4.shape[2]
    n_sel = ovlt.shape[0]
    eye = jnp.eye(tq, dtype=BF16)
    return pl.pallas_call(
        _cmp_attn_kernel,
        grid=(batch * N_KV_HEADS, nt),
        in_specs=[
            pl.BlockSpec((GROUP, tq, 64), lambda g, i: (g % 2, (g // 2) * nt + i, 0)),
            pl.BlockSpec((1, 1, n_rows, 64), lambda g, i: (g % 2, g // 2, 0, 0)),
            pl.BlockSpec((1, 1, n_rows, 64), lambda g, i: (2 + g % 2, g // 2, 0, 0)),
            pl.BlockSpec((GROUP, tq, n_rows), lambda g, i: (g % 2, i, 0)),
            pl.BlockSpec(ovlt.shape, lambda g, i: (0, 0)),
            pl.BlockSpec((tq, tq), lambda g, i: (0, 0)),
        ],
        out_specs=[
            pl.BlockSpec((tq, GROUP * 64), lambda g, i: ((g // 2) * nt + i, g % 2)),
            pl.BlockSpec((1, tq, n_sel), lambda g, i: (g, i, 0)),
        ],
        out_shape=[
            jax.ShapeDtypeStruct((batch * seq, N_HEADS * 64), F32),
            jax.ShapeDtypeStruct((batch * N_KV_HEADS, seq, n_sel), BF16),
        ],
        compiler_params=_params(("parallel", "arbitrary")),
        name="cmp_attn",
    )(q4, cmp4, cmp4, addc, ovlt, eye)


def _softmax_step(qs, k, v1, mask, add, carry):
    tq = qs.shape[0] // GROUP
    s = _dot_nt(qs, k).reshape(GROUP, tq, k.shape[0])
    if mask is not None:
        s = jnp.where(mask[None] > 0.5, s, NEG)
    if add is not None:
        s = s + add
    m_tile = jnp.max(s, axis=-1, keepdims=True)
    if carry is None:
        m_new = m_tile
    else:
        m, acc = carry
        m_new = jnp.maximum(m, m_tile)
    p = jnp.exp2(s - m_new)
    pv = _dot(p.reshape(GROUP * tq, k.shape[0]).astype(BF16), v1).reshape(GROUP, tq, v1.shape[1])
    if carry is None:
        return m_new, pv
    return m_new, jnp.exp2(m - m_new) * acc + pv


FAR_SLAB = 4 * ATT_TILE


def _sel_win_kernel(q_ref, ks_ref, vs_ref, kw_ref, vw_ref, sel_ref, enear_ref, eslab_ref,
                    winadd_ref, seladd_ref, oc_ref, gates_ref, o_ref):
    i = pl.program_id(1)
    tq = q_ref.shape[1]
    qs = q_ref[...].reshape(GROUP * tq, 64)
    sel = sel_ref[0]

    def rows(ref, start, size):
        return ref[0, 0, pl.ds(pl.multiple_of(start, ATT_TILE), size), :]

    def hide_padding(add, n_pad):
        col = lax.broadcasted_iota(jnp.int32, (1, 1, add.shape[2]), 2)
        return jnp.where(col < n_pad, NEG, add)

    w0 = i * tq
    _, acc_w = _softmax_step(qs, rows(kw_ref, w0, WIN_SPAN), rows(vw_ref, w0, WIN_SPAN), None,
                             hide_padding(winadd_ref[...], WINDOW - i * tq), None)

    n0 = w0 + WINDOW - tq
    carry = _softmax_step(qs, rows(ks_ref, n0, NEAR_SPAN), rows(vs_ref, n0, NEAR_SPAN),
                          _dot(sel, enear_ref[i]), hide_padding(seladd_ref[...], tq - i * tq), None)
    far_end = jnp.maximum(i - 1, 0) * tq
    col = lax.broadcasted_iota(jnp.int32, (1, FAR_SLAB), 1)

    def far_slab(j, c):
        start = WINDOW + j * FAR_SLAB
        mask = jnp.where(col < far_end - j * FAR_SLAB, _dot(sel, eslab_ref[j]), 0.0)
        return _softmax_step(qs, rows(ks_ref, start, FAR_SLAB), rows(vs_ref, start, FAR_SLAB), mask, None, c)

    _, acc_s = lax.fori_loop(0, (far_end + FAR_SLAB - 1) // FAR_SLAB, far_slab, carry)
    o_s = acc_s[..., :64] / acc_s[..., 64:65]
    o_w = acc_w[..., :64] / acc_w[..., 64:65]

    g = jax.nn.sigmoid(gates_ref[0])
    for hh in range(GROUP):
        o_c = oc_ref[:, 64 * hh:64 * (hh + 1)]
        o_ref[:, 64 * hh:64 * (hh + 1)] = (g[:, 3 * hh:3 * hh + 1] * o_c
                                            + g[:, 3 * hh + 1:3 * hh + 2] * o_s[hh]
                                            + g[:, 3 * hh + 2:3 * hh + 3] * o_w[hh])


def _sel_win_attn(q4, kk5, vv5, sel, enear, eslab, winadd, seladd, o_c, gates2, batch, seq):
    tq = ATT_TILE
    nt = seq // tq
    n_sel = sel.shape[2]
    padded = kk5.shape[2]

    def kv_spec(arr, base):
        return pl.BlockSpec((1, 1, padded, arr.shape[3]), lambda g, i: (base + g % 2, g // 2, 0, 0))

    whole = lambda arr: pl.BlockSpec(arr.shape, lambda g, i: (0,) * arr.ndim)
    return pl.pallas_call(
        _sel_win_kernel,
        grid=(batch * N_KV_HEADS, nt),
        in_specs=[
            pl.BlockSpec((GROUP, tq, 64), lambda g, i: (g % 2, (g // 2) * nt + i, 0)),
            kv_spec(kk5, 0), kv_spec(vv5, 0), kv_spec(kk5, 2), kv_spec(vv5, 2),
            pl.BlockSpec((1, tq, n_sel), lambda g, i: (g, i, 0)),
            whole(enear), whole(eslab),
            pl.BlockSpec((GROUP, tq, WIN_SPAN), lambda g, i: (g % 2, 0, 0)),
            pl.BlockSpec((GROUP, tq, NEAR_SPAN), lambda g, i: (g % 2, 0, 0)),
            pl.BlockSpec((tq, GROUP * 64), lambda g, i: ((g // 2) * nt + i, g % 2)),
            pl.BlockSpec((1, tq, LANES), lambda g, i: (g % 2, (g // 2) * nt + i, 0)),
        ],
        out_specs=pl.BlockSpec((tq, GROUP * 64), lambda g, i: ((g // 2) * nt + i, g % 2)),
        out_shape=jax.ShapeDtypeStruct((batch * seq, N_HEADS * 64), F32),
        compiler_params=_params(("parallel", "arbitrary")),
        name="sel_win_attn",
    )(q4, kk5, vv5, kk5, vv5, sel, enear, eslab, winadd, seladd, o_c, gates2)


def _rglru_kernel(xrg_ref, xg_ref, cw_ref, cb_ref, wa_ref, ba_ref, wx_ref, bx_ref, lam_ref, gn_ref,
                  o_ref, xs_ref, h_ref):
    step = pl.program_id(1)
    tc = xrg_ref.shape[0]

    @pl.when(step == 0)
    def _():
        xs_ref[0:8, :] = jnp.zeros((8, xs_ref.shape[1]), F32)
        h_ref[...] = jnp.zeros(h_ref.shape, F32)

    xs_ref[8:, :] = xrg_ref[...]
    xc = cb_ref[...] + cw_ref[CONV_WIDTH - 1:CONV_WIDTH, :] * xs_ref[8:, :]
    for k in range(1, CONV_WIDTH):
        xc = xc + cw_ref[CONV_WIDTH - 1 - k:CONV_WIDTH - k, :] * xs_ref[pl.ds(8 - k, tc), :]
    xs_ref[0:8, :] = xs_ref[tc:tc + 8, :]

    xcb = xc.astype(BF16)
    r = jax.nn.sigmoid(_dot(xcb, wa_ref[...]) + ba_ref[...])
    gi = jax.nn.sigmoid(_dot(xcb, wx_ref[...]) + bx_ref[...])
    lam = lam_ref[...]
    softplus_neg = jnp.maximum(-lam, 0.0) + jnp.log1p(jnp.exp(-jnp.abs(lam)))
    log_a = -LRU_C * r * softplus_neg
    a = jnp.exp(log_a)
    b = jnp.sqrt(1.0 - jnp.exp(2.0 * log_a)) * (gi * xc)

    row = lax.broadcasted_iota(jnp.int32, (tc, 1), 0)
    s = 1
    while s < tc:
        valid = row >= s
        b = jnp.where(valid, a * pltpu.roll(b, s, 0) + b, b)
        a = jnp.where(valid, a * pltpu.roll(a, s, 0), a)
        s *= 2
    h = b + a * h_ref[0:1, :]
    h_ref[0:1, :] = h[tc - 1:tc, :]
    o_ref[...] = _rms(h * jax.nn.gelu(xg_ref[...]), gn_ref[...]).astype(BF16)


def _rglru(xrg, cw, cb, wa, ba, wx, bx, lam, gn, batch, seq, tc):
    nt = seq // tc
    w = cw.shape[1]
    vec = pl.BlockSpec((1, w), lambda b, i: (0, 0))
    return pl.pallas_call(
        _rglru_kernel,
        grid=(batch, nt),
        in_specs=[
            pl.BlockSpec((tc, w), lambda b, i: (b * nt + i, 0)),
            pl.BlockSpec((tc, w), lambda b, i: (b * nt + i, 1)),
            pl.BlockSpec((CONV_WIDTH, w), lambda b, i: (0, 0)),
            vec,
            pl.BlockSpec((w, w), lambda b, i: (0, 0)), vec,
            pl.BlockSpec((w, w), lambda b, i: (0, 0)), vec,
            vec, vec,
        ],
        out_specs=pl.BlockSpec((tc, w), lambda b, i: (b * nt + i, 0)),
        out_shape=jax.ShapeDtypeStruct((batch * seq, w), BF16),
        scratch_shapes=[pltpu.VMEM((tc + 8, w), F32), pltpu.VMEM((8, w), F32)],
        compiler_params=_params(("parallel", "arbitrary")),
        name="rglru",
    )(xrg, xrg, cw, cb, wa, ba, wx, bx, lam, gn)


def _out_proj_kernel(a_ref, l_ref, x_ref, ga_ref, woa_ref, wol_ref, gf_ref, wq_ref, sk_ref,
                     h1_ref, xt_ref, st_ref):
    an = _rms(a_ref[...], ga_ref[...]).astype(BF16)
    h1 = x_ref[...] + _dot(an, woa_ref[...]) + _dot(l_ref[...], wol_ref[...])
    h1_ref[...] = h1
    xn = _rms(h1, gf_ref[...])
    xt_ref[...] = xn.T.astype(BF16)
    q = _dot(xn.astype(BF16), wq_ref[...])
    for hp in range(2 * PEER_HEADS):
        qs = q[:, LANES * hp:LANES * (hp + 1)].astype(BF16)
        st_ref[hp] = _dot_nt(sk_ref[hp % 2], qs)


def _out_proj(a_out, l_n, x2, ga, woa, wol, gf, wq, sk, tm):
    n, d = x2.shape
    full = lambda arr: pl.BlockSpec(arr.shape, lambda i: (0,) * arr.ndim)
    return pl.pallas_call(
        _out_proj_kernel,
        grid=(n // tm,),
        in_specs=[
            pl.BlockSpec((tm, a_out.shape[1]), lambda i: (i, 0)),
            pl.BlockSpec((tm, l_n.shape[1]), lambda i: (i, 0)),
            pl.BlockSpec((tm, d), lambda i: (i, 0)),
            full(ga), full(woa), full(wol), full(gf), full(wq), full(sk),
        ],
        out_specs=[
            pl.BlockSpec((tm, d), lambda i: (i, 0)),
            pl.BlockSpec((d, tm), lambda i: (0, i)),
            pl.BlockSpec((2 * PEER_HEADS, PEER_N_KEYS, tm), lambda i: (0, 0, i)),
        ],
        out_shape=[
            jax.ShapeDtypeStruct((n, d), F32),
            jax.ShapeDtypeStruct((d, n), BF16),
            jax.ShapeDtypeStruct((2 * PEER_HEADS, PEER_N_KEYS, n), F32),
        ],
        compiler_params=_params(("parallel",)),
        name="out_proj",
    )(a_out, l_n, x2, ga, woa, wol, gf, wq, sk)


N_TOP = PEER_TOPK
_PAIRS = [(i, j) for i in range(N_TOP) for j in range(N_TOP) if (i + 1) * (j + 1) <= N_TOP]


def _top_values(s):
    n, tl = s.shape
    parts = [s[SUBLANES * k:SUBLANES * (k + 1)] for k in range(n // SUBLANES)]
    base = lax.broadcasted_iota(jnp.int32, (SUBLANES, tl), 0).astype(F32)
    rows = [base + float(SUBLANES * k) for k in range(len(parts))]
    out = []
    for r in range(N_TOP):
        m = jnp.max(functools.reduce(jnp.maximum, parts), axis=0, keepdims=True)
        out.append(m)
        first = jnp.full((SUBLANES, tl), float(n), F32)
        for k in reversed(range(len(parts))):
            first = jnp.where(parts[k] == m, rows[k], first)
        first = jnp.min(first, axis=0, keepdims=True)
        taken = TAKEN * (1.0 + r / TAKEN_STEPS)
        parts = [jnp.where(rows[k] == first, taken, parts[k]) for k in range(len(parts))]
    rank = jnp.concatenate(parts, axis=0)
    rank = jnp.where(rank <= TAKEN, (rank * (1.0 / TAKEN) - 1.0) * TAKEN_STEPS, float(PEER_TOPK))
    return out, rank


def _peer_stats_kernel(st_ref, r2_ref, e2_ref, l_ref, e1_ref):
    tops = [[], []]
    rank1 = []
    for h in range(PEER_HEADS):
        vals, rank = _top_values(st_ref[2 * h])
        tops[0].append(vals)
        rank1.append(rank)
        vals, rank = _top_values(st_ref[2 * h + 1])
        tops[1].append(vals)
        r2_ref[h] = rank.astype(BF16)
    a = [jnp.concatenate([tops[0][h][i] for h in range(PEER_HEADS)], axis=0) for i in range(N_TOP)]
    b = [jnp.concatenate([tops[1][h][j] for h in range(PEER_HEADS)], axis=0) for j in range(N_TOP)]
    cand = [a[i] + b[j] for (i, j) in _PAIRS]
    work = list(cand)
    kth = []
    for _ in range(N_TOP):
        m = functools.reduce(jnp.maximum, work)
        kth.append(m)
        found = jnp.zeros(m.shape, jnp.bool_)
        for k in range(len(work)):
            hit = (work[k] == m) & jnp.logical_not(found)
            work[k] = jnp.where(hit, REMOVED, work[k])
            found = found | hit
    thr = kth[PEER_TOPK - 1]
    top = a[0] + b[0]
    z = functools.reduce(lambda u, w: u + w, [jnp.where(c >= thr, jnp.exp(c - top), 0.0) for c in cand])
    n_sel = [functools.reduce(lambda u, w: u + w,
                              [(a[r] + b[j] >= thr).astype(F32) for j in range(PEER_TOPK)])
             for r in range(PEER_TOPK)]
    for h in range(PEER_HEADS):
        row = slice(h, h + 1)
        prefix = jnp.zeros(rank1[h].shape, F32)
        for r in range(PEER_TOPK):
            prefix = jnp.where(rank1[h] == float(r), n_sel[r][row], prefix)
        l_ref[h] = prefix
        e1_ref[h] = jnp.exp(st_ref[2 * h] - a[0][row])
        e2_ref[h] = (jnp.exp(st_ref[2 * h + 1] - b[0][row]) / z[row]).astype(BF16)


def _peer_stats(st, tl):
    hp, nk, n = st.shape
    out_spec = pl.BlockSpec((PEER_HEADS, nk, tl), lambda i: (0, 0, i))
    return pl.pallas_call(
        _peer_stats_kernel,
        grid=(n // tl,),
        in_specs=[pl.BlockSpec((hp, nk, tl), lambda i: (0, 0, i))],
        out_specs=[out_spec] * 4,
        out_shape=[
            jax.ShapeDtypeStruct((PEER_HEADS, nk, n), BF16),
            jax.ShapeDtypeStruct((PEER_HEADS, nk, n), BF16),
            jax.ShapeDtypeStruct((PEER_HEADS, nk, n), F32),
            jax.ShapeDtypeStruct((PEER_HEADS, nk, n), F32),
        ],
        compiler_params=_params(("parallel",)),
        name="peer_stats",
    )(st)


def _peer_main_kernel(u_ref, vt_ref, xt_ref, r2_ref, e2_ref, l_ref, e1_ref, h1_ref, o_ref, acc_ref):
    c = pl.program_id(1)
    ce = u_ref.shape[0]
    tm = xt_ref.shape[1]
    nk = PEER_N_KEYS

    @pl.when(c == 0)
    def _():
        acc_ref[...] = jnp.zeros(acc_ref.shape, F32)

    act = _gelu_tanh(_dot(u_ref[...], xt_ref[...]))

    def rows_bf16(ref, h, i1):
        packed = jnp.broadcast_to(ref[h, pl.ds(i1, 1), :], (BF16_ROWS, tm)).astype(BF16)
        return jnp.concatenate([packed] * (nk // BF16_ROWS), axis=0)

    w_parts = []
    for ii in range(ce // nk):
        i1 = c * (ce // nk) + ii
        g = None
        for h in range(PEER_HEADS):
            e2 = e2_ref[h]
            term = jnp.where(r2_ref[h] < rows_bf16(l_ref, h, i1), e2, jnp.zeros_like(e2)) * rows_bf16(e1_ref, h, i1)
            g = term if g is None else g + term
        w_parts.append(act[ii * nk:(ii + 1) * nk, :].astype(BF16) * g)
    w = jnp.concatenate(w_parts, axis=0)
    acc_ref[...] += _dot(vt_ref[...], w)

    @pl.when(c == pl.num_programs(1) - 1)
    def _():
        o_ref[...] = h1_ref[...] + acc_ref[...].T


def _peer_main(u_b, vt_b, xt, r2, e2, n_sel, e1, h1, tm, ce):
    ne, d = u_b.shape
    n = xt.shape[1]
    per_tok = pl.BlockSpec((PEER_HEADS, PEER_N_KEYS, tm), lambda t, c: (0, 0, t))
    return pl.pallas_call(
        _peer_main_kernel,
        grid=(n // tm, ne // ce),
        in_specs=[
            pl.BlockSpec((ce, d), lambda t, c: (c, 0)),
            pl.BlockSpec((d, ce), lambda t, c: (0, c)),
            pl.BlockSpec((d, tm), lambda t, c: (0, t)),
            per_tok, per_tok, per_tok, per_tok,
            pl.BlockSpec((tm, d), lambda t, c: (t, 0)),
        ],
        out_specs=pl.BlockSpec((tm, d), lambda t, c: (t, 0)),
        out_shape=jax.ShapeDtypeStruct((n, d), F32),
        scratch_shapes=[pltpu.VMEM((d, tm), F32)],
        compiler_params=_params(("parallel", "arbitrary")),
        name="peer_main",
    )(u_b, vt_b, xt, r2, e2, n_sel, e1, h1)


def _ple_final_kernel(h_ref, p_ref, gp_ref, wg_ref, bg_ref, wp_ref, gfin_ref, o_ref):
    h = h_ref[...]
    gate = jax.nn.sigmoid(_dot(_rms(h, gp_ref[...]).astype(BF16), wg_ref[...]) + bg_ref[...])
    h = h + gate * _dot(p_ref[...].astype(BF16), wp_ref[...])
    o_ref[...] = _rms(h, gfin_ref[...])


def _ple_final(h2, p2, gp, wg, bg, wp, gfin, tm):
    n, d = h2.shape
    full = lambda arr: pl.BlockSpec(arr.shape, lambda i: (0,) * arr.ndim)
    return pl.pallas_call(
        _ple_final_kernel,
        grid=(n // tm,),
        in_specs=[
            pl.BlockSpec((tm, d), lambda i: (i, 0)),
            pl.BlockSpec((tm, p2.shape[1]), lambda i: (i, 0)),
            full(gp), full(wg), full(bg), full(wp), full(gfin),
        ],
        out_specs=pl.BlockSpec((tm, d), lambda i: (i, 0)),
        out_shape=jax.ShapeDtypeStruct((n, d), F32),
        compiler_params=_params(("parallel",)),
        name="ple_final",
    )(h2, p2, gp, wg, bg, wp, gfin)


def _overlap_t(seq, n_rows):
    n_cmp = n_rows - 1
    cmp_start = np.arange(n_cmp) * CMP_STRIDE
    sel_start = np.arange(seq // SEL_LEN) * SEL_LEN
    ov = np.clip(np.minimum(cmp_start[:, None] + CMP_LEN, sel_start[None, :] + SEL_LEN)
                 - np.maximum(cmp_start[:, None], sel_start[None, :]), 0, None).astype(np.float32) / CMP_LEN
    out = np.zeros((seq // SEL_LEN, n_rows), np.float32)
    out[:, :n_cmp] = ov.T
    return out


def _block_expand(seq, starts, width):
    pos = np.asarray(starts).reshape(-1, 1, 1) + np.arange(width).reshape(1, 1, width)
    blk = np.where(pos >= 0, pos // SEL_LEN, -1)
    return (blk == np.arange(seq // SEL_LEN).reshape(1, -1, 1)).astype(np.float32)


def _block_diag(w):
    nb, bd, _ = w.shape
    out = jnp.zeros((nb * bd, nb * bd), w.dtype)
    for j in range(nb):
        out = out.at[j * bd:(j + 1) * bd, j * bd:(j + 1) * bd].set(w[j])
    return out


def _layer(h, p_i, attn_norm, w_in, cmp_k_pe, cmp_k_w1, cmp_k_w2, cmp_v_pe, cmp_v_w1, cmp_v_w2,
           rel_table, conv_w, conv_b, lru_wa, lru_ba, lru_wx, lru_bx, lru_lambda,
           grp_norm_attn, grp_norm_lru, w_out, ffn_norm, peer_wq, peer_subkeys, peer_u, peer_v,
           ple_norm, ple_wgate, ple_bgate, ple_proj, *, tm, tc, tl, ce):
    batch, seq, d = h.shape
    n = batch * seq
    x2 = h.reshape(n, d)
    aw = N_HEADS * HEAD_DIM
    kvw = N_KV_HEADS * HEAD_DIM
    lw = d - aw

    o = 0
    cols = {}
    for name, width in (("q", aw), ("kc", kvw), ("vc", kvw), ("ks", kvw), ("vs", kvw), ("kw", kvw),
                        ("vw", kvw), ("gates", N_HEADS * 3), ("xr", lw), ("xg", lw)):
        cols[name] = w_in[:, o:o + width]
        o += width
    gate_pad = [jnp.pad(cols["gates"][:, N_GATE_COLS * k:N_GATE_COLS * (k + 1)], ((0, 0), (0, LANES - N_GATE_COLS)))
                for k in range(N_KV_HEADS)]
    wf = jnp.concatenate([cols["xr"], cols["xg"], cols["kc"], cols["vc"]] + gate_pad, axis=1).astype(BF16)
    wb = jnp.concatenate([cols["q"], cols["ks"], cols["kw"], cols["vs"], cols["vw"]], axis=1).astype(BF16)

    xrg, kvc, gates2, q4, kk, vv = _in_proj(x2, attn_norm.reshape(1, d), wf, wb, tm)

    n_rows = seq // CMP_STRIDE
    r4 = kvc.reshape(4, batch, n_rows, CMP_STRIDE * HEAD_DIM)
    pe2 = jnp.stack([cmp_k_pe.reshape(1, -1), cmp_v_pe.reshape(1, -1)])
    w1s = jnp.stack([cmp_k_w1, cmp_v_w1]).astype(BF16)
    w2s = jnp.stack([cmp_k_w2, cmp_v_w2]).astype(BF16)
    cmp4 = _compress(r4, pe2, w1s, w2s)

    winadd, seladd = _bias_near(rel_table)
    addc = _bias_cmp(rel_table, seq, n_rows, min(seq, 512))
    ovlt = jnp.asarray(_overlap_t(seq, n_rows), BF16)
    o_c, sel = _cmp_attn(q4, cmp4, addc, ovlt, batch, seq)

    nt = seq // ATT_TILE
    tile_starts = np.arange(nt) * ATT_TILE
    enear = jnp.asarray(_block_expand(seq, tile_starts - ATT_TILE, NEAR_SPAN), BF16)
    eslab = jnp.asarray(_block_expand(seq, np.arange(max(seq // FAR_SLAB, 1)) * FAR_SLAB, FAR_SLAB), BF16)
    front = ((0, 0), (0, 0), (WINDOW, 0), (0, 0))
    kk5 = jnp.pad(kk.reshape(4, batch, seq, HEAD_DIM), front)
    vv5 = jnp.pad(vv.reshape(4, batch, seq, LANES), front)
    a_out = _sel_win_attn(q4, kk5, vv5, sel, enear, eslab, winadd, seladd, o_c, gates2, batch, seq)

    l_n = _rglru(xrg, conv_w.reshape(CONV_WIDTH, lw), conv_b.reshape(1, lw),
                 _block_diag(lru_wa).astype(BF16), lru_ba.reshape(1, lw),
                 _block_diag(lru_wx).astype(BF16), lru_bx.reshape(1, lw),
                 lru_lambda.reshape(1, lw), grp_norm_lru.reshape(1, lw), batch, seq, tc)

    w_out_b = w_out.astype(BF16)
    h1, xt, st = _out_proj(a_out, l_n, x2, grp_norm_attn.reshape(1, aw), w_out_b[:aw], w_out_b[aw:],
                           ffn_norm.reshape(1, d), peer_wq.astype(BF16), peer_subkeys.astype(BF16), tm)
    r2, e2, n_sel, e1 = _peer_stats(st, tl)
    h2 = _peer_main(peer_u.astype(BF16), peer_v.T.astype(BF16), xt, r2, e2, n_sel, e1, h1, tm, ce)
    ple_args = (h2, p_i.reshape(n, -1), ple_norm.reshape(1, d), ple_wgate.astype(BF16), ple_bgate.reshape(1, d),
                ple_proj.astype(BF16))
    return ple_args, dict(a_out=a_out, l_n=l_n, h1=h1)


def kernel(x, p, attn_norm, w_in, cmp_k_pe, cmp_k_w1, cmp_k_w2, cmp_v_pe, cmp_v_w1, cmp_v_w2, rel_table, conv_w, conv_b, lru_wa, lru_ba, lru_wx, lru_bx, lru_lambda, grp_norm_attn, grp_norm_lru, w_out, ffn_norm, peer_wq, peer_subkeys, peer_u, peer_v, ple_norm, ple_wgate, ple_bgate, ple_proj, final_norm):
    batch, seq, d = x.shape
    assert w_in.shape[0] == 1, "single-layer stack only"
    tm = min(512, seq)
    (h2, p2, gp, wg, bg, wp), _ = _layer(
        x, p[0], attn_norm[0], w_in[0], cmp_k_pe[0], cmp_k_w1[0], cmp_k_w2[0], cmp_v_pe[0], cmp_v_w1[0],
        cmp_v_w2[0], rel_table, conv_w[0], conv_b[0], lru_wa[0], lru_ba[0], lru_wx[0], lru_bx[0],
        lru_lambda[0], grp_norm_attn[0], grp_norm_lru[0], w_out[0], ffn_norm[0], peer_wq[0],
        peer_subkeys[0], peer_u[0], peer_v[0], ple_norm[0], ple_wgate[0], ple_bgate[0], ple_proj[0],
        tm=tm, tc=min(256, seq), tl=min(256, seq), ce=2048)
    out = _ple_final(h2, p2, gp, wg, bg, wp, final_norm.reshape(1, d), tm)
    return out.reshape(batch, seq, d)
```

```python
import functools
import math

import numpy as np
import jax
import jax.numpy as jnp
from jax import lax
from jax.experimental import pallas as pl
from jax.experimental.pallas import tpu as pltpu

F32 = jnp.float32
BF16 = jnp.bfloat16

N_HEADS = 8
HEAD_DIM = 64
N_KV_HEADS = 2
GROUP = N_HEADS // N_KV_HEADS
LRU_C = 8.0
CONV_WIDTH = 4
CMP_LEN = 32
CMP_STRIDE = 16
SEL_LEN = 64
SEL_TOPK = 16
N_LOCAL_BLOCKS = 2
WINDOW = 512
FORCE_LOCAL = 2.0e4
FORCE_INIT = 1.0e4
REL_BUCKETS = 32
REL_MAX_DIST = 128
PEER_HEADS = 8
PEER_N_KEYS = 128
PEER_TOPK = 16
EPS = 1e-6
NEG = -1e30
LOG2E = math.log2(math.e)
Q_SCALE = HEAD_DIM ** -0.5 * LOG2E
REMOVED = -3.0e38
TAKEN = -2.0 ** 120
TAKEN_STEPS = 32.0

LANES = 128
SUBLANES = 8
BF16_ROWS = 16
VMEM_LIMIT = 56 * 1024 * 1024

ATT_TILE = 256
CMP_TILE = 512
N_GATE_COLS = GROUP * 3


def _dot(a, b):
    return jnp.dot(a, b, preferred_element_type=F32)


def _dot_nt(a, b):
    return lax.dot_general(a, b, (((1,), (1,)), ((), ())), preferred_element_type=F32)


def _gelu_tanh(x):
    c = math.sqrt(2.0 / math.pi)
    half = 0.5 * x
    return half + half * jnp.tanh(x * (c + (c * 0.044715) * (x * x)))


def _rms(x, g):
    return x * lax.rsqrt(jnp.mean(x * x, axis=-1, keepdims=True) + EPS) * g


def _params(sem, vmem=VMEM_LIMIT):
    return pltpu.CompilerParams(dimension_semantics=sem, vmem_limit_bytes=vmem)


def _in_proj_kernel(x_ref, g_ref, wf_ref, wb_ref, xrg_ref, kvc_ref, gates_ref, q4_ref, kk_ref, vv_ref):
    dh = HEAD_DIM
    n_kv = 2 * N_KV_HEADS
    xb = _rms(x_ref[...], g_ref[...]).astype(BF16)
    yf = _dot(xb, wf_ref[...])
    kv0 = xrg_ref.shape[1]
    g0 = kv0 + n_kv * dh
    xrg_ref[...] = yf[:, :kv0]
    for j in range(n_kv):
        kvc_ref[j] = yf[:, kv0 + dh * j:kv0 + dh * (j + 1)]
    for k in range(N_KV_HEADS):
        gates_ref[k] = yf[:, g0 + LANES * k:g0 + LANES * (k + 1)]
    yb = _dot(xb, wb_ref[...])
    for h in range(N_HEADS):
        q4_ref[h] = (yb[:, dh * h:dh * (h + 1)] * Q_SCALE).astype(BF16)
    k0 = N_HEADS * dh
    v0 = k0 + n_kv * dh
    ones_col = (lax.broadcasted_iota(jnp.int32, (yb.shape[0], LANES - dh), 1) == 0).astype(F32)
    for j in range(n_kv):
        kk_ref[j] = yb[:, k0 + dh * j:k0 + dh * (j + 1)].astype(BF16)
        vv_ref[j] = jnp.concatenate([yb[:, v0 + dh * j:v0 + dh * (j + 1)], ones_col], axis=1).astype(BF16)


def _in_proj(x2, g, wf, wb, tm):
    n, d = x2.shape
    dh = HEAD_DIM
    n_kv = 2 * N_KV_HEADS
    lru2 = wf.shape[1] - n_kv * dh - N_KV_HEADS * LANES
    shapes = [
        ((n, lru2), F32),
        ((n_kv, n, dh), F32),
        ((N_KV_HEADS, n, LANES), F32),
        ((N_HEADS, n, dh), BF16),
        ((n_kv, n, dh), BF16),
        ((n_kv, n, LANES), BF16),
    ]

    def spec(shape):
        if len(shape) == 2:
            return pl.BlockSpec((tm, shape[1]), lambda i: (i, 0))
        return pl.BlockSpec((shape[0], tm, shape[2]), lambda i: (0, i, 0))

    return pl.pallas_call(
        _in_proj_kernel,
        grid=(n // tm,),
        in_specs=[
            pl.BlockSpec((tm, d), lambda i: (i, 0)),
            pl.BlockSpec((1, d), lambda i: (0, 0)),
            pl.BlockSpec(wf.shape, lambda i: (0, 0)),
            pl.BlockSpec(wb.shape, lambda i: (0, 0)),
        ],
        out_specs=[spec(shape) for shape, _ in shapes],
        out_shape=[jax.ShapeDtypeStruct(shape, dtype) for shape, dtype in shapes],
        compiler_params=_params(("parallel",)),
        name="in_proj",
    )(x2, g, wf, wb)


def _compress_kernel(r_ref, pe_ref, w1_ref, w2_ref, o_ref):
    r = r_ref[0, 0]
    n_rows = r.shape[0]
    half = r.shape[1]
    pe = pe_ref[0]
    top = _dot((r + pe[:, :half]).astype(BF16), w1_ref[0, :half, :])
    bot = _dot((r + pe[:, half:]).astype(BF16), w1_ref[0, half:, :])
    hid = top + pltpu.roll(bot, n_rows - 1, 0)
    act = jax.nn.gelu(hid)
    out = _dot(act.astype(BF16), w2_ref[0])
    row = lax.broadcasted_iota(jnp.int32, out.shape, 0)
    o_ref[0, 0] = jnp.where(row < n_rows - 1, out, 0.0).astype(BF16)


def _compress(r4, pe2, w1s, w2s):
    four, b, n_rows, width = r4.shape
    return pl.pallas_call(
        _compress_kernel,
        grid=(four, b),
        in_specs=[
            pl.BlockSpec((1, 1, n_rows, width), lambda j, bb: (j, bb, 0, 0)),
            pl.BlockSpec((1, 1, 2 * width), lambda j, bb: (j // 2, 0, 0)),
            pl.BlockSpec((1, 2 * width, w1s.shape[2]), lambda j, bb: (j // 2, 0, 0)),
            pl.BlockSpec((1, w2s.shape[1], HEAD_DIM), lambda j, bb: (j // 2, 0, 0)),
        ],
        out_specs=pl.BlockSpec((1, 1, n_rows, HEAD_DIM), lambda j, bb: (j, bb, 0, 0)),
        out_shape=jax.ShapeDtypeStruct((four, b, n_rows, HEAD_DIM), BF16),
        compiler_params=_params(("parallel", "parallel")),
        name="compress",
    )(r4, pe2, w1s, w2s)


def _bucket_breakpoints():
    n = np.arange(0, 4 * REL_MAX_DIST)
    max_exact = REL_BUCKETS // 2
    nf = np.maximum(n, max_exact).astype(np.float32)
    large = max_exact + (np.log(nf / max_exact) / np.float32(math.log(REL_MAX_DIST / max_exact))
                         * (REL_BUCKETS - max_exact)).astype(np.int32)
    large = np.minimum(large, REL_BUCKETS - 1)
    bucket = np.where(n < max_exact, n, large)
    return [int(np.argmax(bucket >= j)) for j in range(1, REL_BUCKETS)]


_BREAKS = _bucket_breakpoints()


def _bias_of_dist(dist, tab_ref, h):
    v = jnp.full(dist.shape, tab_ref[0, h], F32)
    for j, bp in enumerate(_BREAKS, start=1):
        v = jnp.where(dist >= bp, tab_ref[j, h], v)
    v = (v - tab_ref[REL_BUCKETS - 1, h]) * LOG2E
    return jnp.where(dist >= 0, v, NEG)


WIN_SPAN = WINDOW + ATT_TILE
NEAR_SPAN = 2 * ATT_TILE


def _bias_near_kernel(tab_ref, win_ref, sel_ref):
    h = pl.program_id(0)
    t = ATT_TILE
    for ref, span in ((win_ref, WIN_SPAN), (sel_ref, NEAR_SPAN)):
        r = lax.broadcasted_iota(jnp.int32, (t, span), 0)
        c = lax.broadcasted_iota(jnp.int32, (t, span), 1)
        dist = r - c + (span - t)
        ref[0] = jnp.where(dist < WINDOW, _bias_of_dist(dist, tab_ref, h), NEG)


def _bias_near(rel_table):
    t = ATT_TILE
    return pl.pallas_call(
        _bias_near_kernel,
        grid=(N_HEADS,),
        in_specs=[pl.BlockSpec(memory_space=pltpu.SMEM)],
        out_specs=[pl.BlockSpec((1, t, WIN_SPAN), lambda h: (h, 0, 0)),
                   pl.BlockSpec((1, t, NEAR_SPAN), lambda h: (h, 0, 0))],
        out_shape=[jax.ShapeDtypeStruct((N_HEADS, t, WIN_SPAN), F32),
                   jax.ShapeDtypeStruct((N_HEADS, t, NEAR_SPAN), F32)],
        compiler_params=_params(("arbitrary",)),
        name="bias_near",
    )(rel_table)


def _bias_cmp_kernel(tab_ref, o_ref, *, n_cmp):
    h = pl.program_id(0)
    i = pl.program_id(1)
    tq, nc = o_ref.shape[1], o_ref.shape[2]
    span = 2 * nc
    b = lax.broadcasted_iota(jnp.int32, (CMP_STRIDE, span), 0)
    u = lax.broadcasted_iota(jnp.int32, (CMP_STRIDE, span), 1)
    table = _bias_of_dist(CMP_STRIDE * (nc - 1 - u) + b - (CMP_LEN - 1), tab_ref, h)
    col = lax.broadcasted_iota(jnp.int32, (CMP_STRIDE, nc), 1)
    groups = tq // CMP_STRIDE
    for k in range(groups):
        a = i * groups + k
        win = pltpu.roll(table, (a + span - (nc - 1)) % span, 1)[:, :nc]
        o_ref[0, CMP_STRIDE * k:CMP_STRIDE * (k + 1), :] = jnp.where(col < n_cmp, win, NEG)


def _bias_cmp(rel_table, seq, n_rows, tq):
    return pl.pallas_call(
        functools.partial(_bias_cmp_kernel, n_cmp=n_rows - 1),
        grid=(N_HEADS, seq // tq),
        in_specs=[pl.BlockSpec(memory_space=pltpu.SMEM)],
        out_specs=pl.BlockSpec((1, tq, n_rows), lambda h, i: (h, i, 0)),
        out_shape=jax.ShapeDtypeStruct((N_HEADS, seq, n_rows), F32),
        compiler_params=_params(("arbitrary", "arbitrary")),
        name="bias_cmp",
    )(rel_table)


def _cmp_attn_kernel(q_ref, kc_ref, vc_ref, addc_ref, ovlt_ref, eye_ref, oc_ref, sel_ref):
    i = pl.program_id(1)
    tq = q_ref.shape[1]
    n_sel = ovlt_ref.shape[0]
    kc = kc_ref[0, 0]
    vc = vc_ref[0, 0]
    t_col = lax.broadcasted_iota(jnp.int32, (tq, 1), 0) + i * tq
    row_ok = (t_col >= CMP_LEN - 1).astype(F32)
    psum = None
    for hh in range(GROUP):
        logit = _dot_nt(q_ref[hh], kc) + addc_ref[hh]
        m = jnp.max(logit, axis=-1, keepdims=True)
        e = jnp.exp2(logit - m)
        p = e / jnp.sum(e, axis=-1, keepdims=True)
        oc_ref[:, HEAD_DIM * hh:HEAD_DIM * (hh + 1)] = _dot(p.astype(BF16), vc) * row_ok
        psum = p if psum is None else psum + p
    psum = psum * row_ok
    hi = psum.astype(BF16)
    lo = (psum - hi.astype(F32)).astype(BF16)
    ovlt = ovlt_ref[...]
    imp = _dot_nt(ovlt, hi) + _dot_nt(ovlt, lo)
    s_idx = lax.broadcasted_iota(jnp.int32, (n_sel, tq), 0)
    t_blk = (lax.broadcasted_iota(jnp.int32, (n_sel, tq), 1) + i * tq) // SEL_LEN
    d_blk = t_blk - s_idx
    local = (d_blk >= 0) & (d_blk < N_LOCAL_BLOCKS)
    v = jnp.where(local, FORCE_LOCAL,
                  jnp.where(s_idx == 0, FORCE_INIT, jnp.where(d_blk >= 0, imp, -1.0)))
    rows = s_idx.astype(F32)
    sel = jnp.zeros((n_sel, tq), F32)
    for _ in range(min(SEL_TOPK, n_sel)):
        m = jnp.max(v, axis=0, keepdims=True)
        first = jnp.min(jnp.where(v == m, rows, float(n_sel)), axis=0, keepdims=True)
        pick = rows == first
        sel = jnp.where(pick, 1.0, sel)
        v = jnp.where(pick, REMOVED, v)
    sel_ref[0] = _dot_nt(eye_ref[...], sel.astype(BF16)).astype(BF16)


def _cmp_attn(q4, cmp4, addc, ovlt, batch, seq):
    tq = min(CMP_TILE, seq)
    nt = seq // tq
    n_rows = cmp4.shape[2]
    n_sel = ovlt.shape[0]
    eye = jnp.eye(tq, dtype=BF16)
    return pl.pallas_call(
        _cmp_attn_kernel,
        grid=(batch * N_KV_HEADS, nt),
        in_specs=[
            pl.BlockSpec((GROUP, tq, HEAD_DIM), lambda g, i: (g % 2, (g // 2) * nt + i, 0)),
            pl.BlockSpec((1, 1, n_rows, HEAD_DIM), lambda g, i: (g % 2, g // 2, 0, 0)),
            pl.BlockSpec((1, 1, n_rows, HEAD_DIM), lambda g, i: (2 + g % 2, g // 2, 0, 0)),
            pl.BlockSpec((GROUP, tq, n_rows), lambda g, i: (g % 2, i, 0)),
            pl.BlockSpec(ovlt.shape, lambda g, i: (0, 0)),
            pl.BlockSpec((tq, tq), lambda g, i: (0, 0)),
        ],
        out_specs=[
            pl.BlockSpec((tq, GROUP * HEAD_DIM), lambda g, i: ((g // 2) * nt + i, g % 2)),
            pl.BlockSpec((1, tq, n_sel), lambda g, i: (g, i, 0)),
        ],
        out_shape=[
            jax.ShapeDtypeStruct((batch * seq, N_HEADS * HEAD_DIM), F32),
            jax.ShapeDtypeStruct((batch * N_KV_HEADS, seq, n_sel), BF16),
        ],
        compiler_params=_params(("parallel", "arbitrary")),
        name="cmp_attn",
    )(q4, cmp4, cmp4, addc, ovlt, eye)


def _softmax_step(qs, k, v1, mask, add, carry):
    tq = qs.shape[0] // GROUP
    s = _dot_nt(qs, k).reshape(GROUP, tq, k.shape[0])
    if mask is not None:
        s = jnp.where(mask[None] > 0.5, s, NEG)
    if add is not None:
        s = s + add
    m_tile = jnp.max(s, axis=-1, keepdims=True)
    if carry is None:
        m_new = m_tile
    else:
        m, acc = carry
        m_new = jnp.maximum(m, m_tile)
    p = jnp.exp2(s - m_new)
    pv = _dot(p.reshape(GROUP * tq, k.shape[0]).astype(BF16), v1).reshape(GROUP, tq, v1.shape[1])
    if carry is None:
        return m_new, pv
    return m_new, jnp.exp2(m - m_new) * acc + pv


FAR_SLAB = 4 * ATT_TILE


def _sel_win_kernel(q_ref, ks_ref, vs_ref, kw_ref, vw_ref, sel_ref, enear_ref, eslab_ref,
                    winadd_ref, seladd_ref, oc_ref, gates_ref, o_ref):
    i = pl.program_id(1)
    tq = q_ref.shape[1]
    qs = q_ref[...].reshape(GROUP * tq, HEAD_DIM)
    sel = sel_ref[0]

    def rows(ref, start, size):
        return ref[0, 0, pl.ds(pl.multiple_of(start, ATT_TILE), size), :]

    def hide_padding(add, n_pad):
        col = lax.broadcasted_iota(jnp.int32, (1, 1, add.shape[2]), 2)
        return jnp.where(col < n_pad, NEG, add)

    w0 = i * tq
    _, acc_w = _softmax_step(qs, rows(kw_ref, w0, WIN_SPAN), rows(vw_ref, w0, WIN_SPAN), None,
                             hide_padding(winadd_ref[...], WINDOW - i * tq), None)

    n0 = w0 + WINDOW - tq
    carry = _softmax_step(qs, rows(ks_ref, n0, NEAR_SPAN), rows(vs_ref, n0, NEAR_SPAN),
                          _dot(sel, enear_ref[i]), hide_padding(seladd_ref[...], tq - i * tq), None)
    far_end = jnp.maximum(i - 1, 0) * tq
    col = lax.broadcasted_iota(jnp.int32, (1, FAR_SLAB), 1)

    def far_slab(j, c):
        start = WINDOW + j * FAR_SLAB
        mask = jnp.where(col < far_end - j * FAR_SLAB, _dot(sel, eslab_ref[j]), 0.0)
        return _softmax_step(qs, rows(ks_ref, start, FAR_SLAB), rows(vs_ref, start, FAR_SLAB), mask, None, c)

    _, acc_s = lax.fori_loop(0, (far_end + FAR_SLAB - 1) // FAR_SLAB, far_slab, carry)
    o_s = acc_s[..., :HEAD_DIM] / acc_s[..., HEAD_DIM:HEAD_DIM + 1]
    o_w = acc_w[..., :HEAD_DIM] / acc_w[..., HEAD_DIM:HEAD_DIM + 1]

    g = jax.nn.sigmoid(gates_ref[0])
    for hh in range(GROUP):
        o_c = oc_ref[:, HEAD_DIM * hh:HEAD_DIM * (hh + 1)]
        o_ref[:, HEAD_DIM * hh:HEAD_DIM * (hh + 1)] = (g[:, 3 * hh:3 * hh + 1] * o_c
                                            + g[:, 3 * hh + 1:3 * hh + 2] * o_s[hh]
                                            + g[:, 3 * hh + 2:3 * hh + 3] * o_w[hh])


def _sel_win_attn(q4, kk5, vv5, sel, enear, eslab, winadd, seladd, o_c, gates2, batch, seq):
    tq = ATT_TILE
    nt = seq // tq
    n_sel = sel.shape[2]
    padded = kk5.shape[2]

    def kv_spec(arr, base):
        return pl.BlockSpec((1, 1, padded, arr.shape[3]), lambda g, i: (base + g % 2, g // 2, 0, 0))

    whole = lambda arr: pl.BlockSpec(arr.shape, lambda g, i: (0,) * arr.ndim)
    return pl.pallas_call(
        _sel_win_kernel,
        grid=(batch * N_KV_HEADS, nt),
        in_specs=[
            pl.BlockSpec((GROUP, tq, HEAD_DIM), lambda g, i: (g % 2, (g // 2) * nt + i, 0)),
            kv_spec(kk5, 0), kv_spec(vv5, 0), kv_spec(kk5, 2), kv_spec(vv5, 2),
            pl.BlockSpec((1, tq, n_sel), lambda g, i: (g, i, 0)),
            whole(enear), whole(eslab),
            pl.BlockSpec((GROUP, tq, WIN_SPAN), lambda g, i: (g % 2, 0, 0)),
            pl.BlockSpec((GROUP, tq, NEAR_SPAN), lambda g, i: (g % 2, 0, 0)),
            pl.BlockSpec((tq, GROUP * HEAD_DIM), lambda g, i: ((g // 2) * nt + i, g % 2)),
            pl.BlockSpec((1, tq, LANES), lambda g, i: (g % 2, (g // 2) * nt + i, 0)),
        ],
        out_specs=pl.BlockSpec((tq, GROUP * HEAD_DIM), lambda g, i: ((g // 2) * nt + i, g % 2)),
        out_shape=jax.ShapeDtypeStruct((batch * seq, N_HEADS * HEAD_DIM), F32),
        compiler_params=_params(("parallel", "arbitrary")),
        name="sel_win_attn",
    )(q4, kk5, vv5, kk5, vv5, sel, enear, eslab, winadd, seladd, o_c, gates2)


def _rglru_kernel(xrg_ref, xg_ref, cw_ref, cb_ref, wa_ref, ba_ref, wx_ref, bx_ref, lam_ref, gn_ref,
                  o_ref, xs_ref, h_ref):
    step = pl.program_id(1)
    tc = xrg_ref.shape[0]

    @pl.when(step == 0)
    def _():
        xs_ref[0:8, :] = jnp.zeros((8, xs_ref.shape[1]), F32)
        h_ref[...] = jnp.zeros(h_ref.shape, F32)

    xs_ref[8:, :] = xrg_ref[...]
    xc = cb_ref[...] + cw_ref[CONV_WIDTH - 1:CONV_WIDTH, :] * xs_ref[8:, :]
    for k in range(1, CONV_WIDTH):
        xc = xc + cw_ref[CONV_WIDTH - 1 - k:CONV_WIDTH - k, :] * xs_ref[pl.ds(8 - k, tc), :]
    xs_ref[0:8, :] = xs_ref[tc:tc + 8, :]

    xcb = xc.astype(BF16)
    r = jax.nn.sigmoid(_dot(xcb, wa_ref[...]) + ba_ref[...])
    gi = jax.nn.sigmoid(_dot(xcb, wx_ref[...]) + bx_ref[...])
    lam = lam_ref[...]
    softplus_neg = jnp.maximum(-lam, 0.0) + jnp.log1p(jnp.exp(-jnp.abs(lam)))
    log_a = -LRU_C * r * softplus_neg
    a = jnp.exp(log_a)
    b = jnp.sqrt(1.0 - jnp.exp(2.0 * log_a)) * (gi * xc)

    row = lax.broadcasted_iota(jnp.int32, (tc, 1), 0)
    s = 1
    while s < tc:
        valid = row >= s
        b = jnp.where(valid, a * pltpu.roll(b, s, 0) + b, b)
        a = jnp.where(valid, a * pltpu.roll(a, s, 0), a)
        s *= 2
    h = b + a * h_ref[0:1, :]
    h_ref[0:1, :] = h[tc - 1:tc, :]
    o_ref[...] = _rms(h * jax.nn.gelu(xg_ref[...]), gn_ref[...]).astype(BF16)


def _rglru(xrg, cw, cb, wa, ba, wx, bx, lam, gn, batch, seq, tc):
    nt = seq // tc
    w = cw.shape[1]
    vec = pl.BlockSpec((1, w), lambda b, i: (0, 0))
    return pl.pallas_call(
        _rglru_kernel,
        grid=(batch, nt),
        in_specs=[
            pl.BlockSpec((tc, w), lambda b, i: (b * nt + i, 0)),
            pl.BlockSpec((tc, w), lambda b, i: (b * nt + i, 1)),
            pl.BlockSpec((CONV_WIDTH, w), lambda b, i: (0, 0)),
            vec,
            pl.BlockSpec((w, w), lambda b, i: (0, 0)), vec,
            pl.BlockSpec((w, w), lambda b, i: (0, 0)), vec,
            vec, vec,
        ],
        out_specs=pl.BlockSpec((tc, w), lambda b, i: (b * nt + i, 0)),
        out_shape=jax.ShapeDtypeStruct((batch * seq, w), BF16),
        scratch_shapes=[pltpu.VMEM((tc + 8, w), F32), pltpu.VMEM((8, w), F32)],
        compiler_params=_params(("parallel", "arbitrary")),
        name="rglru",
    )(xrg, xrg, cw, cb, wa, ba, wx, bx, lam, gn)


def _out_proj_kernel(a_ref, l_ref, x_ref, ga_ref, woa_ref, wol_ref, gf_ref, wq_ref, sk_ref,
                     h1_ref, xt_ref, st_ref):
    an = _rms(a_ref[...], ga_ref[...]).astype(BF16)
    h1 = x_ref[...] + _dot(an, woa_ref[...]) + _dot(l_ref[...], wol_ref[...])
    h1_ref[...] = h1
    xn = _rms(h1, gf_ref[...])
    xt_ref[...] = xn.T.astype(BF16)
    q = _dot(xn.astype(BF16), wq_ref[...])
    for hp in range(2 * PEER_HEADS):
        qs = q[:, LANES * hp:LANES * (hp + 1)].astype(BF16)
        st_ref[hp] = _dot_nt(sk_ref[hp % 2], qs)


def _out_proj(a_out, l_n, x2, ga, woa, wol, gf, wq, sk, tm):
    n, d = x2.shape
    full = lambda arr: pl.BlockSpec(arr.shape, lambda i: (0,) * arr.ndim)
    return pl.pallas_call(
        _out_proj_kernel,
        grid=(n // tm,),
        in_specs=[
            pl.BlockSpec((tm, a_out.shape[1]), lambda i: (i, 0)),
            pl.BlockSpec((tm, l_n.shape[1]), lambda i: (i, 0)),
            pl.BlockSpec((tm, d), lambda i: (i, 0)),
            full(ga), full(woa), full(wol), full(gf), full(wq), full(sk),
        ],
        out_specs=[
            pl.BlockSpec((tm, d), lambda i: (i, 0)),
            pl.BlockSpec((d, tm), lambda i: (0, i)),
            pl.BlockSpec((2 * PEER_HEADS, PEER_N_KEYS, tm), lambda i: (0, 0, i)),
        ],
        out_shape=[
            jax.ShapeDtypeStruct((n, d), F32),
            jax.ShapeDtypeStruct((d, n), BF16),
            jax.ShapeDtypeStruct((2 * PEER_HEADS, PEER_N_KEYS, n), F32),
        ],
        compiler_params=_params(("parallel",)),
        name="out_proj",
    )(a_out, l_n, x2, ga, woa, wol, gf, wq, sk)


N_TOP = PEER_TOPK
_PAIRS = [(i, j) for i in range(N_TOP) for j in range(N_TOP) if (i + 1) * (j + 1) <= N_TOP]


def _top_values(s):
    n, tl = s.shape
    parts = [s[SUBLANES * k:SUBLANES * (k + 1)] for k in range(n // SUBLANES)]
    base = lax.broadcasted_iota(jnp.int32, (SUBLANES, tl), 0).astype(F32)
    rows = [base + float(SUBLANES * k) for k in range(len(parts))]
    out = []
    for r in range(N_TOP):
        m = jnp.max(functools.reduce(jnp.maximum, parts), axis=0, keepdims=True)
        out.append(m)
        first = jnp.full((SUBLANES, tl), float(n), F32)
        for k in reversed(range(len(parts))):
            first = jnp.where(parts[k] == m, rows[k], first)
        first = jnp.min(first, axis=0, keepdims=True)
        taken = TAKEN * (1.0 + r / TAKEN_STEPS)
        parts = [jnp.where(rows[k] == first, taken, parts[k]) for k in range(len(parts))]
    rank = jnp.concatenate(parts, axis=0)
    rank = jnp.where(rank <= TAKEN, (rank * (1.0 / TAKEN) - 1.0) * TAKEN_STEPS, float(PEER_TOPK))
    return out, rank


def _peer_stats_kernel(st_ref, r2_ref, e2_ref, l_ref, e1_ref):
    tops = [[], []]
    rank1 = []
    for h in range(PEER_HEADS):
        vals, rank = _top_values(st_ref[2 * h])
        tops[0].append(vals)
        rank1.append(rank)
        vals, rank = _top_values(st_ref[2 * h + 1])
        tops[1].append(vals)
        r2_ref[h] = rank.astype(BF16)
    a = [jnp.concatenate([tops[0][h][i] for h in range(PEER_HEADS)], axis=0) for i in range(N_TOP)]
    b = [jnp.concatenate([tops[1][h][j] for h in range(PEER_HEADS)], axis=0) for j in range(N_TOP)]
    cand = [a[i] + b[j] for (i, j) in _PAIRS]
    work = list(cand)
    kth = []
    for _ in range(N_TOP):
        m = functools.reduce(jnp.maximum, work)
        kth.append(m)
        found = jnp.zeros(m.shape, jnp.bool_)
        for k in range(len(work)):
            hit = (work[k] == m) & jnp.logical_not(found)
            work[k] = jnp.where(hit, REMOVED, work[k])
            found = found | hit
    thr = kth[PEER_TOPK - 1]
    top = a[0] + b[0]
    z = functools.reduce(lambda u, w: u + w, [jnp.where(c >= thr, jnp.exp(c - top), 0.0) for c in cand])
    n_sel = [functools.reduce(lambda u, w: u + w,
                              [(a[r] + b[j] >= thr).astype(F32) for j in range(PEER_TOPK)])
             for r in range(PEER_TOPK)]
    for h in range(PEER_HEADS):
        row = slice(h, h + 1)
        prefix = jnp.zeros(rank1[h].shape, F32)
        for r in range(PEER_TOPK):
            prefix = jnp.where(rank1[h] == float(r), n_sel[r][row], prefix)
        l_ref[h] = prefix
        e1_ref[h] = jnp.exp(st_ref[2 * h] - a[0][row])
        e2_ref[h] = (jnp.exp(st_ref[2 * h + 1] - b[0][row]) / z[row]).astype(BF16)


def _peer_stats(st, tl):
    hp, nk, n = st.shape
    out_spec = pl.BlockSpec((PEER_HEADS, nk, tl), lambda i: (0, 0, i))
    return pl.pallas_call(
        _peer_stats_kernel,
        grid=(n // tl,),
        in_specs=[pl.BlockSpec((hp, nk, tl), lambda i: (0, 0, i))],
        out_specs=[out_spec] * 4,
        out_shape=[
            jax.ShapeDtypeStruct((PEER_HEADS, nk, n), BF16),
            jax.ShapeDtypeStruct((PEER_HEADS, nk, n), BF16),
            jax.ShapeDtypeStruct((PEER_HEADS, nk, n), F32),
            jax.ShapeDtypeStruct((PEER_HEADS, nk, n), F32),
        ],
        compiler_params=_params(("parallel",)),
        name="peer_stats",
    )(st)


def _peer_main_kernel(u_ref, vt_ref, xt_ref, r2_ref, e2_ref, l_ref, e1_ref, h1_ref, o_ref, acc_ref):
    c = pl.program_id(1)
    ce = u_ref.shape[0]
    tm = xt_ref.shape[1]
    nk = PEER_N_KEYS

    @pl.when(c == 0)
    def _():
        acc_ref[...] = jnp.zeros(acc_ref.shape, F32)

    act = _gelu_tanh(_dot(u_ref[...], xt_ref[...]))

    def rows_bf16(ref, h, i1):
        packed = jnp.broadcast_to(ref[h, pl.ds(i1, 1), :], (BF16_ROWS, tm)).astype(BF16)
        return jnp.concatenate([packed] * (nk // BF16_ROWS), axis=0)

    w_parts = []
    for ii in range(ce // nk):
        i1 = c * (ce // nk) + ii
        g = None
        for h in range(PEER_HEADS):
            e2 = e2_ref[h]
            term = jnp.where(r2_ref[h] < rows_bf16(l_ref, h, i1), e2, jnp.zeros_like(e2)) * rows_bf16(e1_ref, h, i1)
            g = term if g is None else g + term
        w_parts.append(act[ii * nk:(ii + 1) * nk, :].astype(BF16) * g)
    w = jnp.concatenate(w_parts, axis=0)
    acc_ref[...] += _dot(vt_ref[...], w)

    @pl.when(c == pl.num_programs(1) - 1)
    def _():
        o_ref[...] = h1_ref[...] + acc_ref[...].T


def _peer_main(u_b, vt_b, xt, r2, e2, n_sel, e1, h1, tm, ce):
    ne, d = u_b.shape
    n = xt.shape[1]
    per_tok = pl.BlockSpec((PEER_HEADS, PEER_N_KEYS, tm), lambda t, c: (0, 0, t))
    return pl.pallas_call(
        _peer_main_kernel,
        grid=(n // tm, ne // ce),
        in_specs=[
            pl.BlockSpec((ce, d), lambda t, c: (c, 0)),
            pl.BlockSpec((d, ce), lambda t, c: (0, c)),
            pl.BlockSpec((d, tm), lambda t, c: (0, t)),
            per_tok, per_tok, per_tok, per_tok,
            pl.BlockSpec((tm, d), lambda t, c: (t, 0)),
        ],
        out_specs=pl.BlockSpec((tm, d), lambda t, c: (t, 0)),
        out_shape=jax.ShapeDtypeStruct((n, d), F32),
        scratch_shapes=[pltpu.VMEM((d, tm), F32)],
        compiler_params=_params(("parallel", "arbitrary")),
        name="peer_main",
    )(u_b, vt_b, xt, r2, e2, n_sel, e1, h1)


def _ple_final_kernel(h_ref, p_ref, gp_ref, wg_ref, bg_ref, wp_ref, gfin_ref, o_ref):
    h = h_ref[...]
    gate = jax.nn.sigmoid(_dot(_rms(h, gp_ref[...]).astype(BF16), wg_ref[...]) + bg_ref[...])
    h = h + gate * _dot(p_ref[...].astype(BF16), wp_ref[...])
    o_ref[...] = _rms(h, gfin_ref[...])


def _ple_final(h2, p2, gp, wg, bg, wp, gfin, tm):
    n, d = h2.shape
    full = lambda arr: pl.BlockSpec(arr.shape, lambda i: (0,) * arr.ndim)
    return pl.pallas_call(
        _ple_final_kernel,
        grid=(n // tm,),
        in_specs=[
            pl.BlockSpec((tm, d), lambda i: (i, 0)),
            pl.BlockSpec((tm, p2.shape[1]), lambda i: (i, 0)),
            full(gp), full(wg), full(bg), full(wp), full(gfin),
        ],
        out_specs=pl.BlockSpec((tm, d), lambda i: (i, 0)),
        out_shape=jax.ShapeDtypeStruct((n, d), F32),
        compiler_params=_params(("parallel",)),
        name="ple_final",
    )(h2, p2, gp, wg, bg, wp, gfin)


def _overlap_t(seq, n_rows):
    n_cmp = n_rows - 1
    cmp_start = np.arange(n_cmp) * CMP_STRIDE
    sel_start = np.arange(seq // SEL_LEN) * SEL_LEN
    ov = np.clip(np.minimum(cmp_start[:, None] + CMP_LEN, sel_start[None, :] + SEL_LEN)
                 - np.maximum(cmp_start[:, None], sel_start[None, :]), 0, None).astype(np.float32) / CMP_LEN
    out = np.zeros((seq // SEL_LEN, n_rows), np.float32)
    out[:, :n_cmp] = ov.T
    return out


def _block_expand(seq, starts, width):
    pos = np.asarray(starts).reshape(-1, 1, 1) + np.arange(width).reshape(1, 1, width)
    blk = np.where(pos >= 0, pos // SEL_LEN, -1)
    return (blk == np.arange(seq // SEL_LEN).reshape(1, -1, 1)).astype(np.float32)


def _block_diag(w):
    nb, bd, _ = w.shape
    out = jnp.zeros((nb * bd, nb * bd), w.dtype)
    for j in range(nb):
        out = out.at[j * bd:(j + 1) * bd, j * bd:(j + 1) * bd].set(w[j])
    return out


def _layer(h, p_i, attn_norm, w_in, cmp_k_pe, cmp_k_w1, cmp_k_w2, cmp_v_pe, cmp_v_w1, cmp_v_w2,
           rel_table, conv_w, conv_b, lru_wa, lru_ba, lru_wx, lru_bx, lru_lambda,
           grp_norm_attn, grp_norm_lru, w_out, ffn_norm, peer_wq, peer_subkeys, peer_u, peer_v,
           ple_norm, ple_wgate, ple_bgate, ple_proj, *, tm, tc, tl, ce):
    batch, seq, d = h.shape
    n = batch * seq
    x2 = h.reshape(n, d)
    aw = N_HEADS * HEAD_DIM
    kvw = N_KV_HEADS * HEAD_DIM
    lw = d - aw

    o = 0
    cols = {}
    for name, width in (("q", aw), ("kc", kvw), ("vc", kvw), ("ks", kvw), ("vs", kvw), ("kw", kvw),
                        ("vw", kvw), ("gates", N_HEADS * 3), ("xr", lw), ("xg", lw)):
        cols[name] = w_in[:, o:o + width]
        o += width
    gate_pad = [jnp.pad(cols["gates"][:, N_GATE_COLS * k:N_GATE_COLS * (k + 1)], ((0, 0), (0, LANES - N_GATE_COLS)))
                for k in range(N_KV_HEADS)]
    wf = jnp.concatenate([cols["xr"], cols["xg"], cols["kc"], cols["vc"]] + gate_pad, axis=1).astype(BF16)
    wb = jnp.concatenate([cols["q"], cols["ks"], cols["kw"], cols["vs"], cols["vw"]], axis=1).astype(BF16)

    xrg, kvc, gates2, q4, kk, vv = _in_proj(x2, attn_norm.reshape(1, d), wf, wb, tm)

    n_rows = seq // CMP_STRIDE
    r4 = kvc.reshape(4, batch, n_rows, CMP_STRIDE * HEAD_DIM)
    pe2 = jnp.stack([cmp_k_pe.reshape(1, -1), cmp_v_pe.reshape(1, -1)])
    w1s = jnp.stack([cmp_k_w1, cmp_v_w1]).astype(BF16)
    w2s = jnp.stack([cmp_k_w2, cmp_v_w2]).astype(BF16)
    cmp4 = _compress(r4, pe2, w1s, w2s)

    winadd, seladd = _bias_near(rel_table)
    addc = _bias_cmp(rel_table, seq, n_rows, min(seq, 512))
    ovlt = jnp.asarray(_overlap_t(seq, n_rows), BF16)
    o_c, sel = _cmp_attn(q4, cmp4, addc, ovlt, batch, seq)

    nt = seq // ATT_TILE
    tile_starts = np.arange(nt) * ATT_TILE
    enear = jnp.asarray(_block_expand(seq, tile_starts - ATT_TILE, NEAR_SPAN), BF16)
    eslab = jnp.asarray(_block_expand(seq, np.arange(max(seq // FAR_SLAB, 1)) * FAR_SLAB, FAR_SLAB), BF16)
    front = ((0, 0), (0, 0), (WINDOW, 0), (0, 0))
    kk5 = jnp.pad(kk.reshape(4, batch, seq, HEAD_DIM), front)
    vv5 = jnp.pad(vv.reshape(4, batch, seq, LANES), front)
    a_out = _sel_win_attn(q4, kk5, vv5, sel, enear, eslab, winadd, seladd, o_c, gates2, batch, seq)

    l_n = _rglru(xrg, conv_w.reshape(CONV_WIDTH, lw), conv_b.reshape(1, lw),
                 _block_diag(lru_wa).astype(BF16), lru_ba.reshape(1, lw),
                 _block_diag(lru_wx).astype(BF16), lru_bx.reshape(1, lw),
                 lru_lambda.reshape(1, lw), grp_norm_lru.reshape(1, lw), batch, seq, tc)

    w_out_b = w_out.astype(BF16)
    h1, xt, st = _out_proj(a_out, l_n, x2, grp_norm_attn.reshape(1, aw), w_out_b[:aw], w_out_b[aw:],
                           ffn_norm.reshape(1, d), peer_wq.astype(BF16), peer_subkeys.astype(BF16), tm)
    r2, e2, n_sel, e1 = _peer_stats(st, tl)
    h2 = _peer_main(peer_u.astype(BF16), peer_v.T.astype(BF16), xt, r2, e2, n_sel, e1, h1, tm, ce)
    return (h2, p_i.reshape(n, -1), ple_norm.reshape(1, d), ple_wgate.astype(BF16), ple_bgate.reshape(1, d),
            ple_proj.astype(BF16))


def kernel(x, p, attn_norm, w_in, cmp_k_pe, cmp_k_w1, cmp_k_w2, cmp_v_pe, cmp_v_w1, cmp_v_w2, rel_table, conv_w, conv_b, lru_wa, lru_ba, lru_wx, lru_bx, lru_lambda, grp_norm_attn, grp_norm_lru, w_out, ffn_norm, peer_wq, peer_subkeys, peer_u, peer_v, ple_norm, ple_wgate, ple_bgate, ple_proj, final_norm):
    batch, seq, d = x.shape
    assert w_in.shape[0] == 1, "single-layer stack only"
    tm = min(512, seq)
    h2, p2, gp, wg, bg, wp = _layer(
        x, p[0], attn_norm[0], w_in[0], cmp_k_pe[0], cmp_k_w1[0], cmp_k_w2[0], cmp_v_pe[0], cmp_v_w1[0],
        cmp_v_w2[0], rel_table, conv_w[0], conv_b[0], lru_wa[0], lru_ba[0], lru_wx[0], lru_bx[0],
        lru_lambda[0], grp_norm_attn[0], grp_norm_lru[0], w_out[0], ffn_norm[0], peer_wq[0],
        peer_subkeys[0], peer_u[0], peer_v[0], ple_norm[0], ple_wgate[0], ple_bgate[0], ple_proj[0],
        tm=tm, tc=min(256, seq), tl=min(256, seq), ce=2048)
    out = _ple_final(h2, p2, gp, wg, bg, wp, final_norm.reshape(1, d), tm)
    return out.reshape(batch, seq, d)
```

```python
import functools
import math

import numpy as np
import jax
import jax.numpy as jnp
from jax import lax
from jax.experimental import pallas as pl
from jax.experimental.pallas import tpu as pltpu

F32 = jnp.float32
BF16 = jnp.bfloat16

N_HEADS = 8
HEAD_DIM = 64
N_KV_HEADS = 2
GROUP = N_HEADS // N_KV_HEADS
LRU_C = 8.0
CONV_WIDTH = 4
CMP_LEN = 32
CMP_STRIDE = 16
SEL_LEN = 64
SEL_TOPK = 16
N_LOCAL_BLOCKS = 2
WINDOW = 512
FORCE_LOCAL = 2.0e4
FORCE_INIT = 1.0e4
REL_BUCKETS = 32
REL_MAX_DIST = 128
PEER_HEADS = 8
PEER_N_KEYS = 128
PEER_TOPK = 16
EPS = 1e-6
NEG = -1e30
LOG2E = math.log2(math.e)
Q_SCALE = HEAD_DIM ** -0.5 * LOG2E
REMOVED = -3.0e38
TAKEN = -2.0 ** 120
TAKEN_STEPS = 32.0

LANES = 128
SUBLANES = 8
BF16_ROWS = 16
VMEM_LIMIT = 56 * 1024 * 1024

ATT_TILE = 256
CMP_TILE = 512
N_GATE_COLS = GROUP * 3


def _dot(a, b):
    return jnp.dot(a, b, preferred_element_type=F32)


def _dot_nt(a, b):
    return lax.dot_general(a, b, (((1,), (1,)), ((), ())), preferred_element_type=F32)


def _gelu_tanh(x):
    c = math.sqrt(2.0 / math.pi)
    half = 0.5 * x
    return half + half * jnp.tanh(x * (c + (c * 0.044715) * (x * x)))


def _rms(x, g):
    return x * lax.rsqrt(jnp.mean(x * x, axis=-1, keepdims=True) + EPS) * g


def _params(sem, vmem=VMEM_LIMIT):
    return pltpu.CompilerParams(dimension_semantics=sem, vmem_limit_bytes=vmem)


def _in_proj_kernel(x_ref, g_ref, wf_ref, wb_ref, xrg_ref, kvc_ref, gates_ref, q4_ref, kk_ref, vv_ref):
    dh = HEAD_DIM
    n_kv = 2 * N_KV_HEADS
    xb = _rms(x_ref[...], g_ref[...]).astype(BF16)
    yf = _dot(xb, wf_ref[...])
    kv0 = xrg_ref.shape[1]
    g0 = kv0 + n_kv * dh
    xrg_ref[...] = yf[:, :kv0]
    for j in range(n_kv):
        kvc_ref[j] = yf[:, kv0 + dh * j:kv0 + dh * (j + 1)]
    for k in range(N_KV_HEADS):
        gates_ref[k] = yf[:, g0 + LANES * k:g0 + LANES * (k + 1)]
    yb = _dot(xb, wb_ref[...])
    for h in range(N_HEADS):
        q4_ref[h] = (yb[:, dh * h:dh * (h + 1)] * Q_SCALE).astype(BF16)
    k0 = N_HEADS * dh
    v0 = k0 + n_kv * dh
    ones_col = (lax.broadcasted_iota(jnp.int32, (yb.shape[0], LANES - dh), 1) == 0).astype(F32)
    for j in range(n_kv):
        kk_ref[j] = yb[:, k0 + dh * j:k0 + dh * (j + 1)].astype(BF16)
        vv_ref[j] = jnp.concatenate([yb[:, v0 + dh * j:v0 + dh * (j + 1)], ones_col], axis=1).astype(BF16)


def _in_proj(x2, g, wf, wb, tm):
    n, d = x2.shape
    dh = HEAD_DIM
    n_kv = 2 * N_KV_HEADS
    lru2 = wf.shape[1] - n_kv * dh - N_KV_HEADS * LANES
    shapes = [
        ((n, lru2), F32),
        ((n_kv, n, dh), F32),
        ((N_KV_HEADS, n, LANES), F32),
        ((N_HEADS, n, dh), BF16),
        ((n_kv, n, dh), BF16),
        ((n_kv, n, LANES), BF16),
    ]

    def spec(shape):
        if len(shape) == 2:
            return pl.BlockSpec((tm, shape[1]), lambda i: (i, 0))
        return pl.BlockSpec((shape[0], tm, shape[2]), lambda i: (0, i, 0))

    return pl.pallas_call(
        _in_proj_kernel,
        grid=(n // tm,),
        in_specs=[
            pl.BlockSpec((tm, d), lambda i: (i, 0)),
            pl.BlockSpec((1, d), lambda i: (0, 0)),
            pl.BlockSpec(wf.shape, lambda i: (0, 0)),
            pl.BlockSpec(wb.shape, lambda i: (0, 0)),
        ],
        out_specs=[spec(shape) for shape, _ in shapes],
        out_shape=[jax.ShapeDtypeStruct(shape, dtype) for shape, dtype in shapes],
        compiler_params=_params(("parallel",)),
        name="in_proj",
    )(x2, g, wf, wb)


def _compress_kernel(r_ref, pe_ref, w1_ref, w2_ref, o_ref):
    r = r_ref[0, 0]
    n_rows = r.shape[0]
    half = r.shape[1]
    pe = pe_ref[0]
    top = _dot((r + pe[:, :half]).astype(BF16), w1_ref[0, :half, :])
    bot = _dot((r + pe[:, half:]).astype(BF16), w1_ref[0, half:, :])
    hid = top + pltpu.roll(bot, n_rows - 1, 0)
    act = jax.nn.gelu(hid)
    out = _dot(act.astype(BF16), w2_ref[0])
    row = lax.broadcasted_iota(jnp.int32, out.shape, 0)
    o_ref[0, 0] = jnp.where(row < n_rows - 1, out, 0.0).astype(BF16)


def _compress(r4, pe2, w1s, w2s):
    four, b, n_rows, width = r4.shape
    return pl.pallas_call(
        _compress_kernel,
        grid=(four, b),
        in_specs=[
            pl.BlockSpec((1, 1, n_rows, width), lambda j, bb: (j, bb, 0, 0)),
            pl.BlockSpec((1, 1, 2 * width), lambda j, bb: (j // 2, 0, 0)),
            pl.BlockSpec((1, 2 * width, w1s.shape[2]), lambda j, bb: (j // 2, 0, 0)),
            pl.BlockSpec((1, w2s.shape[1], HEAD_DIM), lambda j, bb: (j // 2, 0, 0)),
        ],
        out_specs=pl.BlockSpec((1, 1, n_rows, HEAD_DIM), lambda j, bb: (j, bb, 0, 0)),
        out_shape=jax.ShapeDtypeStruct((four, b, n_rows, HEAD_DIM), BF16),
        compiler_params=_params(("parallel", "parallel")),
        name="compress",
    )(r4, pe2, w1s, w2s)


def _bucket_breakpoints():
    n = np.arange(0, 4 * REL_MAX_DIST)
    max_exact = REL_BUCKETS // 2
    nf = np.maximum(n, max_exact).astype(np.float32)
    large = max_exact + (np.log(nf / max_exact) / np.float32(math.log(REL_MAX_DIST / max_exact))
                         * (REL_BUCKETS - max_exact)).astype(np.int32)
    large = np.minimum(large, REL_BUCKETS - 1)
    bucket = np.where(n < max_exact, n, large)
    return [int(np.argmax(bucket >= j)) for j in range(1, REL_BUCKETS)]


_BREAKS = _bucket_breakpoints()


def _bias_of_dist(dist, tab_ref, h):
    v = jnp.full(dist.shape, tab_ref[0, h], F32)
    for j, bp in enumerate(_BREAKS, start=1):
        v = jnp.where(dist >= bp, tab_ref[j, h], v)
    v = (v - tab_ref[REL_BUCKETS - 1, h]) * LOG2E
    return jnp.where(dist >= 0, v, NEG)


WIN_SPAN = WINDOW + ATT_TILE
NEAR_SPAN = 2 * ATT_TILE


def _bias_near_kernel(tab_ref, win_ref, sel_ref):
    h = pl.program_id(0)
    t = ATT_TILE
    for ref, span in ((win_ref, WIN_SPAN), (sel_ref, NEAR_SPAN)):
        r = lax.broadcasted_iota(jnp.int32, (t, span), 0)
        c = lax.broadcasted_iota(jnp.int32, (t, span), 1)
        dist = r - c + (span - t)
        ref[0] = jnp.where(dist < WINDOW, _bias_of_dist(dist, tab_ref, h), NEG)


def _bias_near(rel_table):
    t = ATT_TILE
    return pl.pallas_call(
        _bias_near_kernel,
        grid=(N_HEADS,),
        in_specs=[pl.BlockSpec(memory_space=pltpu.SMEM)],
        out_specs=[pl.BlockSpec((1, t, WIN_SPAN), lambda h: (h, 0, 0)),
                   pl.BlockSpec((1, t, NEAR_SPAN), lambda h: (h, 0, 0))],
        out_shape=[jax.ShapeDtypeStruct((N_HEADS, t, WIN_SPAN), F32),
                   jax.ShapeDtypeStruct((N_HEADS, t, NEAR_SPAN), F32)],
        compiler_params=_params(("arbitrary",)),
        name="bias_near",
    )(rel_table)


def _bias_cmp_kernel(tab_ref, o_ref, *, n_cmp):
    h = pl.program_id(0)
    i = pl.program_id(1)
    tq, nc = o_ref.shape[1], o_ref.shape[2]
    span = 2 * nc
    b = lax.broadcasted_iota(jnp.int32, (CMP_STRIDE, span), 0)
    u = lax.broadcasted_iota(jnp.int32, (CMP_STRIDE, span), 1)
    table = _bias_of_dist(CMP_STRIDE * (nc - 1 - u) + b - (CMP_LEN - 1), tab_ref, h)
    col = lax.broadcasted_iota(jnp.int32, (CMP_STRIDE, nc), 1)
    groups = tq // CMP_STRIDE
    for k in range(groups):
        a = i * groups + k
        win = pltpu.roll(table, (a + span - (nc - 1)) % span, 1)[:, :nc]
        o_ref[0, CMP_STRIDE * k:CMP_STRIDE * (k + 1), :] = jnp.where(col < n_cmp, win, NEG)


def _bias_cmp(rel_table, seq, n_rows, tq):
    return pl.pallas_call(
        functools.partial(_bias_cmp_kernel, n_cmp=n_rows - 1),
        grid=(N_HEADS, seq // tq),
        in_specs=[pl.BlockSpec(memory_space=pltpu.SMEM)],
        out_specs=pl.BlockSpec((1, tq, n_rows), lambda h, i: (h, i, 0)),
        out_shape=jax.ShapeDtypeStruct((N_HEADS, seq, n_rows), F32),
        compiler_params=_params(("arbitrary", "arbitrary")),
        name="bias_cmp",
    )(rel_table)


def _cmp_attn_kernel(q_ref, kc_ref, vc_ref, addc_ref, ovlt_ref, eye_ref, oc_ref, sel_ref):
    i = pl.program_id(1)
    tq = q_ref.shape[1]
    n_sel = ovlt_ref.shape[0]
    kc = kc_ref[0, 0]
    vc = vc_ref[0, 0]
    t_col = lax.broadcasted_iota(jnp.int32, (tq, 1), 0) + i * tq
    row_ok = (t_col >= CMP_LEN - 1).astype(F32)
    psum = None
    for hh in range(GROUP):
        logit = _dot_nt(q_ref[hh], kc) + addc_ref[hh]
        m = jnp.max(logit, axis=-1, keepdims=True)
        e = jnp.exp2(logit - m)
        p = e / jnp.sum(e, axis=-1, keepdims=True)
        oc_ref[:, HEAD_DIM * hh:HEAD_DIM * (hh + 1)] = _dot(p.astype(BF16), vc) * row_ok
        psum = p if psum is None else psum + p
    psum = psum * row_ok
    hi = psum.astype(BF16)
    lo = (psum - hi.astype(F32)).astype(BF16)
    ovlt = ovlt_ref[...]
    imp = _dot_nt(ovlt, hi) + _dot_nt(ovlt, lo)
    s_idx = lax.broadcasted_iota(jnp.int32, (n_sel, tq), 0)
    t_blk = (lax.broadcasted_iota(jnp.int32, (n_sel, tq), 1) + i * tq) // SEL_LEN
    d_blk = t_blk - s_idx
    local = (d_blk >= 0) & (d_blk < N_LOCAL_BLOCKS)
    v = jnp.where(local, FORCE_LOCAL,
                  jnp.where(s_idx == 0, FORCE_INIT, jnp.where(d_blk >= 0, imp, -1.0)))
    rows = s_idx.astype(F32)
    sel = jnp.zeros((n_sel, tq), F32)
    for _ in range(min(SEL_TOPK, n_sel)):
        m = jnp.max(v, axis=0, keepdims=True)
        first = jnp.min(jnp.where(v == m, rows, float(n_sel)), axis=0, keepdims=True)
        pick = rows == first
        sel = jnp.where(pick, 1.0, sel)
        v = jnp.where(pick, REMOVED, v)
    sel_ref[0] = _dot_nt(eye_ref[...], sel.astype(BF16)).astype(BF16)


def _cmp_attn(q4, cmp4, addc, ovlt, batch, seq):
    tq = min(CMP_TILE, seq)
    nt = seq // tq
    n_rows = cmp4.shape[2]
    n_sel = ovlt.shape[0]
    eye = jnp.eye(tq, dtype=BF16)
    return pl.pallas_call(
        _cmp_attn_kernel,
        grid=(batch * N_KV_HEADS, nt),
        in_specs=[
            pl.BlockSpec((GROUP, tq, HEAD_DIM), lambda g, i: (g % 2, (g // 2) * nt + i, 0)),
            pl.BlockSpec((1, 1, n_rows, HEAD_DIM), lambda g, i: (g % 2, g // 2, 0, 0)),
            pl.BlockSpec((1, 1, n_rows, HEAD_DIM), lambda g, i: (2 + g % 2, g // 2, 0, 0)),
            pl.BlockSpec((GROUP, tq, n_rows), lambda g, i: (g % 2, i, 0)),
            pl.BlockSpec(ovlt.shape, lambda g, i: (0, 0)),
            pl.BlockSpec((tq, tq), lambda g, i: (0, 0)),
        ],
        out_specs=[
            pl.BlockSpec((tq, GROUP * HEAD_DIM), lambda g, i: ((g // 2) * nt + i, g % 2)),
            pl.BlockSpec((1, tq, n_sel), lambda g, i: (g, i, 0)),
        ],
        out_shape=[
            jax.ShapeDtypeStruct((batch * seq, N_HEADS * HEAD_DIM), F32),
            jax.ShapeDtypeStruct((batch * N_KV_HEADS, seq, n_sel), BF16),
        ],
        compiler_params=_params(("parallel", "arbitrary")),
        name="cmp_attn",
    )(q4, cmp4, cmp4, addc, ovlt, eye)


def _softmax_step(qs, k, v1, mask, add, carry):
    tq = qs.shape[0] // GROUP
    s = _dot_nt(qs, k).reshape(GROUP, tq, k.shape[0])
    if mask is not None:
        s = jnp.where(mask[None] > 0.5, s, NEG)
    if add is not None:
        s = s + add
    m_tile = jnp.max(s, axis=-1, keepdims=True)
    if carry is None:
        m_new = m_tile
    else:
        m, acc = carry
        m_new = jnp.maximum(m, m_tile)
    p = jnp.exp2(s - m_new)
    pv = _dot(p.reshape(GROUP * tq, k.shape[0]).astype(BF16), v1).reshape(GROUP, tq, v1.shape[1])
    if carry is None:
        return m_new, pv
    return m_new, jnp.exp2(m - m_new) * acc + pv


FAR_SLAB = 4 * ATT_TILE


def _sel_win_kernel(q_ref, ks_ref, vs_ref, kw_ref, vw_ref, sel_ref, enear_ref, eslab_ref,
                    winadd_ref, seladd_ref, oc_ref, gates_ref, o_ref):
    i = pl.program_id(1)
    tq = q_ref.shape[1]
    qs = q_ref[...].reshape(GROUP * tq, HEAD_DIM)
    sel = sel_ref[0]

    def rows(ref, start, size):
        return ref[0, 0, pl.ds(pl.multiple_of(start, ATT_TILE), size), :]

    def hide_padding(add, n_pad):
        col = lax.broadcasted_iota(jnp.int32, (1, 1, add.shape[2]), 2)
        return jnp.where(col < n_pad, NEG, add)

    w0 = i * tq
    _, acc_w = _softmax_step(qs, rows(kw_ref, w0, WIN_SPAN), rows(vw_ref, w0, WIN_SPAN), None,
                             hide_padding(winadd_ref[...], WINDOW - i * tq), None)

    n0 = w0 + WINDOW - tq
    carry = _softmax_step(qs, rows(ks_ref, n0, NEAR_SPAN), rows(vs_ref, n0, NEAR_SPAN),
                          _dot(sel, enear_ref[i]), hide_padding(seladd_ref[...], tq - i * tq), None)
    far_end = jnp.maximum(i - 1, 0) * tq
    col = lax.broadcasted_iota(jnp.int32, (1, FAR_SLAB), 1)

    def far_slab(j, c):
        start = WINDOW + j * FAR_SLAB
        mask = jnp.where(col < far_end - j * FAR_SLAB, _dot(sel, eslab_ref[j]), 0.0)
        return _softmax_step(qs, rows(ks_ref, start, FAR_SLAB), rows(vs_ref, start, FAR_SLAB), mask, None, c)

    _, acc_s = lax.fori_loop(0, (far_end + FAR_SLAB - 1) // FAR_SLAB, far_slab, carry)
    o_s = acc_s[..., :HEAD_DIM] / acc_s[..., HEAD_DIM:HEAD_DIM + 1]
    o_w = acc_w[..., :HEAD_DIM] / acc_w[..., HEAD_DIM:HEAD_DIM + 1]

    g = jax.nn.sigmoid(gates_ref[0])
    for hh in range(GROUP):
        o_c = oc_ref[:, HEAD_DIM * hh:HEAD_DIM * (hh + 1)]
        o_ref[:, HEAD_DIM * hh:HEAD_DIM * (hh + 1)] = (g[:, 3 * hh:3 * hh + 1] * o_c
                                            + g[:, 3 * hh + 1:3 * hh + 2] * o_s[hh]
                                            + g[:, 3 * hh + 2:3 * hh + 3] * o_w[hh])


def _sel_win_attn(q4, kk5, vv5, sel, enear, eslab, winadd, seladd, o_c, gates2, batch, seq):
    tq = ATT_TILE
    nt = seq // tq
    n_sel = sel.shape[2]
    padded = kk5.shape[2]

    def kv_spec(arr, base):
        return pl.BlockSpec((1, 1, padded, arr.shape[3]), lambda g, i: (base + g % 2, g // 2, 0, 0))

    whole = lambda arr: pl.BlockSpec(arr.shape, lambda g, i: (0,) * arr.ndim)
    return pl.pallas_call(
        _sel_win_kernel,
        grid=(batch * N_KV_HEADS, nt),
        in_specs=[
            pl.BlockSpec((GROUP, tq, HEAD_DIM), lambda g, i: (g % 2, (g // 2) * nt + i, 0)),
            kv_spec(kk5, 0), kv_spec(vv5, 0), kv_spec(kk5, 2), kv_spec(vv5, 2),
            pl.BlockSpec((1, tq, n_sel), lambda g, i: (g, i, 0)),
            whole(enear), whole(eslab),
            pl.BlockSpec((GROUP, tq, WIN_SPAN), lambda g, i: (g % 2, 0, 0)),
            pl.BlockSpec((GROUP, tq, NEAR_SPAN), lambda g, i: (g % 2, 0, 0)),
            pl.BlockSpec((tq, GROUP * HEAD_DIM), lambda g, i: ((g // 2) * nt + i, g % 2)),
            pl.BlockSpec((1, tq, LANES), lambda g, i: (g % 2, (g // 2) * nt + i, 0)),
        ],
        out_specs=pl.BlockSpec((tq, GROUP * HEAD_DIM), lambda g, i: ((g // 2) * nt + i, g % 2)),
        out_shape=jax.ShapeDtypeStruct((batch * seq, N_HEADS * HEAD_DIM), F32),
        compiler_params=_params(("parallel", "arbitrary")),
        name="sel_win_attn",
    )(q4, kk5, vv5, kk5, vv5, sel, enear, eslab, winadd, seladd, o_c, gates2)


def _rglru_kernel(xrg_ref, xg_ref, cw_ref, cb_ref, wa_ref, ba_ref, wx_ref, bx_ref, lam_ref, gn_ref,
                  o_ref, xs_ref, h_ref):
    step = pl.program_id(1)
    tc = xrg_ref.shape[0]

    @pl.when(step == 0)
    def _():
        xs_ref[0:8, :] = jnp.zeros((8, xs_ref.shape[1]), F32)
        h_ref[...] = jnp.zeros(h_ref.shape, F32)

    xs_ref[8:, :] = xrg_ref[...]
    xc = cb_ref[...] + cw_ref[CONV_WIDTH - 1:CONV_WIDTH, :] * xs_ref[8:, :]
    for k in range(1, CONV_WIDTH):
        xc = xc + cw_ref[CONV_WIDTH - 1 - k:CONV_WIDTH - k, :] * xs_ref[pl.ds(8 - k, tc), :]
    xs_ref[0:8, :] = xs_ref[tc:tc + 8, :]

    xcb = xc.astype(BF16)
    r = jax.nn.sigmoid(_dot(xcb, wa_ref[...]) + ba_ref[...])
    gi = jax.nn.sigmoid(_dot(xcb, wx_ref[...]) + bx_ref[...])
    lam = lam_ref[...]
    softplus_neg = jnp.maximum(-lam, 0.0) + jnp.log1p(jnp.exp(-jnp.abs(lam)))
    log_a = -LRU_C * r * softplus_neg
    a = jnp.exp(log_a)
    b = jnp.sqrt(1.0 - jnp.exp(2.0 * log_a)) * (gi * xc)

    row = lax.broadcasted_iota(jnp.int32, (tc, 1), 0)
    s = 1
    while s < tc:
        valid = row >= s
        b = jnp.where(valid, a * pltpu.roll(b, s, 0) + b, b)
        a = jnp.where(valid, a * pltpu.roll(a, s, 0), a)
        s *= 2
    h = b + a * h_ref[0:1, :]
    h_ref[0:1, :] = h[tc - 1:tc, :]
    o_ref[...] = _rms(h * jax.nn.gelu(xg_ref[...]), gn_ref[...]).astype(BF16)


def _rglru(xrg, cw, cb, wa, ba, wx, bx, lam, gn, batch, seq, tc):
    nt = seq // tc
    w = cw.shape[1]
    vec = pl.BlockSpec((1, w), lambda b, i: (0, 0))
    return pl.pallas_call(
        _rglru_kernel,
        grid=(batch, nt),
        in_specs=[
            pl.BlockSpec((tc, w), lambda b, i: (b * nt + i, 0)),
            pl.BlockSpec((tc, w), lambda b, i: (b * nt + i, 1)),
            pl.BlockSpec((CONV_WIDTH, w), lambda b, i: (0, 0)),
            vec,
            pl.BlockSpec((w, w), lambda b, i: (0, 0)), vec,
            pl.BlockSpec((w, w), lambda b, i: (0, 0)), vec,
            vec, vec,
        ],
        out_specs=pl.BlockSpec((tc, w), lambda b, i: (b * nt + i, 0)),
        out_shape=jax.ShapeDtypeStruct((batch * seq, w), BF16),
        scratch_shapes=[pltpu.VMEM((tc + 8, w), F32), pltpu.VMEM((8, w), F32)],
        compiler_params=_params(("parallel", "arbitrary")),
        name="rglru",
    )(xrg, xrg, cw, cb, wa, ba, wx, bx, lam, gn)


def _out_proj_kernel(a_ref, l_ref, x_ref, ga_ref, woa_ref, wol_ref, gf_ref, wq_ref, sk_ref,
                     h1_ref, xt_ref, st_ref):
    an = _rms(a_ref[...], ga_ref[...]).astype(BF16)
    h1 = x_ref[...] + _dot(an, woa_ref[...]) + _dot(l_ref[...], wol_ref[...])
    h1_ref[...] = h1
    xn = _rms(h1, gf_ref[...])
    xt_ref[...] = xn.T.astype(BF16)
    q = _dot(xn.astype(BF16), wq_ref[...])
    for hp in range(2 * PEER_HEADS):
        qs = q[:, LANES * hp:LANES * (hp + 1)].astype(BF16)
        st_ref[hp] = _dot_nt(sk_ref[hp % 2], qs)


def _out_proj(a_out, l_n, x2, ga, woa, wol, gf, wq, sk, tm):
    n, d = x2.shape
    full = lambda arr: pl.BlockSpec(arr.shape, lambda i: (0,) * arr.ndim)
    return pl.pallas_call(
        _out_proj_kernel,
        grid=(n // tm,),
        in_specs=[
            pl.BlockSpec((tm, a_out.shape[1]), lambda i: (i, 0)),
            pl.BlockSpec((tm, l_n.shape[1]), lambda i: (i, 0)),
            pl.BlockSpec((tm, d), lambda i: (i, 0)),
            full(ga), full(woa), full(wol), full(gf), full(wq), full(sk),
        ],
        out_specs=[
            pl.BlockSpec((tm, d), lambda i: (i, 0)),
            pl.BlockSpec((d, tm), lambda i: (0, i)),
            pl.BlockSpec((2 * PEER_HEADS, PEER_N_KEYS, tm), lambda i: (0, 0, i)),
        ],
        out_shape=[
            jax.ShapeDtypeStruct((n, d), F32),
            jax.ShapeDtypeStruct((d, n), BF16),
            jax.ShapeDtypeStruct((2 * PEER_HEADS, PEER_N_KEYS, n), F32),
        ],
        compiler_params=_params(("parallel",)),
        name="out_proj",
    )(a_out, l_n, x2, ga, woa, wol, gf, wq, sk)


N_TOP = PEER_TOPK
_PAIRS = [(i, j) for i in range(N_TOP) for j in range(N_TOP) if (i + 1) * (j + 1) <= N_TOP]


def _top_values(s):
    n, tl = s.shape
    parts = [s[SUBLANES * k:SUBLANES * (k + 1)] for k in range(n // SUBLANES)]
    base = lax.broadcasted_iota(jnp.int32, (SUBLANES, tl), 0).astype(F32)
    rows = [base + float(SUBLANES * k) for k in range(len(parts))]
    out = []
    for r in range(N_TOP):
        m = jnp.max(functools.reduce(jnp.maximum, parts), axis=0, keepdims=True)
        out.append(m)
        first = jnp.full((SUBLANES, tl), float(n), F32)
        for k in reversed(range(len(parts))):
            first = jnp.where(parts[k] == m, rows[k], first)
        first = jnp.min(first, axis=0, keepdims=True)
        taken = TAKEN * (1.0 + r / TAKEN_STEPS)
        parts = [jnp.where(rows[k] == first, taken, parts[k]) for k in range(len(parts))]
    rank = jnp.concatenate(parts, axis=0)
    rank = jnp.where(rank <= TAKEN, (rank * (1.0 / TAKEN) - 1.0) * TAKEN_STEPS, float(PEER_TOPK))
    return out, rank


def _peer_stats_kernel(st_ref, r2_ref, e2_ref, l_ref, e1_ref):
    tops = [[], []]
    rank1 = []
    for h in range(PEER_HEADS):
        vals, rank = _top_values(st_ref[2 * h])
        tops[0].append(vals)
        rank1.append(rank)
        vals, rank = _top_values(st_ref[2 * h + 1])
        tops[1].append(vals)
        r2_ref[h] = rank.astype(BF16)
    a = [jnp.concatenate([tops[0][h][i] for h in range(PEER_HEADS)], axis=0) for i in range(N_TOP)]
    b = [jnp.concatenate([tops[1][h][j] for h in range(PEER_HEADS)], axis=0) for j in range(N_TOP)]
    cand = [a[i] + b[j] for (i, j) in _PAIRS]
    work = list(cand)
    kth = []
    for _ in range(N_TOP):
        m = functools.reduce(jnp.maximum, work)
        kth.append(m)
        found = jnp.zeros(m.shape, jnp.bool_)
        for k in range(len(work)):
            hit = (work[k] == m) & jnp.logical_not(found)
            work[k] = jnp.where(hit, REMOVED, work[k])
            found = found | hit
    thr = kth[PEER_TOPK - 1]
    top = a[0] + b[0]
    z = functools.reduce(lambda u, w: u + w, [jnp.where(c >= thr, jnp.exp(c - top), 0.0) for c in cand])
    n_sel = [functools.reduce(lambda u, w: u + w,
                              [(a[r] + b[j] >= thr).astype(F32) for j in range(PEER_TOPK)])
             for r in range(PEER_TOPK)]
    for h in range(PEER_HEADS):
        row = slice(h, h + 1)
        prefix = jnp.zeros(rank1[h].shape, F32)
        for r in range(PEER_TOPK):
            prefix = jnp.where(rank1[h] == float(r), n_sel[r][row], prefix)
        l_ref[h] = prefix
        e1_ref[h] = jnp.exp(st_ref[2 * h] - a[0][row])
        e2_ref[h] = (jnp.exp(st_ref[2 * h + 1] - b[0][row]) / z[row]).astype(BF16)


def _peer_stats(st, tl):
    hp, nk, n = st.shape
    out_spec = pl.BlockSpec((PEER_HEADS, nk, tl), lambda i: (0, 0, i))
    return pl.pallas_call(
        _peer_stats_kernel,
        grid=(n // tl,),
        in_specs=[pl.BlockSpec((hp, nk, tl), lambda i: (0, 0, i))],
        out_specs=[out_spec] * 4,
        out_shape=[
            jax.ShapeDtypeStruct((PEER_HEADS, nk, n), BF16),
            jax.ShapeDtypeStruct((PEER_HEADS, nk, n), BF16),
            jax.ShapeDtypeStruct((PEER_HEADS, nk, n), F32),
            jax.ShapeDtypeStruct((PEER_HEADS, nk, n), F32),
        ],
        compiler_params=_params(("parallel",)),
        name="peer_stats",
    )(st)


def _peer_main_kernel(u_ref, vt_ref, xt_ref, r2_ref, e2_ref, l_ref, e1_ref, h1_ref, o_ref, acc_ref):
    c = pl.program_id(1)
    ce = u_ref.shape[0]
    tm = xt_ref.shape[1]
    nk = PEER_N_KEYS

    @pl.when(c == 0)
    def _():
        acc_ref[...] = jnp.zeros(acc_ref.shape, F32)

    act = _gelu_tanh(_dot(u_ref[...], xt_ref[...]).astype(BF16))

    def rows_bf16(ref, h, i1):
        packed = jnp.broadcast_to(ref[h, pl.ds(i1, 1), :], (BF16_ROWS, tm)).astype(BF16)
        return jnp.concatenate([packed] * (nk // BF16_ROWS), axis=0)

    w_parts = []
    for ii in range(ce // nk):
        i1 = c * (ce // nk) + ii
        g = None
        for h in range(PEER_HEADS):
            e2 = e2_ref[h]
            term = jnp.where(r2_ref[h] < rows_bf16(l_ref, h, i1), e2, jnp.zeros_like(e2)) * rows_bf16(e1_ref, h, i1)
            g = term if g is None else g + term
        w_parts.append(act[ii * nk:(ii + 1) * nk, :].astype(BF16) * g)
    w = jnp.concatenate(w_parts, axis=0)
    acc_ref[...] += _dot(vt_ref[...], w)

    @pl.when(c == pl.num_programs(1) - 1)
    def _():
        o_ref[...] = h1_ref[...] + acc_ref[...].T


def _peer_main(u_b, vt_b, xt, r2, e2, n_sel, e1, h1, tm, ce):
    ne, d = u_b.shape
    n = xt.shape[1]
    per_tok = pl.BlockSpec((PEER_HEADS, PEER_N_KEYS, tm), lambda t, c: (0, 0, t))
    return pl.pallas_call(
        _peer_main_kernel,
        grid=(n // tm, ne // ce),
        in_specs=[
            pl.BlockSpec((ce, d), lambda t, c: (c, 0)),
            pl.BlockSpec((d, ce), lambda t, c: (0, c)),
            pl.BlockSpec((d, tm), lambda t, c: (0, t)),
            per_tok, per_tok, per_tok, per_tok,
            pl.BlockSpec((tm, d), lambda t, c: (t, 0)),
        ],
        out_specs=pl.BlockSpec((tm, d), lambda t, c: (t, 0)),
        out_shape=jax.ShapeDtypeStruct((n, d), F32),
        scratch_shapes=[pltpu.VMEM((d, tm), F32)],
        compiler_params=_params(("parallel", "arbitrary")),
        name="peer_main",
    )(u_b, vt_b, xt, r2, e2, n_sel, e1, h1)


def _ple_final_kernel(h_ref, p_ref, gp_ref, wg_ref, bg_ref, wp_ref, gfin_ref, o_ref):
    h = h_ref[...]
    gate = jax.nn.sigmoid(_dot(_rms(h, gp_ref[...]).astype(BF16), wg_ref[...]) + bg_ref[...])
    h = h + gate * _dot(p_ref[...].astype(BF16), wp_ref[...])
    o_ref[...] = _rms(h, gfin_ref[...])


def _ple_final(h2, p2, gp, wg, bg, wp, gfin, tm):
    n, d = h2.shape
    full = lambda arr: pl.BlockSpec(arr.shape, lambda i: (0,) * arr.ndim)
    return pl.pallas_call(
        _ple_final_kernel,
        grid=(n // tm,),
        in_specs=[
            pl.BlockSpec((tm, d), lambda i: (i, 0)),
            pl.BlockSpec((tm, p2.shape[1]), lambda i: (i, 0)),
            full(gp), full(wg), full(bg), full(wp), full(gfin),
        ],
        out_specs=pl.BlockSpec((tm, d), lambda i: (i, 0)),
        out_shape=jax.ShapeDtypeStruct((n, d), F32),
        compiler_params=_params(("parallel",)),
        name="ple_final",
    )(h2, p2, gp, wg, bg, wp, gfin)


def _overlap_t(seq, n_rows):
    n_cmp = n_rows - 1
    cmp_start = np.arange(n_cmp) * CMP_STRIDE
    sel_start = np.arange(seq // SEL_LEN) * SEL_LEN
    ov = np.clip(np.minimum(cmp_start[:, None] + CMP_LEN, sel_start[None, :] + SEL_LEN)
                 - np.maximum(cmp_start[:, None], sel_start[None, :]), 0, None).astype(np.float32) / CMP_LEN
    out = np.zeros((seq // SEL_LEN, n_rows), np.float32)
    out[:, :n_cmp] = ov.T
    return out


def _block_expand(seq, starts, width):
    pos = np.asarray(starts).reshape(-1, 1, 1) + np.arange(width).reshape(1, 1, width)
    blk = np.where(pos >= 0, pos // SEL_LEN, -1)
    return (blk == np.arange(seq // SEL_LEN).reshape(1, -1, 1)).astype(np.float32)


def _block_diag(w):
    nb, bd, _ = w.shape
    out = jnp.zeros((nb * bd, nb * bd), w.dtype)
    for j in range(nb):
        out = out.at[j * bd:(j + 1) * bd, j * bd:(j + 1) * bd].set(w[j])
    return out


def _layer(h, p_i, attn_norm, w_in, cmp_k_pe, cmp_k_w1, cmp_k_w2, cmp_v_pe, cmp_v_w1, cmp_v_w2,
           rel_table, conv_w, conv_b, lru_wa, lru_ba, lru_wx, lru_bx, lru_lambda,
           grp_norm_attn, grp_norm_lru, w_out, ffn_norm, peer_wq, peer_subkeys, peer_u, peer_v,
           ple_norm, ple_wgate, ple_bgate, ple_proj, *, tm, tc, tl, ce):
    batch, seq, d = h.shape
    n = batch * seq
    x2 = h.reshape(n, d)
    aw = N_HEADS * HEAD_DIM
    kvw = N_KV_HEADS * HEAD_DIM
    lw = d - aw

    o = 0
    cols = {}
    for name, width in (("q", aw), ("kc", kvw), ("vc", kvw), ("ks", kvw), ("vs", kvw), ("kw", kvw),
                        ("vw", kvw), ("gates", N_HEADS * 3), ("xr", lw), ("xg", lw)):
        cols[name] = w_in[:, o:o + width]
        o += width
    gate_pad = [jnp.pad(cols["gates"][:, N_GATE_COLS * k:N_GATE_COLS * (k + 1)], ((0, 0), (0, LANES - N_GATE_COLS)))
                for k in range(N_KV_HEADS)]
    wf = jnp.concatenate([cols["xr"], cols["xg"], cols["kc"], cols["vc"]] + gate_pad, axis=1).astype(BF16)
    wb = jnp.concatenate([cols["q"], cols["ks"], cols["kw"], cols["vs"], cols["vw"]], axis=1).astype(BF16)

    xrg, kvc, gates2, q4, kk, vv = _in_proj(x2, attn_norm.reshape(1, d), wf, wb, tm)

    n_rows = seq // CMP_STRIDE
    r4 = kvc.reshape(4, batch, n_rows, CMP_STRIDE * HEAD_DIM)
    pe2 = jnp.stack([cmp_k_pe.reshape(1, -1), cmp_v_pe.reshape(1, -1)])
    w1s = jnp.stack([cmp_k_w1, cmp_v_w1]).astype(BF16)
    w2s = jnp.stack([cmp_k_w2, cmp_v_w2]).astype(BF16)
    cmp4 = _compress(r4, pe2, w1s, w2s)

    winadd, seladd = _bias_near(rel_table)
    addc = _bias_cmp(rel_table, seq, n_rows, min(seq, 512))
    ovlt = jnp.asarray(_overlap_t(seq, n_rows), BF16)
    o_c, sel = _cmp_attn(q4, cmp4, addc, ovlt, batch, seq)

    nt = seq // ATT_TILE
    tile_starts = np.arange(nt) * ATT_TILE
    enear = jnp.asarray(_block_expand(seq, tile_starts - ATT_TILE, NEAR_SPAN), BF16)
    eslab = jnp.asarray(_block_expand(seq, np.arange(max(seq // FAR_SLAB, 1)) * FAR_SLAB, FAR_SLAB), BF16)
    front = ((0, 0), (0, 0), (WINDOW, 0), (0, 0))
    kk5 = jnp.pad(kk.reshape(4, batch, seq, HEAD_DIM), front)
    vv5 = jnp.pad(vv.reshape(4, batch, seq, LANES), front)
    a_out = _sel_win_attn(q4, kk5, vv5, sel, enear, eslab, winadd, seladd, o_c, gates2, batch, seq)

    l_n = _rglru(xrg, conv_w.reshape(CONV_WIDTH, lw), conv_b.reshape(1, lw),
                 _block_diag(lru_wa).astype(BF16), lru_ba.reshape(1, lw),
                 _block_diag(lru_wx).astype(BF16), lru_bx.reshape(1, lw),
                 lru_lambda.reshape(1, lw), grp_norm_lru.reshape(1, lw), batch, seq, tc)

    w_out_b = w_out.astype(BF16)
    h1, xt, st = _out_proj(a_out, l_n, x2, grp_norm_attn.reshape(1, aw), w_out_b[:aw], w_out_b[aw:],
                           ffn_norm.reshape(1, d), peer_wq.astype(BF16), peer_subkeys.astype(BF16), tm)
    r2, e2, n_sel, e1 = _peer_stats(st, tl)
    h2 = _peer_main(peer_u.astype(BF16), peer_v.T.astype(BF16), xt, r2, e2, n_sel, e1, h1, tm, ce)
    return (h2, p_i.reshape(n, -1), ple_norm.reshape(1, d), ple_wgate.astype(BF16), ple_bgate.reshape(1, d),
            ple_proj.astype(BF16))


def kernel(x, p, attn_norm, w_in, cmp_k_pe, cmp_k_w1, cmp_k_w2, cmp_v_pe, cmp_v_w1, cmp_v_w2, rel_table, conv_w, conv_b, lru_wa, lru_ba, lru_wx, lru_bx, lru_lambda, grp_norm_attn, grp_norm_lru, w_out, ffn_norm, peer_wq, peer_subkeys, peer_u, peer_v, ple_norm, ple_wgate, ple_bgate, ple_proj, final_norm):
    batch, seq, d = x.shape
    assert w_in.shape[0] == 1, "single-layer stack only"
    tm = min(512, seq)
    h2, p2, gp, wg, bg, wp = _layer(
        x, p[0], attn_norm[0], w_in[0], cmp_k_pe[0], cmp_k_w1[0], cmp_k_w2[0], cmp_v_pe[0], cmp_v_w1[0],
        cmp_v_w2[0], rel_table, conv_w[0], conv_b[0], lru_wa[0], lru_ba[0], lru_wx[0], lru_bx[0],
        lru_lambda[0], grp_norm_attn[0], grp_norm_lru[0], w_out[0], ffn_norm[0], peer_wq[0],
        peer_subkeys[0], peer_u[0], peer_v[0], ple_norm[0], ple_wgate[0], ple_bgate[0], ple_proj[0],
        tm=tm, tc=min(256, seq), tl=min(256, seq), ce=2048)
    out = _ple_final(h2, p2, gp, wg, bg, wp, final_norm.reshape(1, d), tm)
    return out.reshape(batch, seq, d)
```

```python
import functools
import math

import numpy as np
import jax
import jax.numpy as jnp
from jax import lax
from jax.experimental import pallas as pl
from jax.experimental.pallas import tpu as pltpu

F32 = jnp.float32
BF16 = jnp.bfloat16

N_HEADS = 8
HEAD_DIM = 64
N_KV_HEADS = 2
GROUP = N_HEADS // N_KV_HEADS
LRU_C = 8.0
CONV_WIDTH = 4
CMP_LEN = 32
CMP_STRIDE = 16
SEL_LEN = 64
SEL_TOPK = 16
N_LOCAL_BLOCKS = 2
WINDOW = 512
FORCE_LOCAL = 2.0e4
FORCE_INIT = 1.0e4
REL_BUCKETS = 32
REL_MAX_DIST = 128
PEER_HEADS = 8
PEER_N_KEYS = 128
PEER_TOPK = 16
EPS = 1e-6
NEG = -1e30
LOG2E = math.log2(math.e)
Q_SCALE = HEAD_DIM ** -0.5 * LOG2E
REMOVED = -3.0e38
TAKEN = -2.0 ** 120
TAKEN_STEPS = 32.0

LANES = 128
SUBLANES = 8
BF16_ROWS = 16
VMEM_LIMIT = 56 * 1024 * 1024

ATT_TILE = 256
CMP_TILE = 512
N_GATE_COLS = GROUP * 3


def _dot(a, b):
    return jnp.dot(a, b, preferred_element_type=F32)


def _dot_nt(a, b):
    return lax.dot_general(a, b, (((1,), (1,)), ((), ())), preferred_element_type=F32)


def _gelu_tanh(x):
    c = math.sqrt(2.0 / math.pi)
    half = 0.5 * x
    return half + half * jnp.tanh(x * (c + (c * 0.044715) * (x * x)))


def _rms(x, g):
    return x * lax.rsqrt(jnp.mean(x * x, axis=-1, keepdims=True) + EPS) * g


def _params(sem, vmem=VMEM_LIMIT):
    return pltpu.CompilerParams(dimension_semantics=sem, vmem_limit_bytes=vmem)


def _in_proj_kernel(x_ref, g_ref, wf_ref, wb_ref, xrg_ref, kvc_ref, gates_ref, q4_ref, kk_ref, vv_ref):
    dh = HEAD_DIM
    n_kv = 2 * N_KV_HEADS
    xb = _rms(x_ref[...], g_ref[...]).astype(BF16)
    yf = _dot(xb, wf_ref[...])
    kv0 = xrg_ref.shape[1]
    g0 = kv0 + n_kv * dh
    xrg_ref[...] = yf[:, :kv0]
    for j in range(n_kv):
        kvc_ref[j] = yf[:, kv0 + dh * j:kv0 + dh * (j + 1)]
    for k in range(N_KV_HEADS):
        gates_ref[k] = yf[:, g0 + LANES * k:g0 + LANES * (k + 1)]
    yb = _dot(xb, wb_ref[...])
    for h in range(N_HEADS):
        q4_ref[h] = (yb[:, dh * h:dh * (h + 1)] * Q_SCALE).astype(BF16)
    k0 = N_HEADS * dh
    v0 = k0 + n_kv * dh
    ones_col = (lax.broadcasted_iota(jnp.int32, (yb.shape[0], LANES - dh), 1) == 0).astype(F32)
    for j in range(n_kv):
        kk_ref[j] = yb[:, k0 + dh * j:k0 + dh * (j + 1)].astype(BF16)
        vv_ref[j] = jnp.concatenate([yb[:, v0 + dh * j:v0 + dh * (j + 1)], ones_col], axis=1).astype(BF16)


def _in_proj(x2, g, wf, wb, tm):
    n, d = x2.shape
    dh = HEAD_DIM
    n_kv = 2 * N_KV_HEADS
    lru2 = wf.shape[1] - n_kv * dh - N_KV_HEADS * LANES
    shapes = [
        ((n, lru2), F32),
        ((n_kv, n, dh), F32),
        ((N_KV_HEADS, n, LANES), F32),
        ((N_HEADS, n, dh), BF16),
        ((n_kv, n, dh), BF16),
        ((n_kv, n, LANES), BF16),
    ]

    def spec(shape):
        if len(shape) == 2:
            return pl.BlockSpec((tm, shape[1]), lambda i: (i, 0))
        return pl.BlockSpec((shape[0], tm, shape[2]), lambda i: (0, i, 0))

    return pl.pallas_call(
        _in_proj_kernel,
        grid=(n // tm,),
        in_specs=[
            pl.BlockSpec((tm, d), lambda i: (i, 0)),
            pl.BlockSpec((1, d), lambda i: (0, 0)),
            pl.BlockSpec(wf.shape, lambda i: (0, 0)),
            pl.BlockSpec(wb.shape, lambda i: (0, 0)),
        ],
        out_specs=[spec(shape) for shape, _ in shapes],
        out_shape=[jax.ShapeDtypeStruct(shape, dtype) for shape, dtype in shapes],
        compiler_params=_params(("parallel",)),
        name="in_proj",
    )(x2, g, wf, wb)


def _compress_kernel(r_ref, pe_ref, w1_ref, w2_ref, o_ref):
    r = r_ref[0, 0]
    n_rows = r.shape[0]
    half = r.shape[1]
    pe = pe_ref[0]
    top = _dot((r + pe[:, :half]).astype(BF16), w1_ref[0, :half, :])
    bot = _dot((r + pe[:, half:]).astype(BF16), w1_ref[0, half:, :])
    hid = top + pltpu.roll(bot, n_rows - 1, 0)
    act = jax.nn.gelu(hid)
    out = _dot(act.astype(BF16), w2_ref[0])
    row = lax.broadcasted_iota(jnp.int32, out.shape, 0)
    o_ref[0, 0] = jnp.where(row < n_rows - 1, out, 0.0).astype(BF16)


def _compress(r4, pe2, w1s, w2s):
    four, b, n_rows, width = r4.shape
    return pl.pallas_call(
        _compress_kernel,
        grid=(four, b),
        in_specs=[
            pl.BlockSpec((1, 1, n_rows, width), lambda j, bb: (j, bb, 0, 0)),
            pl.BlockSpec((1, 1, 2 * width), lambda j, bb: (j // 2, 0, 0)),
            pl.BlockSpec((1, 2 * width, w1s.shape[2]), lambda j, bb: (j // 2, 0, 0)),
            pl.BlockSpec((1, w2s.shape[1], HEAD_DIM), lambda j, bb: (j // 2, 0, 0)),
        ],
        out_specs=pl.BlockSpec((1, 1, n_rows, HEAD_DIM), lambda j, bb: (j, bb, 0, 0)),
        out_shape=jax.ShapeDtypeStruct((four, b, n_rows, HEAD_DIM), BF16),
        compiler_params=_params(("parallel", "parallel")),
        name="compress",
    )(r4, pe2, w1s, w2s)


def _bucket_breakpoints():
    n = np.arange(0, 4 * REL_MAX_DIST)
    max_exact = REL_BUCKETS // 2
    nf = np.maximum(n, max_exact).astype(np.float32)
    large = max_exact + (np.log(nf / max_exact) / np.float32(math.log(REL_MAX_DIST / max_exact))
                         * (REL_BUCKETS - max_exact)).astype(np.int32)
    large = np.minimum(large, REL_BUCKETS - 1)
    bucket = np.where(n < max_exact, n, large)
    return [int(np.argmax(bucket >= j)) for j in range(1, REL_BUCKETS)]


_BREAKS = _bucket_breakpoints()


def _bias_of_dist(dist, tab_ref, h):
    v = jnp.full(dist.shape, tab_ref[0, h], F32)
    for j, bp in enumerate(_BREAKS, start=1):
        v = jnp.where(dist >= bp, tab_ref[j, h], v)
    v = (v - tab_ref[REL_BUCKETS - 1, h]) * LOG2E
    return jnp.where(dist >= 0, v, NEG)


WIN_SPAN = WINDOW + ATT_TILE
NEAR_SPAN = 2 * ATT_TILE


def _bias_near_kernel(tab_ref, win_ref, sel_ref):
    h = pl.program_id(0)
    t = ATT_TILE
    for ref, span in ((win_ref, WIN_SPAN), (sel_ref, NEAR_SPAN)):
        r = lax.broadcasted_iota(jnp.int32, (t, span), 0)
        c = lax.broadcasted_iota(jnp.int32, (t, span), 1)
        dist = r - c + (span - t)
        ref[0] = jnp.where(dist < WINDOW, _bias_of_dist(dist, tab_ref, h), NEG)


def _bias_near(rel_table):
    t = ATT_TILE
    return pl.pallas_call(
        _bias_near_kernel,
        grid=(N_HEADS,),
        in_specs=[pl.BlockSpec(memory_space=pltpu.SMEM)],
        out_specs=[pl.BlockSpec((1, t, WIN_SPAN), lambda h: (h, 0, 0)),
                   pl.BlockSpec((1, t, NEAR_SPAN), lambda h: (h, 0, 0))],
        out_shape=[jax.ShapeDtypeStruct((N_HEADS, t, WIN_SPAN), F32),
                   jax.ShapeDtypeStruct((N_HEADS, t, NEAR_SPAN), F32)],
        compiler_params=_params(("arbitrary",)),
        name="bias_near",
    )(rel_table)


def _bias_cmp_kernel(tab_ref, o_ref, *, n_cmp):
    h = pl.program_id(0)
    i = pl.program_id(1)
    tq, nc = o_ref.shape[1], o_ref.shape[2]
    span = 2 * nc
    b = lax.broadcasted_iota(jnp.int32, (CMP_STRIDE, span), 0)
    u = lax.broadcasted_iota(jnp.int32, (CMP_STRIDE, span), 1)
    table = _bias_of_dist(CMP_STRIDE * (nc - 1 - u) + b - (CMP_LEN - 1), tab_ref, h)
    col = lax.broadcasted_iota(jnp.int32, (CMP_STRIDE, nc), 1)
    groups = tq // CMP_STRIDE
    for k in range(groups):
        a = i * groups + k
        win = pltpu.roll(table, (a + span - (nc - 1)) % span, 1)[:, :nc]
        o_ref[0, CMP_STRIDE * k:CMP_STRIDE * (k + 1), :] = jnp.where(col < n_cmp, win, NEG)


def _bias_cmp(rel_table, seq, n_rows, tq):
    return pl.pallas_call(
        functools.partial(_bias_cmp_kernel, n_cmp=n_rows - 1),
        grid=(N_HEADS, seq // tq),
        in_specs=[pl.BlockSpec(memory_space=pltpu.SMEM)],
        out_specs=pl.BlockSpec((1, tq, n_rows), lambda h, i: (h, i, 0)),
        out_shape=jax.ShapeDtypeStruct((N_HEADS, seq, n_rows), F32),
        compiler_params=_params(("arbitrary", "arbitrary")),
        name="bias_cmp",
    )(rel_table)


def _cmp_attn_kernel(q_ref, kc_ref, vc_ref, addc_ref, ovlt_ref, eye_ref, oc_ref, sel_ref):
    i = pl.program_id(1)
    tq = q_ref.shape[1]
    n_sel = ovlt_ref.shape[0]
    kc = kc_ref[0, 0]
    vc = vc_ref[0, 0]
    t_col = lax.broadcasted_iota(jnp.int32, (tq, 1), 0) + i * tq
    row_ok = (t_col >= CMP_LEN - 1).astype(F32)
    psum = None
    for hh in range(GROUP):
        logit = _dot_nt(q_ref[hh], kc) + addc_ref[hh]
        m = jnp.max(logit, axis=-1, keepdims=True)
        e = jnp.exp2(logit - m)
        p = e / jnp.sum(e, axis=-1, keepdims=True)
        oc_ref[:, HEAD_DIM * hh:HEAD_DIM * (hh + 1)] = _dot(p.astype(BF16), vc) * row_ok
        psum = p if psum is None else psum + p
    psum = psum * row_ok
    hi = psum.astype(BF16)
    lo = (psum - hi.astype(F32)).astype(BF16)
    ovlt = ovlt_ref[...]
    imp = _dot_nt(ovlt, hi) + _dot_nt(ovlt, lo)
    s_idx = lax.broadcasted_iota(jnp.int32, (n_sel, tq), 0)
    t_blk = (lax.broadcasted_iota(jnp.int32, (n_sel, tq), 1) + i * tq) // SEL_LEN
    d_blk = t_blk - s_idx
    local = (d_blk >= 0) & (d_blk < N_LOCAL_BLOCKS)
    v = jnp.where(local, FORCE_LOCAL,
                  jnp.where(s_idx == 0, FORCE_INIT, jnp.where(d_blk >= 0, imp, -1.0)))
    rows = s_idx.astype(F32)
    sel = jnp.zeros((n_sel, tq), F32)
    for _ in range(min(SEL_TOPK, n_sel)):
        m = jnp.max(v, axis=0, keepdims=True)
        first = jnp.min(jnp.where(v == m, rows, float(n_sel)), axis=0, keepdims=True)
        pick = rows == first
        sel = jnp.where(pick, 1.0, sel)
        v = jnp.where(pick, REMOVED, v)
    sel_ref[0] = _dot_nt(eye_ref[...], sel.astype(BF16)).astype(BF16)


def _cmp_attn(q4, cmp4, addc, ovlt, batch, seq):
    tq = min(CMP_TILE, seq)
    nt = seq // tq
    n_rows = cmp4.shape[2]
    n_sel = ovlt.shape[0]
    eye = jnp.eye(tq, dtype=BF16)
    return pl.pallas_call(
        _cmp_attn_kernel,
        grid=(batch * N_KV_HEADS, nt),
        in_specs=[
            pl.BlockSpec((GROUP, tq, HEAD_DIM), lambda g, i: (g % 2, (g // 2) * nt + i, 0)),
            pl.BlockSpec((1, 1, n_rows, HEAD_DIM), lambda g, i: (g % 2, g // 2, 0, 0)),
            pl.BlockSpec((1, 1, n_rows, HEAD_DIM), lambda g, i: (2 + g % 2, g // 2, 0, 0)),
            pl.BlockSpec((GROUP, tq, n_rows), lambda g, i: (g % 2, i, 0)),
            pl.BlockSpec(ovlt.shape, lambda g, i: (0, 0)),
            pl.BlockSpec((tq, tq), lambda g, i: (0, 0)),
        ],
        out_specs=[
            pl.BlockSpec((tq, GROUP * HEAD_DIM), lambda g, i: ((g // 2) * nt + i, g % 2)),
            pl.BlockSpec((1, tq, n_sel), lambda g, i: (g, i, 0)),
        ],
        out_shape=[
            jax.ShapeDtypeStruct((batch * seq, N_HEADS * HEAD_DIM), F32),
            jax.ShapeDtypeStruct((batch * N_KV_HEADS, seq, n_sel), BF16),
        ],
        compiler_params=_params(("parallel", "arbitrary")),
        name="cmp_attn",
    )(q4, cmp4, cmp4, addc, ovlt, eye)


def _softmax_step(qs, k, v1, mask, add, carry):
    tq = qs.shape[0] // GROUP
    s = _dot_nt(qs, k).reshape(GROUP, tq, k.shape[0])
    if mask is not None:
        s = jnp.where(mask[None] > 0.5, s, NEG)
    if add is not None:
        s = s + add
    m_tile = jnp.max(s, axis=-1, keepdims=True)
    if carry is None:
        m_new = m_tile
    else:
        m, acc = carry
        m_new = jnp.maximum(m, m_tile)
    p = jnp.exp2(s - m_new)
    pv = _dot(p.reshape(GROUP * tq, k.shape[0]).astype(BF16), v1).reshape(GROUP, tq, v1.shape[1])
    if carry is None:
        return m_new, pv
    return m_new, jnp.exp2(m - m_new) * acc + pv


FAR_SLAB = 4 * ATT_TILE


def _sel_win_kernel(q_ref, ks_ref, vs_ref, kw_ref, vw_ref, sel_ref, enear_ref, eslab_ref,
                    winadd_ref, seladd_ref, oc_ref, gates_ref, o_ref):
    i = pl.program_id(1)
    tq = q_ref.shape[1]
    qs = q_ref[...].reshape(GROUP * tq, HEAD_DIM)
    sel = sel_ref[0]

    def rows(ref, start, size):
        return ref[0, 0, pl.ds(pl.multiple_of(start, ATT_TILE), size), :]

    def hide_padding(add, n_pad):
        col = lax.broadcasted_iota(jnp.int32, (1, 1, add.shape[2]), 2)
        return jnp.where(col < n_pad, NEG, add)

    w0 = i * tq
    _, acc_w = _softmax_step(qs, rows(kw_ref, w0, WIN_SPAN), rows(vw_ref, w0, WIN_SPAN), None,
                             hide_padding(winadd_ref[...], WINDOW - i * tq), None)

    n0 = w0 + WINDOW - tq
    carry = _softmax_step(qs, rows(ks_ref, n0, NEAR_SPAN), rows(vs_ref, n0, NEAR_SPAN),
                          _dot(sel, enear_ref[i]), hide_padding(seladd_ref[...], tq - i * tq), None)
    far_end = jnp.maximum(i - 1, 0) * tq
    col = lax.broadcasted_iota(jnp.int32, (1, FAR_SLAB), 1)

    def far_slab(j, c):
        start = WINDOW + j * FAR_SLAB
        mask = jnp.where(col < far_end - j * FAR_SLAB, _dot(sel, eslab_ref[j]), 0.0)
        return _softmax_step(qs, rows(ks_ref, start, FAR_SLAB), rows(vs_ref, start, FAR_SLAB), mask, None, c)

    _, acc_s = lax.fori_loop(0, (far_end + FAR_SLAB - 1) // FAR_SLAB, far_slab, carry)
    o_s = acc_s[..., :HEAD_DIM] / acc_s[..., HEAD_DIM:HEAD_DIM + 1]
    o_w = acc_w[..., :HEAD_DIM] / acc_w[..., HEAD_DIM:HEAD_DIM + 1]

    g = jax.nn.sigmoid(gates_ref[0])
    for hh in range(GROUP):
        o_c = oc_ref[:, HEAD_DIM * hh:HEAD_DIM * (hh + 1)]
        o_ref[:, HEAD_DIM * hh:HEAD_DIM * (hh + 1)] = (g[:, 3 * hh:3 * hh + 1] * o_c
                                            + g[:, 3 * hh + 1:3 * hh + 2] * o_s[hh]
                                            + g[:, 3 * hh + 2:3 * hh + 3] * o_w[hh])


def _sel_win_attn(q4, kk5, vv5, sel, enear, eslab, winadd, seladd, o_c, gates2, batch, seq):
    tq = ATT_TILE
    nt = seq // tq
    n_sel = sel.shape[2]
    padded = kk5.shape[2]

    def kv_spec(arr, base):
        return pl.BlockSpec((1, 1, padded, arr.shape[3]), lambda g, i: (base + g % 2, g // 2, 0, 0))

    whole = lambda arr: pl.BlockSpec(arr.shape, lambda g, i: (0,) * arr.ndim)
    return pl.pallas_call(
        _sel_win_kernel,
        grid=(batch * N_KV_HEADS, nt),
        in_specs=[
            pl.BlockSpec((GROUP, tq, HEAD_DIM), lambda g, i: (g % 2, (g // 2) * nt + i, 0)),
            kv_spec(kk5, 0), kv_spec(vv5, 0), kv_spec(kk5, 2), kv_spec(vv5, 2),
            pl.BlockSpec((1, tq, n_sel), lambda g, i: (g, i, 0)),
            whole(enear), whole(eslab),
            pl.BlockSpec((GROUP, tq, WIN_SPAN), lambda g, i: (g % 2, 0, 0)),
            pl.BlockSpec((GROUP, tq, NEAR_SPAN), lambda g, i: (g % 2, 0, 0)),
            pl.BlockSpec((tq, GROUP * HEAD_DIM), lambda g, i: ((g // 2) * nt + i, g % 2)),
            pl.BlockSpec((1, tq, LANES), lambda g, i: (g % 2, (g // 2) * nt + i, 0)),
        ],
        out_specs=pl.BlockSpec((tq, GROUP * HEAD_DIM), lambda g, i: ((g // 2) * nt + i, g % 2)),
        out_shape=jax.ShapeDtypeStruct((batch * seq, N_HEADS * HEAD_DIM), F32),
        compiler_params=_params(("parallel", "arbitrary")),
        name="sel_win_attn",
    )(q4, kk5, vv5, kk5, vv5, sel, enear, eslab, winadd, seladd, o_c, gates2)


def _rglru_kernel(xrg_ref, xg_ref, cw_ref, cb_ref, wa_ref, ba_ref, wx_ref, bx_ref, lam_ref, gn_ref,
                  o_ref, xs_ref, h_ref):
    step = pl.program_id(1)
    tc = xrg_ref.shape[0]

    @pl.when(step == 0)
    def _():
        xs_ref[0:8, :] = jnp.zeros((8, xs_ref.shape[1]), F32)
        h_ref[...] = jnp.zeros(h_ref.shape, F32)

    xs_ref[8:, :] = xrg_ref[...]
    xc = cb_ref[...] + cw_ref[CONV_WIDTH - 1:CONV_WIDTH, :] * xs_ref[8:, :]
    for k in range(1, CONV_WIDTH):
        xc = xc + cw_ref[CONV_WIDTH - 1 - k:CONV_WIDTH - k, :] * xs_ref[pl.ds(8 - k, tc), :]
    xs_ref[0:8, :] = xs_ref[tc:tc + 8, :]

    xcb = xc.astype(BF16)
    r = jax.nn.sigmoid(_dot(xcb, wa_ref[...]) + ba_ref[...])
    gi = jax.nn.sigmoid(_dot(xcb, wx_ref[...]) + bx_ref[...])
    lam = lam_ref[...]
    softplus_neg = jnp.maximum(-lam, 0.0) + jnp.log1p(jnp.exp(-jnp.abs(lam)))
    log_a = -LRU_C * r * softplus_neg
    a = jnp.exp(log_a)
    b = jnp.sqrt(1.0 - jnp.exp(2.0 * log_a)) * (gi * xc)

    row = lax.broadcasted_iota(jnp.int32, (tc, 1), 0)
    s = 1
    while s < tc:
        valid = row >= s
        b = jnp.where(valid, a * pltpu.roll(b, s, 0) + b, b)
        a = jnp.where(valid, a * pltpu.roll(a, s, 0), a)
        s *= 2
    h = b + a * h_ref[0:1, :]
    h_ref[0:1, :] = h[tc - 1:tc, :]
    o_ref[...] = _rms(h * jax.nn.gelu(xg_ref[...]), gn_ref[...]).astype(BF16)


def _rglru(xrg, cw, cb, wa, ba, wx, bx, lam, gn, batch, seq, tc):
    nt = seq // tc
    w = cw.shape[1]
    vec = pl.BlockSpec((1, w), lambda b, i: (0, 0))
    return pl.pallas_call(
        _rglru_kernel,
        grid=(batch, nt),
        in_specs=[
            pl.BlockSpec((tc, w), lambda b, i: (b * nt + i, 0)),
            pl.BlockSpec((tc, w), lambda b, i: (b * nt + i, 1)),
            pl.BlockSpec((CONV_WIDTH, w), lambda b, i: (0, 0)),
            vec,
            pl.BlockSpec((w, w), lambda b, i: (0, 0)), vec,
            pl.BlockSpec((w, w), lambda b, i: (0, 0)), vec,
            vec, vec,
        ],
        out_specs=pl.BlockSpec((tc, w), lambda b, i: (b * nt + i, 0)),
        out_shape=jax.ShapeDtypeStruct((batch * seq, w), BF16),
        scratch_shapes=[pltpu.VMEM((tc + 8, w), F32), pltpu.VMEM((8, w), F32)],
        compiler_params=_params(("parallel", "arbitrary")),
        name="rglru",
    )(xrg, xrg, cw, cb, wa, ba, wx, bx, lam, gn)


def _out_proj_kernel(a_ref, l_ref, x_ref, ga_ref, woa_ref, wol_ref, gf_ref, wq_ref, sk_ref,
                     h1_ref, xt_ref, st_ref):
    an = _rms(a_ref[...], ga_ref[...]).astype(BF16)
    h1 = x_ref[...] + _dot(an, woa_ref[...]) + _dot(l_ref[...], wol_ref[...])
    h1_ref[...] = h1
    xn = _rms(h1, gf_ref[...])
    xt_ref[...] = xn.T.astype(BF16)
    q = _dot(xn.astype(BF16), wq_ref[...])
    for hp in range(2 * PEER_HEADS):
        qs = q[:, LANES * hp:LANES * (hp + 1)].astype(BF16)
        st_ref[hp] = _dot_nt(sk_ref[hp % 2], qs)


def _out_proj(a_out, l_n, x2, ga, woa, wol, gf, wq, sk, tm):
    n, d = x2.shape
    full = lambda arr: pl.BlockSpec(arr.shape, lambda i: (0,) * arr.ndim)
    return pl.pallas_call(
        _out_proj_kernel,
        grid=(n // tm,),
        in_specs=[
            pl.BlockSpec((tm, a_out.shape[1]), lambda i: (i, 0)),
            pl.BlockSpec((tm, l_n.shape[1]), lambda i: (i, 0)),
            pl.BlockSpec((tm, d), lambda i: (i, 0)),
            full(ga), full(woa), full(wol), full(gf), full(wq), full(sk),
        ],
        out_specs=[
            pl.BlockSpec((tm, d), lambda i: (i, 0)),
            pl.BlockSpec((d, tm), lambda i: (0, i)),
            pl.BlockSpec((2 * PEER_HEADS, PEER_N_KEYS, tm), lambda i: (0, 0, i)),
        ],
        out_shape=[
            jax.ShapeDtypeStruct((n, d), F32),
            jax.ShapeDtypeStruct((d, n), BF16),
            jax.ShapeDtypeStruct((2 * PEER_HEADS, PEER_N_KEYS, n), F32),
        ],
        compiler_params=_params(("parallel",)),
        name="out_proj",
    )(a_out, l_n, x2, ga, woa, wol, gf, wq, sk)


N_TOP = PEER_TOPK
_PAIRS = [(i, j) for i in range(N_TOP) for j in range(N_TOP) if (i + 1) * (j + 1) <= N_TOP]


def _top_values(s):
    n, tl = s.shape
    parts = [s[SUBLANES * k:SUBLANES * (k + 1)] for k in range(n // SUBLANES)]
    base = lax.broadcasted_iota(jnp.int32, (SUBLANES, tl), 0).astype(F32)
    rows = [base + float(SUBLANES * k) for k in range(len(parts))]
    out = []
    for r in range(N_TOP):
        m = jnp.max(functools.reduce(jnp.maximum, parts), axis=0, keepdims=True)
        out.append(m)
        first = jnp.full((SUBLANES, tl), float(n), F32)
        for k in reversed(range(len(parts))):
            first = jnp.where(parts[k] == m, rows[k], first)
        first = jnp.min(first, axis=0, keepdims=True)
        taken = TAKEN * (1.0 + r / TAKEN_STEPS)
        parts = [jnp.where(rows[k] == first, taken, parts[k]) for k in range(len(parts))]
    rank = jnp.concatenate(parts, axis=0)
    rank = jnp.where(rank <= TAKEN, (rank * (1.0 / TAKEN) - 1.0) * TAKEN_STEPS, float(PEER_TOPK))
    return out, rank


def _peer_stats_kernel(st_ref, r2_ref, e2_ref, l_ref, e1_ref):
    tops = [[], []]
    rank1 = []
    for h in range(PEER_HEADS):
        vals, rank = _top_values(st_ref[2 * h])
        tops[0].append(vals)
        rank1.append(rank)
        vals, rank = _top_values(st_ref[2 * h + 1])
        tops[1].append(vals)
        r2_ref[h] = rank.astype(BF16)
    a = [jnp.concatenate([tops[0][h][i] for h in range(PEER_HEADS)], axis=0) for i in range(N_TOP)]
    b = [jnp.concatenate([tops[1][h][j] for h in range(PEER_HEADS)], axis=0) for j in range(N_TOP)]
    cand = [a[i] + b[j] for (i, j) in _PAIRS]
    work = list(cand)
    kth = []
    for _ in range(N_TOP):
        m = functools.reduce(jnp.maximum, work)
        kth.append(m)
        found = jnp.zeros(m.shape, jnp.bool_)
        for k in range(len(work)):
            hit = (work[k] == m) & jnp.logical_not(found)
            work[k] = jnp.where(hit, REMOVED, work[k])
            found = found | hit
    thr = kth[PEER_TOPK - 1]
    top = a[0] + b[0]
    z = functools.reduce(lambda u, w: u + w, [jnp.where(c >= thr, jnp.exp(c - top), 0.0) for c in cand])
    n_sel = [functools.reduce(lambda u, w: u + w,
                              [(a[r] + b[j] >= thr).astype(F32) for j in range(PEER_TOPK)])
             for r in range(PEER_TOPK)]
    for h in range(PEER_HEADS):
        row = slice(h, h + 1)
        prefix = jnp.zeros(rank1[h].shape, F32)
        for r in range(PEER_TOPK):
            prefix = jnp.where(rank1[h] == float(r), n_sel[r][row], prefix)
        l_ref[h] = prefix
        e1_ref[h] = jnp.exp(st_ref[2 * h] - a[0][row])
        e2_ref[h] = (jnp.exp(st_ref[2 * h + 1] - b[0][row]) / z[row]).astype(BF16)


def _peer_stats(st, tl):
    hp, nk, n = st.shape
    out_spec = pl.BlockSpec((PEER_HEADS, nk, tl), lambda i: (0, 0, i))
    return pl.pallas_call(
        _peer_stats_kernel,
        grid=(n // tl,),
        in_specs=[pl.BlockSpec((hp, nk, tl), lambda i: (0, 0, i))],
        out_specs=[out_spec] * 4,
        out_shape=[
            jax.ShapeDtypeStruct((PEER_HEADS, nk, n), BF16),
            jax.ShapeDtypeStruct((PEER_HEADS, nk, n), BF16),
            jax.ShapeDtypeStruct((PEER_HEADS, nk, n), F32),
            jax.ShapeDtypeStruct((PEER_HEADS, nk, n), F32),
        ],
        compiler_params=_params(("parallel",)),
        name="peer_stats",
    )(st)


def _peer_main_kernel(u_ref, vt_ref, xt_ref, r2_ref, e2_ref, l_ref, e1_ref, h1_ref, o_ref, acc_ref):
    c = pl.program_id(1)
    ce = u_ref.shape[0]
    tm = xt_ref.shape[1]
    nk = PEER_N_KEYS

    @pl.when(c == 0)
    def _():
        acc_ref[...] = jnp.zeros(acc_ref.shape, F32)

    act = _gelu_tanh(_dot(u_ref[...], xt_ref[...]).astype(BF16))

    def rows_bf16(ref, h, i1):
        packed = jnp.broadcast_to(ref[h, pl.ds(i1, 1), :], (BF16_ROWS, tm)).astype(BF16)
        return jnp.concatenate([packed] * (nk // BF16_ROWS), axis=0)

    w_parts = []
    for ii in range(ce // nk):
        i1 = c * (ce // nk) + ii
        g = None
        for h in range(PEER_HEADS):
            e2 = e2_ref[h]
            term = jnp.where(r2_ref[h] < rows_bf16(l_ref, h, i1), e2, jnp.zeros_like(e2)) * rows_bf16(e1_ref, h, i1)
            g = term if g is None else g + term
        w_parts.append(act[ii * nk:(ii + 1) * nk, :] * g)
    w = jnp.concatenate(w_parts, axis=0)
    acc_ref[...] += _dot(vt_ref[...], w)

    @pl.when(c == pl.num_programs(1) - 1)
    def _():
        o_ref[...] = h1_ref[...] + acc_ref[...].T


def _peer_main(u_b, vt_b, xt, r2, e2, n_sel, e1, h1, tm, ce):
    ne, d = u_b.shape
    n = xt.shape[1]
    per_tok = pl.BlockSpec((PEER_HEADS, PEER_N_KEYS, tm), lambda t, c: (0, 0, t))
    return pl.pallas_call(
        _peer_main_kernel,
        grid=(n // tm, ne // ce),
        in_specs=[
            pl.BlockSpec((ce, d), lambda t, c: (c, 0)),
            pl.BlockSpec((d, ce), lambda t, c: (0, c)),
            pl.BlockSpec((d, tm), lambda t, c: (0, t)),
            per_tok, per_tok, per_tok, per_tok,
            pl.BlockSpec((tm, d), lambda t, c: (t, 0)),
        ],
        out_specs=pl.BlockSpec((tm, d), lambda t, c: (t, 0)),
        out_shape=jax.ShapeDtypeStruct((n, d), F32),
        scratch_shapes=[pltpu.VMEM((d, tm), F32)],
        compiler_params=_params(("parallel", "arbitrary")),
        name="peer_main",
    )(u_b, vt_b, xt, r2, e2, n_sel, e1, h1)


def _ple_final_kernel(h_ref, p_ref, gp_ref, wg_ref, bg_ref, wp_ref, gfin_ref, o_ref):
    h = h_ref[...]
    gate = jax.nn.sigmoid(_dot(_rms(h, gp_ref[...]).astype(BF16), wg_ref[...]) + bg_ref[...])
    h = h + gate * _dot(p_ref[...].astype(BF16), wp_ref[...])
    o_ref[...] = _rms(h, gfin_ref[...])


def _ple_final(h2, p2, gp, wg, bg, wp, gfin, tm):
    n, d = h2.shape
    full = lambda arr: pl.BlockSpec(arr.shape, lambda i: (0,) * arr.ndim)
    return pl.pallas_call(
        _ple_final_kernel,
        grid=(n // tm,),
        in_specs=[
            pl.BlockSpec((tm, d), lambda i: (i, 0)),
            pl.BlockSpec((tm, p2.shape[1]), lambda i: (i, 0)),
            full(gp), full(wg), full(bg), full(wp), full(gfin),
        ],
        out_specs=pl.BlockSpec((tm, d), lambda i: (i, 0)),
        out_shape=jax.ShapeDtypeStruct((n, d), F32),
        compiler_params=_params(("parallel",)),
        name="ple_final",
    )(h2, p2, gp, wg, bg, wp, gfin)


def _overlap_t(seq, n_rows):
    n_cmp = n_rows - 1
    cmp_start = np.arange(n_cmp) * CMP_STRIDE
    sel_start = np.arange(seq // SEL_LEN) * SEL_LEN
    ov = np.clip(np.minimum(cmp_start[:, None] + CMP_LEN, sel_start[None, :] + SEL_LEN)
                 - np.maximum(cmp_start[:, None], sel_start[None, :]), 0, None).astype(np.float32) / CMP_LEN
    out = np.zeros((seq // SEL_LEN, n_rows), np.float32)
    out[:, :n_cmp] = ov.T
    return out


def _block_expand(seq, starts, width):
    pos = np.asarray(starts).reshape(-1, 1, 1) + np.arange(width).reshape(1, 1, width)
    blk = np.where(pos >= 0, pos // SEL_LEN, -1)
    return (blk == np.arange(seq // SEL_LEN).reshape(1, -1, 1)).astype(np.float32)


def _block_diag(w):
    nb, bd, _ = w.shape
    out = jnp.zeros((nb * bd, nb * bd), w.dtype)
    for j in range(nb):
        out = out.at[j * bd:(j + 1) * bd, j * bd:(j + 1) * bd].set(w[j])
    return out


def _layer(h, p_i, attn_norm, w_in, cmp_k_pe, cmp_k_w1, cmp_k_w2, cmp_v_pe, cmp_v_w1, cmp_v_w2,
           rel_table, conv_w, conv_b, lru_wa, lru_ba, lru_wx, lru_bx, lru_lambda,
           grp_norm_attn, grp_norm_lru, w_out, ffn_norm, peer_wq, peer_subkeys, peer_u, peer_v,
           ple_norm, ple_wgate, ple_bgate, ple_proj, *, tm, tc, tl, ce):
    batch, seq, d = h.shape
    n = batch * seq
    x2 = h.reshape(n, d)
    aw = N_HEADS * HEAD_DIM
    kvw = N_KV_HEADS * HEAD_DIM
    lw = d - aw

    o = 0
    cols = {}
    for name, width in (("q", aw), ("kc", kvw), ("vc", kvw), ("ks", kvw), ("vs", kvw), ("kw", kvw),
                        ("vw", kvw), ("gates", N_HEADS * 3), ("xr", lw), ("xg", lw)):
        cols[name] = w_in[:, o:o + width]
        o += width
    gate_pad = [jnp.pad(cols["gates"][:, N_GATE_COLS * k:N_GATE_COLS * (k + 1)], ((0, 0), (0, LANES - N_GATE_COLS)))
                for k in range(N_KV_HEADS)]
    wf = jnp.concatenate([cols["xr"], cols["xg"], cols["kc"], cols["vc"]] + gate_pad, axis=1).astype(BF16)
    wb = jnp.concatenate([cols["q"], cols["ks"], cols["kw"], cols["vs"], cols["vw"]], axis=1).astype(BF16)

    xrg, kvc, gates2, q4, kk, vv = _in_proj(x2, attn_norm.reshape(1, d), wf, wb, tm)

    n_rows = seq // CMP_STRIDE
    r4 = kvc.reshape(4, batch, n_rows, CMP_STRIDE * HEAD_DIM)
    pe2 = jnp.stack([cmp_k_pe.reshape(1, -1), cmp_v_pe.reshape(1, -1)])
    w1s = jnp.stack([cmp_k_w1, cmp_v_w1]).astype(BF16)
    w2s = jnp.stack([cmp_k_w2, cmp_v_w2]).astype(BF16)
    cmp4 = _compress(r4, pe2, w1s, w2s)

    winadd, seladd = _bias_near(rel_table)
    addc = _bias_cmp(rel_table, seq, n_rows, min(seq, 512))
    ovlt = jnp.asarray(_overlap_t(seq, n_rows), BF16)
    o_c, sel = _cmp_attn(q4, cmp4, addc, ovlt, batch, seq)

    nt = seq // ATT_TILE
    tile_starts = np.arange(nt) * ATT_TILE
    enear = jnp.asarray(_block_expand(seq, tile_starts - ATT_TILE, NEAR_SPAN), BF16)
    eslab = jnp.asarray(_block_expand(seq, np.arange(max(seq // FAR_SLAB, 1)) * FAR_SLAB, FAR_SLAB), BF16)
    front = ((0, 0), (0, 0), (WINDOW, 0), (0, 0))
    kk5 = jnp.pad(kk.reshape(4, batch, seq, HEAD_DIM), front)
    vv5 = jnp.pad(vv.reshape(4, batch, seq, LANES), front)
    a_out = _sel_win_attn(q4, kk5, vv5, sel, enear, eslab, winadd, seladd, o_c, gates2, batch, seq)

    l_n = _rglru(xrg, conv_w.reshape(CONV_WIDTH, lw), conv_b.reshape(1, lw),
                 _block_diag(lru_wa).astype(BF16), lru_ba.reshape(1, lw),
                 _block_diag(lru_wx).astype(BF16), lru_bx.reshape(1, lw),
                 lru_lambda.reshape(1, lw), grp_norm_lru.reshape(1, lw), batch, seq, tc)

    w_out_b = w_out.astype(BF16)
    h1, xt, st = _out_proj(a_out, l_n, x2, grp_norm_attn.reshape(1, aw), w_out_b[:aw], w_out_b[aw:],
                           ffn_norm.reshape(1, d), peer_wq.astype(BF16), peer_subkeys.astype(BF16), tm)
    r2, e2, n_sel, e1 = _peer_stats(st, tl)
    h2 = _peer_main(peer_u.astype(BF16), peer_v.T.astype(BF16), xt, r2, e2, n_sel, e1, h1, tm, ce)
    return (h2, p_i.reshape(n, -1), ple_norm.reshape(1, d), ple_wgate.astype(BF16), ple_bgate.reshape(1, d),
            ple_proj.astype(BF16))


def kernel(x, p, attn_norm, w_in, cmp_k_pe, cmp_k_w1, cmp_k_w2, cmp_v_pe, cmp_v_w1, cmp_v_w2, rel_table, conv_w, conv_b, lru_wa, lru_ba, lru_wx, lru_bx, lru_lambda, grp_norm_attn, grp_norm_lru, w_out, ffn_norm, peer_wq, peer_subkeys, peer_u, peer_v, ple_norm, ple_wgate, ple_bgate, ple_proj, final_norm):
    batch, seq, d = x.shape
    assert w_in.shape[0] == 1, "single-layer stack only"
    tm = min(512, seq)
    h2, p2, gp, wg, bg, wp = _layer(
        x, p[0], attn_norm[0], w_in[0], cmp_k_pe[0], cmp_k_w1[0], cmp_k_w2[0], cmp_v_pe[0], cmp_v_w1[0],
        cmp_v_w2[0], rel_table, conv_w[0], conv_b[0], lru_wa[0], lru_ba[0], lru_wx[0], lru_bx[0],
        lru_lambda[0], grp_norm_attn[0], grp_norm_lru[0], w_out[0], ffn_norm[0], peer_wq[0],
        peer_subkeys[0], peer_u[0], peer_v[0], ple_norm[0], ple_wgate[0], ple_bgate[0], ple_proj[0],
        tm=tm, tc=min(256, seq), tl=min(512, seq), ce=2048)
    out = _ple_final(h2, p2, gp, wg, bg, wp, final_norm.reshape(1, d), tm)
    return out.reshape(batch, seq, d)
```
